```python
import jax, jax.numpy as jnp
from jax import lax
import numpy as np

D_MODEL = 1024
BATCH = 16
SEQ = 256
DEPTH = 4
DEC_BATCH = 4
DEC_SEQ = 2048
PAST_LEN = 256

GRID_W = 64
EPS = 1e-6
GLA_DK = 64
GLA_DV = 64
GLA_WIDTH = 3 * D_MODEL // 8
GLA_HEADS = GLA_WIDTH // GLA_DV
GLA_LOWRANK = 16
GLA_TAU = 16.0
GLA_CHUNK = 64
CONV_WIDTH = D_MODEL // 4
CONV_K = 3
NA_HD = 64
NA_WIDTH = 3 * D_MODEL // 8
NA_HEADS = NA_WIDTH // NA_HD
NA_MAX_ROWS = 8
NA_COLS = 16
CTX_BLOCK = 128
MIX_WIDTH = GLA_WIDTH + CONV_WIDTH + NA_WIDTH
PEER_HEADS = 8
PEER_NKEYS = 128
PEER_EXPERTS = PEER_NKEYS * PEER_NKEYS
PEER_DKEY = 128
PEER_TOPK = 16
PEER_CHUNK = 128

SPLIT_SIZES = (GLA_HEADS * GLA_DK, GLA_HEADS * GLA_DK, GLA_WIDTH, GLA_WIDTH,
               GLA_LOWRANK, GLA_LOWRANK,
               CONV_WIDTH, CONV_WIDTH, CONV_WIDTH,
               NA_WIDTH, NA_WIDTH, NA_WIDTH)
IN_WIDTH = 4 * GLA_WIDTH + 2 * GLA_LOWRANK + 3 * CONV_WIDTH + 3 * NA_WIDTH

kernel_name = "hymba_gla_conv_natten_peer_diffusion_step"


def rmsnorm(x, w):
    xf = x.astype(jnp.float32)
    y = xf * lax.rsqrt(jnp.mean(xf * xf, axis=-1, keepdims=True) + EPS) * w.astype(jnp.float32)
    return y.astype(x.dtype)


def modulate(cvec, w_ada, b_ada):
    m = (jax.nn.silu(cvec) @ w_ada + b_ada)[:, None, :]
    return jnp.split(m, 6, axis=-1)


def heads(a, n_heads):
    b, t, w = a.shape
    return a.reshape(b, t, n_heads, w // n_heads).transpose(0, 2, 1, 3)


def merge_heads(a):
    b, h, t, d = a.shape
    return a.transpose(0, 2, 1, 3).reshape(b, t, h * d)


def split_proj(h, w_in):
    z = h @ w_in
    cuts = [int(i) for i in np.cumsum(SPLIT_SIZES)[:-1]]
    return jnp.split(z, cuts, axis=-1)


def gla_chunked(q, k, v, log_a, s0):
    f32 = jnp.float32
    q, k, v, log_a = q.astype(f32), k.astype(f32), v.astype(f32), log_a.astype(f32)
    b_, h_, t_, _ = q.shape
    nc = t_ // GLA_CHUNK

    def to_chunks(a):
        return a.reshape(b_, h_, nc, GLA_CHUNK, a.shape[-1]).transpose(2, 0, 1, 3, 4)

    causal = jnp.tril(jnp.ones((GLA_CHUNK, GLA_CHUNK), dtype=bool))[:, :, None]

    def step(s, inp):
        qc, kc, vc, ac = inp
        cum = jnp.cumsum(ac, axis=-2)
        inter = jnp.einsum('bhtk,bhkv->bhtv', qc * jnp.exp(cum), s)
        diff = cum[:, :, :, None, :] - cum[:, :, None, :, :]
        decay = jnp.exp(jnp.where(causal, diff, -jnp.inf))
        att = jnp.einsum('bhtk,bhsk,bhtsk->bhts', qc, kc, decay)
        intra = jnp.einsum('bhts,bhsv->bhtv', att, vc)
        last = cum[:, :, -1:, :]
        s_new = jnp.exp(last[:, :, 0, :])[..., None] * s + jnp.einsum(
            'bhsk,bhsv->bhkv', kc * jnp.exp(last - cum), vc)
        return s_new, inter + intra

    s_fin, o = lax.scan(step, s0.astype(f32), (to_chunks(q), to_chunks(k), to_chunks(v), to_chunks(log_a)))
    o = o.transpose(1, 2, 0, 3, 4).reshape(b_, h_, t_, v.shape[-1])
    return o, s_fin


def gla_bidir(q, k, v, la_f, la_b, s0_f, s0_b):
    o_f, s_f = gla_chunked(q, k, v, la_f, s0_f)
    flip = lambda a: a[:, :, ::-1]
    o_b, s_b = gla_chunked(flip(q), flip(k), flip(v), flip(la_b), s0_b)
    return o_f + flip(o_b), s_f, s_b


def log_decay(lr, w_a, b_a):
    return jax.nn.log_sigmoid((lr @ w_a + b_a).astype(jnp.float32)) / GLA_TAU


def gla_output(o, g, w):
    o = o.transpose(0, 2, 1, 3)
    o = o * lax.rsqrt(jnp.mean(o * o, axis=-1, keepdims=True) + EPS) * w.astype(jnp.float32).reshape(GLA_HEADS, GLA_DV)
    b_, t_ = o.shape[0], o.shape[1]
    return o.reshape(b_, t_, GLA_WIDTH).astype(g.dtype) * jax.nn.silu(g)


def short_conv(u, w, b):
    up = jnp.pad(u, ((0, 0), (1, 1), (0, 0)))
    return w[0] * up[:, :-2] + w[1] * up[:, 1:-1] + w[2] * up[:, 2:] + b


def dense_attention(q, k, v):
    b_, h_, s_, d_ = q.shape
    nb = s_ // CTX_BLOCK
    qb = q.reshape(b_, h_, nb, CTX_BLOCK, d_).transpose(2, 0, 1, 3, 4)
    scale = d_ ** -0.5

    def blk(qi):
        s = jnp.einsum('bhqd,bhkd->bhqk', qi, k).astype(jnp.float32) * scale
        p = jax.nn.softmax(s, axis=-1).astype(v.dtype)
        return jnp.einsum('bhqk,bhkd->bhqd', p, v)

    o = lax.map(blk, qb)
    return o.transpose(1, 2, 0, 3, 4).reshape(b_, h_, s_, d_)


def neighbourhood_attention(q, k, v, k_ctx, v_ctx, rpb):
    b_, h_, t_, d_ = q.shape
    rows = t_ // GRID_W
    wr = min(NA_MAX_ROWS, rows)
    nwin = wr * GRID_W
    r = np.arange(rows)
    rs = np.clip(r - wr // 2, 0, rows - wr)
    key_rows = rs[:, None] + np.arange(wr)[None, :]
    cols = np.arange(GRID_W)
    cs = np.clip(cols - NA_COLS // 2, 0, GRID_W - NA_COLS)
    kcol = np.arange(nwin) % GRID_W
    col_mask = (kcol[None, :] >= cs[:, None]) & (kcol[None, :] < cs[:, None] + NA_COLS)
    dr = np.repeat(key_rows - r[:, None], GRID_W, axis=1)
    dc = np.clip(kcol[None, :] - cols[:, None], -(NA_COLS - 1), NA_COLS - 1)
    bias = rpb[:, dr[:, None, :] + NA_MAX_ROWS - 1, dc[None, :, :] + NA_COLS - 1].astype(jnp.float32)

    qg = q.reshape(b_, h_, rows, GRID_W, d_)
    kw = k.reshape(b_, h_, rows, GRID_W, d_)[:, :, key_rows].reshape(b_, h_, rows, nwin, d_)
    vw = v.reshape(b_, h_, rows, GRID_W, d_)[:, :, key_rows].reshape(b_, h_, rows, nwin, d_)
    scale = d_ ** -0.5
    s_win = jnp.einsum('bhrqd,bhrkd->bhrqk', qg, kw).astype(jnp.float32) * scale + bias
    s_win = jnp.where(col_mask[None, None, None], s_win, -1e30)
    s_ctx = jnp.einsum('bhrqd,bhcd->bhrqc', qg, k_ctx).astype(jnp.float32) * scale
    p = jax.nn.softmax(jnp.concatenate([s_win, s_ctx], axis=-1), axis=-1).astype(v.dtype)
    o = (jnp.einsum('bhrqk,bhrkd->bhrqd', p[..., :nwin], vw)
         + jnp.einsum('bhrqc,bhcd->bhrqd', p[..., nwin:], v_ctx))
    return o.reshape(b_, h_, t_, d_)


def mix_layer(h, lp, s0_f, s0_b, kv_ctx):
    qa, ka, va, ga, lrf, lrb, ch, cb, cc, qn, kn, vn = split_proj(h, lp['w_in'])
    q = heads(qa, GLA_HEADS) * GLA_DK ** -0.5
    k = heads(ka, GLA_HEADS)
    v = heads(va, GLA_HEADS)
    la_f = heads(log_decay(lrf, lp['w_af'], lp['b_af']), GLA_HEADS)
    la_b = heads(log_decay(lrb, lp['w_ab'], lp['b_ab']), GLA_HEADS)
    o_gla, s_f, s_b = gla_bidir(q, k, v, la_f, la_b, s0_f, s0_b)
    y_gla = gla_output(o_gla, ga, lp['gla_norm_w'])
    y_conv = cb * short_conv(cc * ch, lp['conv_w'], lp['conv_b'])
    qh, kh, vh = heads(qn, NA_HEADS), heads(kn, NA_HEADS), heads(vn, NA_HEADS)
    if kv_ctx is None:
        o_na = dense_attention(qh, kh, vh)
    else:
        o_na = neighbourhood_attention(qh, kh, vh, kv_ctx[0], kv_ctx[1], lp['na_rpb'])
    y = jnp.concatenate([y_gla, y_conv, merge_heads(o_na)], axis=-1) @ lp['w_out']
    return y, (kh, vh, s_f, s_b)


def peer_ffn(h, wq, k1, k2, u, v):
    b_, t_, d_ = h.shape
    n = b_ * t_
    xt = h.reshape(n, d_)
    q = (xt @ wq).reshape(n, PEER_HEADS, PEER_DKEY).astype(jnp.float32)
    half = PEER_DKEY // 2
    s1 = jnp.einsum('thd,hnd->thn', q[..., :half], k1.astype(jnp.float32))
    s2 = jnp.einsum('thd,hnd->thn', q[..., half:], k2.astype(jnp.float32))
    v1, i1 = lax.top_k(s1, PEER_TOPK)
    v2, i2 = lax.top_k(s2, PEER_TOPK)
    cand = (v1[..., :, None] + v2[..., None, :]).reshape(n, PEER_HEADS, PEER_TOPK * PEER_TOPK)
    cidx = (i1[..., :, None] * PEER_NKEYS + i2[..., None, :]).reshape(n, PEER_HEADS, PEER_TOPK * PEER_TOPK)
    top, pos = lax.top_k(cand, PEER_TOPK)
    idx = jnp.take_along_axis(cidx, pos, axis=-1)
    g = jax.nn.softmax(top, axis=-1)
    nc = n // PEER_CHUNK

    def expert_block(args):
        xc, ic, gc = args
        act = jax.nn.gelu(jnp.einsum('td,thkd->thk', xc, u[ic]))
        w = (gc * act.astype(jnp.float32)).astype(xc.dtype)
        return jnp.einsum('thk,thkd->td', w, v[ic])

    out = lax.map(expert_block, (xt.reshape(nc, PEER_CHUNK, d_),
                                 idx.reshape(nc, PEER_CHUNK, PEER_HEADS, PEER_TOPK),
                                 g.reshape(nc, PEER_CHUNK, PEER_HEADS, PEER_TOPK)))
    return out.reshape(b_, t_, d_)


def trunk_layer(x, cvec, lp, s0_f, s0_b, kv_ctx):
    sh1, sc1, g1, sh2, sc2, g2 = modulate(cvec, lp['w_ada'], lp['b_ada'])
    h = rmsnorm(x, lp['norm1_w']) * (1 + sc1) + sh1
    y, st = mix_layer(h, lp, s0_f, s0_b, kv_ctx)
    x = x + g1 * y
    h = rmsnorm(x, lp['norm2_w']) * (1 + sc2) + sh2
    x = x + g2 * peer_ffn(h, lp['peer_wq'], lp['peer_k1'], lp['peer_k2'], lp['peer_u'], lp['peer_v'])
    return x, st


def setup_inputs(seed: int = 0) -> dict:
    key = jax.random.key(seed)
    ks = jax.random.split(key, 32)
    nrm = lambda i, shape, s: jax.random.normal(ks[i], shape, jnp.float32) * s
    D = D_MODEL
    return {
        'x_prompt': nrm(0, (BATCH, SEQ, D), 1.0),
        'x_sample': nrm(1, (DEC_BATCH, DEC_SEQ, D), 1.0),
        'cache_na_k': nrm(2, (DEC_BATCH, DEPTH, NA_HEADS, PAST_LEN, NA_HD), 1.0),
        'cache_na_v': nrm(3, (DEC_BATCH, DEPTH, NA_HEADS, PAST_LEN, NA_HD), 1.0),
        'state_gla_fwd': nrm(4, (DEC_BATCH, DEPTH, GLA_HEADS, GLA_DK, GLA_DV), 1.0),
        'state_gla_bwd': nrm(5, (DEC_BATCH, DEPTH, GLA_HEADS, GLA_DK, GLA_DV), 1.0),
        'c': nrm(6, (DEC_BATCH, D), 1.0),
        'c_ctx': nrm(7, (D,), 1.0),
        'w_ada': nrm(8, (DEPTH, D, 6 * D), 0.5 * D ** -0.5),
        'b_ada': nrm(9, (DEPTH, 6 * D), 0.01),
        'norm1_w': 1.0 + nrm(10, (DEPTH, D), 0.01),
        'norm2_w': 1.0 + nrm(11, (DEPTH, D), 0.01),
        'w_in': nrm(12, (DEPTH, D, IN_WIDTH), D ** -0.5),
        'w_af': nrm(13, (DEPTH, GLA_LOWRANK, GLA_HEADS * GLA_DK), GLA_LOWRANK ** -0.5),
        'b_af': nrm(14, (DEPTH, GLA_HEADS * GLA_DK), 0.1),
        'w_ab': nrm(15, (DEPTH, GLA_LOWRANK, GLA_HEADS * GLA_DK), GLA_LOWRANK ** -0.5),
        'b_ab': nrm(16, (DEPTH, GLA_HEADS * GLA_DK), 0.1),
        'gla_norm_w': 1.0 + nrm(17, (DEPTH, GLA_WIDTH), 0.01),
        'conv_w': nrm(18, (DEPTH, CONV_K, CONV_WIDTH), CONV_K ** -0.5),
        'conv_b': nrm(19, (DEPTH, CONV_WIDTH), 0.01),
        'na_rpb': nrm(20, (DEPTH, NA_HEADS, 2 * NA_MAX_ROWS - 1, 2 * NA_COLS - 1), 0.1),
        'w_out': nrm(21, (DEPTH, MIX_WIDTH, D), MIX_WIDTH ** -0.5),
        'peer_wq': nrm(22, (DEPTH, D, PEER_HEADS * PEER_DKEY), D ** -0.5),
        'peer_k1': nrm(23, (DEPTH, PEER_HEADS, PEER_NKEYS, PEER_DKEY // 2), (PEER_DKEY // 2) ** -0.5),
        'peer_k2': nrm(24, (DEPTH, PEER_HEADS, PEER_NKEYS, PEER_DKEY // 2), (PEER_DKEY // 2) ** -0.5),
        'peer_u': nrm(25, (DEPTH, PEER_EXPERTS, D), D ** -0.5),
        'peer_v': nrm(26, (DEPTH, PEER_EXPERTS, D), 0.25),
        'final_norm_w': 1.0 + nrm(27, (D,), 0.01),
    }


def reference(x_prompt, x_sample, cache_na_k, cache_na_v, state_gla_fwd, state_gla_bwd, c, c_ctx,
              w_ada, b_ada, norm1_w, norm2_w, w_in, w_af, b_af, w_ab, b_ab, gla_norm_w,
              conv_w, conv_b, na_rpb, w_out, peer_wq, peer_k1, peer_k2, peer_u, peer_v, final_norm_w):
    x_p = x_prompt
    x_s = x_sample
    zeros_state = jnp.zeros((x_prompt.shape[0], GLA_HEADS, GLA_DK, GLA_DV), jnp.float32)
    new_k, new_v, new_sf, new_sb = [], [], [], []
    for l in range(DEPTH):
        lp = {'w_ada': w_ada[l], 'b_ada': b_ada[l], 'norm1_w': norm1_w[l], 'norm2_w': norm2_w[l],
              'w_in': w_in[l], 'w_af': w_af[l], 'b_af': b_af[l], 'w_ab': w_ab[l], 'b_ab': b_ab[l],
              'gla_norm_w': gla_norm_w[l], 'conv_w': conv_w[l], 'conv_b': conv_b[l],
              'na_rpb': na_rpb[l], 'w_out': w_out[l], 'peer_wq': peer_wq[l], 'peer_k1': peer_k1[l],
              'peer_k2': peer_k2[l], 'peer_u': peer_u[l], 'peer_v': peer_v[l]}
        x_p, (kc, vc, sf, sb) = trunk_layer(x_p, c_ctx[None, :], lp, zeros_state, zeros_state, None)
        new_k.append(kc)
        new_v.append(vc)
        new_sf.append(sf)
        new_sb.append(sb)
        x_s, _ = trunk_layer(x_s, c, lp, state_gla_fwd[:, l], state_gla_bwd[:, l],
                             (cache_na_k[:, l], cache_na_v[:, l]))
    y_prompt = rmsnorm(x_p, final_norm_w)
    y_sample = rmsnorm(x_s, final_norm_w)
    new_na_k = jnp.stack(new_k, axis=1)
    new_na_v = jnp.stack(new_v, axis=1)
    new_gla_fwd = jnp.stack(new_sf, axis=1)
    new_gla_bwd = jnp.stack(new_sb, axis=1)
    return (y_prompt, y_sample, new_na_k, new_na_v, new_gla_fwd, new_gla_bwd)
```

```python
import functools

import numpy as np
import jax
import jax.numpy as jnp
from jax import lax
from jax.experimental import pallas as pl
from jax.experimental.pallas import tpu as pltpu

F32 = jnp.float32
BF16 = jnp.bfloat16
HI = lax.Precision.HIGHEST
NT = (((1,), (1,)), ((), ()))
TN = (((0,), (0,)), ((), ()))

LANES = 128
EPS = 1e-6
D = 1024
HD = 64
PAIR = 2 * HD
GLA_W = 384
CONV_W = 256
NA_W = 384
LOWRANK = 16
GLA_CHUNK = 64
GLA_SUB = 16
GLA_TAU = 16.0
EXP_CLAMP = 80.0
GRID_W = 64
NA_ROWS = 8
NA_COLS = 16
PEER_HEADS = 8
NKEYS = 128
TOPK = 16
TM = 256
TT = 512
EXPERT_BLOCK = 2048
EXPERT_SUB = 256
ROUTE_HEADS_PER_STEP = 2
VMEM_LIMIT = 56 * 1024 * 1024


def _cparams(*sem):
    return pltpu.CompilerParams(dimension_semantics=sem, vmem_limit_bytes=VMEM_LIMIT)


def _silu(x):
    return x * jax.nn.sigmoid(x)


def _norm_mod(x, w, scale, shift):
    ms = jnp.mean(x * x, axis=-1, keepdims=True)
    return x * lax.rsqrt(ms + EPS) * w * (1.0 + scale) + shift


def _head_masks():
    lane = lax.broadcasted_iota(jnp.int32, (1, PAIR), 1)
    m0 = (lane < HD).astype(F32)
    return m0, 1.0 - m0


def _mod_kernel(c_ref, w_ref, b_ref, o_ref):
    s = _silu(c_ref[...]).astype(BF16)
    o_ref[0] = jnp.dot(s, w_ref[0].astype(BF16), preferred_element_type=F32) + b_ref[0]


def _modulation(cvec, w_ada, b_ada):
    depth = w_ada.shape[0]
    nb = w_ada.shape[2] // D
    out = pl.pallas_call(
        _mod_kernel,
        out_shape=jax.ShapeDtypeStruct((depth, 8, nb * D), F32),
        grid=(depth, nb),
        in_specs=[pl.BlockSpec((8, D), lambda l, j: (0, 0)),
                  pl.BlockSpec((1, D, D), lambda l, j: (l, 0, j)),
                  pl.BlockSpec((1, 1, D), lambda l, j: (l, 0, j))],
        out_specs=pl.BlockSpec((1, 8, D), lambda l, j: (l, 0, j)),
        compiler_params=_cparams("arbitrary", "arbitrary"),
        name="adaln_modulation",
    )(cvec, w_ada, b_ada.reshape(depth, 1, nb * D))
    return out.reshape(depth, 8, nb, D)


def _inproj_kernel(x_ref, mod_ref, nw_ref, wg_ref, wc_ref, wn_ref, wl_ref, wab_ref, bab_ref,
                   zg_ref, la_ref, zc_ref, zn_ref):
    h = _norm_mod(x_ref[...], nw_ref[...], mod_ref[0, 1:2, :], mod_ref[0, 0:1, :])
    hb = h.astype(BF16)
    zg_ref[...] = jnp.dot(hb, wg_ref[...], preferred_element_type=F32)
    zc_ref[...] = jnp.dot(hb, wc_ref[...], preferred_element_type=F32)
    zn_ref[...] = jnp.dot(hb, wn_ref[...], preferred_element_type=F32)
    lr = jnp.dot(hb, wl_ref[...], preferred_element_type=F32)
    zz = jnp.dot(lr, wab_ref[...], precision=HI, preferred_element_type=F32) + bab_ref[...]
    la_ref[...] = (jnp.minimum(zz, 0.0) - jnp.log(1.0 + jnp.exp(-jnp.abs(zz)))) * (1.0 / GLA_TAU)


def _const_spec(shape):
    return pl.BlockSpec(shape, lambda *_: (0,) * len(shape))


def _inproj(x, mod, mod_row, nw, wts):
    n = x.shape[0]
    wg, wc, wn, wl, wab, bab = wts
    row = lambda w: pl.BlockSpec((TM, w), lambda i: (i, 0))
    return pl.pallas_call(
        _inproj_kernel,
        out_shape=(jax.ShapeDtypeStruct((n, 4 * GLA_W), F32), jax.ShapeDtypeStruct((n, 2 * GLA_W), F32),
                   jax.ShapeDtypeStruct((n, 3 * CONV_W), F32), jax.ShapeDtypeStruct((n, 3 * NA_W), F32)),
        grid=(n // TM,),
        in_specs=[row(D), pl.BlockSpec((1, 6, D), lambda i: (mod_row(i), 0, 0)), _const_spec((1, D)),
                  _const_spec(wg.shape), _const_spec(wc.shape), _const_spec(wn.shape), _const_spec(wl.shape),
                  _const_spec(wab.shape), _const_spec(bab.shape)],
        out_specs=(row(4 * GLA_W), row(2 * GLA_W), row(3 * CONV_W), row(3 * NA_W)),
        compiler_params=_cparams("arbitrary"),
        name="in_projection",
    )(x, mod, nw, wg, wc, wn, wl, wab, bab)


def _gla_consts(fwd):
    c = GLA_CHUNK
    r = lax.broadcasted_iota(jnp.int32, (c, c), 0)
    s = lax.broadcasted_iota(jnp.int32, (c, c), 1)
    shift = GLA_SUB.bit_length() - 1
    blk_start = (r >> shift) << shift
    if fwd:
        caus = s <= r
        before = s < blk_start
    else:
        caus = s >= r
        before = s >= blk_start + GLA_SUB
    rowid = lax.broadcasted_iota(jnp.int32, (c, PAIR), 0)
    return caus, caus.astype(F32), before.astype(F32), rowid


def _gla_chunk(qc, kc, vc, lac, st, fwd, consts, masks, bd):
    caus, tri, before, rowid = consts
    m0, m1 = masks
    c, sb = GLA_CHUNK, GLA_SUB
    cum = jnp.dot(tri, lac, precision=HI, preferred_element_type=F32)
    bm = jnp.dot(before, lac, precision=HI, preferred_element_type=F32)
    last = cum[c - 1:c] if fwd else cum[0:1]
    qt = qc * jnp.exp(cum - bm)
    att0, att1 = [], []
    for i in range(c // sb):
        bi = bm[i * sb:i * sb + 1]
        seen = (rowid < (i + 1) * sb) if fwd else (rowid >= i * sb)
        kt = (kc * jnp.exp(jnp.where(seen, jnp.minimum(bi - cum, EXP_CLAMP), 0.0))).astype(BF16)
        qi = qt[i * sb:(i + 1) * sb]
        att0.append(lax.dot_general((qi * m0).astype(BF16), kt, NT, preferred_element_type=F32))
        att1.append(lax.dot_general((qi * m1).astype(BF16), kt, NT, preferred_element_type=F32))
    a0 = jnp.where(caus, jnp.concatenate(att0, axis=0), 0.0).astype(BF16)
    a1 = jnp.where(caus, jnp.concatenate(att1, axis=0), 0.0).astype(BF16)
    vb = vc.astype(BF16)
    intra = (jnp.dot(a0, vb, preferred_element_type=F32) * m0
             + jnp.dot(a1, vb, preferred_element_type=F32) * m1)
    inter = lax.dot_general((qc * jnp.exp(cum)).astype(BF16), st.astype(BF16), NT,
                            preferred_element_type=F32)
    khat = (kc * jnp.exp(last - cum)).astype(BF16)
    upd = lax.dot_general(vb, khat, TN, preferred_element_type=F32)
    return inter + intra, st * jnp.exp(last) + upd * bd


def _gla_kernel(*refs, seq, state_in):
    if state_in:
        q_ref, k_ref, v_ref, g_ref, laf_ref, lab_ref, nw_ref, s0f_ref, s0b_ref, y_ref, of_scr = refs
    else:
        q_ref, k_ref, v_ref, g_ref, laf_ref, lab_ref, nw_ref, y_ref, sf_ref, sb_ref, of_scr = refs
    c = GLA_CHUNK
    nc = seq // c
    masks = _head_masks()
    m0, m1 = masks
    rr = lax.broadcasted_iota(jnp.int32, (PAIR, PAIR), 0)
    cc = lax.broadcasted_iota(jnp.int32, (PAIR, PAIR), 1)
    bd = ((rr < HD) == (cc < HD)).astype(F32)
    cf = _gla_consts(True)
    cb = _gla_consts(False)
    scale = HD ** -0.5
    nw = nw_ref[...]

    def rows(i):
        return pl.ds(pl.multiple_of(i * c, c), c)

    def fwd_body(i, st):
        sl = rows(i)
        o, st = _gla_chunk(q_ref[sl, :] * scale, k_ref[sl, :], v_ref[sl, :], laf_ref[sl, :], st,
                           True, cf, masks, bd)
        of_scr[sl, :] = o
        return st

    def bwd_body(j, st):
        sl = rows(nc - 1 - j)
        o, st = _gla_chunk(q_ref[sl, :] * scale, k_ref[sl, :], v_ref[sl, :], lab_ref[sl, :], st,
                           False, cb, masks, bd)
        tot = of_scr[sl, :] + o
        sq = tot * tot
        ms = (jnp.sum(sq * m0, axis=-1, keepdims=True) * m0
              + jnp.sum(sq * m1, axis=-1, keepdims=True) * m1) * (1.0 / HD)
        y_ref[sl, :] = tot * lax.rsqrt(ms + EPS) * nw * _silu(g_ref[sl, :])
        return st

    zero = jnp.zeros((PAIR, PAIR), F32)
    sf = lax.fori_loop(0, nc, fwd_body, s0f_ref[0, 0] if state_in else zero)
    sb = lax.fori_loop(0, nc, bwd_body, s0b_ref[0, 0] if state_in else zero)
    if not state_in:
        sf_ref[0, 0] = sf
        sb_ref[0, 0] = sb


def _gla(zg, la, nw, batch, seq, states):
    n = zg.shape[0]
    npair = GLA_W // PAIR
    col = lambda off: pl.BlockSpec((seq, PAIR), lambda b, p: (b, off + p))
    st_spec = pl.BlockSpec((1, 1, PAIR, PAIR), lambda b, p: (b, p, 0, 0))
    in_specs = [col(0), col(npair), col(2 * npair), col(3 * npair), col(0), col(npair),
                pl.BlockSpec((1, PAIR), lambda b, p: (0, p))]
    args = [zg, zg, zg, zg, la, la, nw]
    y_shape = jax.ShapeDtypeStruct((n, GLA_W), F32)
    y_spec = pl.BlockSpec((seq, PAIR), lambda b, p: (b, p))
    if states is None:
        st_shape = jax.ShapeDtypeStruct((batch, npair, PAIR, PAIR), F32)
        out_shape, out_specs = (y_shape, st_shape, st_shape), (y_spec, st_spec, st_spec)
    else:
        in_specs += [st_spec, st_spec]
        args += list(states)
        out_shape, out_specs = y_shape, y_spec
    return pl.pallas_call(
        functools.partial(_gla_kernel, seq=seq, state_in=states is not None),
        out_shape=out_shape,
        grid=(batch, npair),
        in_specs=in_specs,
        out_specs=out_specs,
        scratch_shapes=[pltpu.VMEM((seq, PAIR), F32)],
        compiler_params=_cparams("arbitrary", "arbitrary"),
        name="gla_bidir",
    )(*args)


def _states_to_blockdiag(s):
    b, h = s.shape[:2]
    st = jnp.swapaxes(s, -1, -2).reshape(b, h // 2, 2, HD, HD)
    eye = jnp.eye(2, dtype=s.dtype)
    return jnp.einsum('bpivk,ij->bpivjk', st, eye).reshape(b, h // 2, PAIR, PAIR)


def _blockdiag_to_states(sbd):
    b, p = sbd.shape[:2]
    s6 = sbd.reshape(b, p, 2, HD, 2, HD)
    diag = jnp.stack([s6[:, :, 0, :, 0, :], s6[:, :, 1, :, 1, :]], axis=2)
    return jnp.swapaxes(diag, -1, -2).reshape(b, 2 * p, HD, HD)


def _conv_kernel(ch_ref, cb_ref, cc_ref, w_ref, b_ref, y_ref, *, seq):
    u = cc_ref[...] * ch_ref[...]
    row = lax.broadcasted_iota(jnp.int32, u.shape, 0)
    prev = jnp.where(row == 0, 0.0, pltpu.roll(u, 1, 0))
    nxt = jnp.where(row == seq - 1, 0.0, pltpu.roll(u, seq - 1, 0))
    y_ref[...] = cb_ref[...] * (w_ref[0:1, :] * prev + w_ref[1:2, :] * u + w_ref[2:3, :] * nxt + b_ref[...])


def _conv(zc, w, b, batch, seq):
    col = lambda j: pl.BlockSpec((seq, CONV_W), lambda i: (i, j))
    return pl.pallas_call(
        functools.partial(_conv_kernel, seq=seq),
        out_shape=jax.ShapeDtypeStruct((zc.shape[0], CONV_W), F32),
        grid=(batch,),
        in_specs=[col(0), col(1), col(2), _const_spec(w.shape), _const_spec(b.shape)],
        out_specs=col(0),
        compiler_params=_cparams("arbitrary"),
        name="gated_conv",
    )(zc, zc, zc, w, b)


def _dense_attn_kernel(q_ref, k_ref, v_ref, o_ref):
    q = q_ref[...] * (HD ** -0.5)
    kb = k_ref[...].astype(BF16)
    vb = v_ref[...].astype(BF16)
    out = jnp.zeros(q.shape, F32)
    for m in _head_masks():
        s = lax.dot_general((q * m).astype(BF16), kb, NT, preferred_element_type=F32)
        p = jnp.exp(s - jnp.max(s, axis=-1, keepdims=True))
        o = jnp.dot(p.astype(BF16), vb, preferred_element_type=F32)
        out = out + o * (m / jnp.sum(p, axis=-1, keepdims=True))
    o_ref[...] = out


def _dense_attn(zn, batch, seq):
    npair = NA_W // PAIR
    col = lambda off: pl.BlockSpec((seq, PAIR), lambda b, p: (b, off + p))
    return pl.pallas_call(
        _dense_attn_kernel,
        out_shape=jax.ShapeDtypeStruct((zn.shape[0], NA_W), F32),
        grid=(batch, npair),
        in_specs=[col(0), col(npair), col(2 * npair)],
        out_specs=col(0),
        compiler_params=_cparams("arbitrary", "arbitrary"),
        name="context_attention",
    )(zn, zn, zn)


def _na_kernel(q_ref, k_ref, v_ref, kc_ref, vc_ref, bias_ref, o_ref, *, rows):
    r = pl.program_id(2)
    rs = jnp.clip(r - NA_ROWS // 2, 0, rows - NA_ROWS)
    win = pl.ds(pl.multiple_of(rs * GRID_W, GRID_W), NA_ROWS * GRID_W)
    q = q_ref[...] * (HD ** -0.5)
    kw = k_ref[win, :].astype(BF16)
    vw = v_ref[win, :].astype(BF16)
    kc = kc_ref[0, 0].astype(BF16)
    vc = vc_ref[0, 0].astype(BF16)
    out = jnp.zeros(q.shape, F32)
    for h, m in enumerate(_head_masks()):
        qh = (q * m).astype(BF16)
        sw = lax.dot_general(qh, kw, NT, preferred_element_type=F32) + bias_ref[h, 0]
        sc = lax.dot_general(qh, kc, NT, preferred_element_type=F32)
        mx = jnp.maximum(jnp.max(sw, axis=-1, keepdims=True), jnp.max(sc, axis=-1, keepdims=True))
        pw = jnp.exp(sw - mx)
        pc = jnp.exp(sc - mx)
        den = jnp.sum(pw, axis=-1, keepdims=True) + jnp.sum(pc, axis=-1, keepdims=True)
        o = (jnp.dot(pw.astype(BF16), vw, preferred_element_type=F32)
             + jnp.dot(pc.astype(BF16), vc, preferred_element_type=F32))
        out = out + o * (m / den)
    o_ref[...] = out


def _na_bias_table(rpb):
    nwin = NA_ROWS * GRID_W
    cols = np.arange(GRID_W)
    cs = np.clip(cols - NA_COLS // 2, 0, GRID_W - NA_COLS)
    kcol = np.arange(nwin) % GRID_W
    krow = np.arange(nwin) // GRID_W
    col_mask = (kcol[None, :] >= cs[:, None]) & (kcol[None, :] < cs[:, None] + NA_COLS)
    dc = np.clip(kcol[None, :] - cols[:, None], -(NA_COLS - 1), NA_COLS - 1) + NA_COLS - 1
    dr = krow[None, :] - np.arange(NA_ROWS)[:, None] + NA_ROWS - 1
    bias = rpb[:, dr[:, None, :], dc[None, :, :]].astype(F32)
    return jnp.where(col_mask[None, None], bias, -1e30)


def _neighbourhood_attn(zn, kctx, vctx, bias, batch, seq):
    npair = NA_W // PAIR
    rows = seq // GRID_W
    ctx = kctx.shape[2]
    nwin = NA_ROWS * GRID_W
    qcol = pl.BlockSpec((GRID_W, PAIR), lambda b, p, r: (b * rows + r, p))
    seqcol = lambda off: pl.BlockSpec((seq, PAIR), lambda b, p, r: (b, off + p))
    ctxcol = pl.BlockSpec((1, 1, ctx, PAIR), lambda b, p, r: (b, 0, 0, p))
    bias_spec = pl.BlockSpec(
        (2, 1, GRID_W, nwin),
        lambda b, p, r: (p, r - jnp.clip(r - NA_ROWS // 2, 0, rows - NA_ROWS), 0, 0))
    return pl.pallas_call(
        functools.partial(_na_kernel, rows=rows),
        out_shape=jax.ShapeDtypeStruct((zn.shape[0], NA_W), F32),
        grid=(batch, npair, rows),
        in_specs=[qcol, seqcol(npair), seqcol(2 * npair), ctxcol, ctxcol, bias_spec],
        out_specs=qcol,
        compiler_params=_cparams("arbitrary", "arbitrary", "arbitrary"),
        name="neighbourhood_attention",
    )(zn, zn, zn, kctx, vctx, bias)


def _outproj_kernel(x_ref, yg_ref, yc_ref, yn_ref, mod_ref, nw_ref, wg_ref, wc_ref, wn_ref, xo_ref, h_ref):
    y = (jnp.dot(yg_ref[...].astype(BF16), wg_ref[...], preferred_element_type=F32)
         + jnp.dot(yc_ref[...].astype(BF16), wc_ref[...], preferred_element_type=F32)
         + jnp.dot(yn_ref[...].astype(BF16), wn_ref[...], preferred_element_type=F32))
    x = x_ref[...] + mod_ref[0, 2:3, :] * y
    xo_ref[...] = x
    h_ref[...] = _norm_mod(x, nw_ref[...], mod_ref[0, 4:5, :], mod_ref[0, 3:4, :]).astype(BF16)


def _outproj(x, yg, yc, yn, mod, mod_row, nw, wts):
    n = x.shape[0]
    wg, wc, wn = wts
    row = lambda w: pl.BlockSpec((TM, w), lambda i: (i, 0))
    return pl.pallas_call(
        _outproj_kernel,
        out_shape=(jax.ShapeDtypeStruct((n, D), F32), jax.ShapeDtypeStruct((n, D), BF16)),
        grid=(n // TM,),
        in_specs=[row(D), row(GLA_W), row(CONV_W), row(NA_W),
                  pl.BlockSpec((1, 6, D), lambda i: (mod_row(i), 0, 0)), _const_spec((1, D)),
                  _const_spec(wg.shape), _const_spec(wc.shape), _const_spec(wn.shape)],
        out_specs=(row(D), row(D)),
        compiler_params=_cparams("arbitrary"),
        name="out_projection",
    )(x, yg, yc, yn, mod, nw, wg, wc, wn)


def _extract_top(s, dst_ref):
    for k in range(TOPK):
        m = jnp.max(s, axis=0, keepdims=True)
        dst_ref[k:k + 1, :] = m
        s = jnp.where(s == m, -jnp.inf, s)


def _route_kernel(h_ref, wq_ref, kk_ref, s1_ref, s2_ref, e1_ref, e2_ref, tau_ref, v1_scr, v2_scr, c_scr):
    q = jnp.dot(h_ref[...], wq_ref[...], preferred_element_type=F32)
    for h in range(ROUTE_HEADS_PER_STEP):
        qh = q[:, h * LANES:(h + 1) * LANES]
        st = lax.dot_general(kk_ref[h], qh, NT, precision=HI, preferred_element_type=F32)
        s1 = st[:NKEYS]
        s2 = st[NKEYS:]
        _extract_top(s1, v1_scr)
        _extract_top(s2, v2_scr)
        v1 = v1_scr[...]
        v2 = v2_scr[...]
        cand = jnp.concatenate([v1[a:a + 1] + v2 for a in range(8)] + [v1[8:] + v2[0:1]], axis=0)
        _extract_top(cand, c_scr)
        top = c_scr[...]
        mx = top[0:1]
        z = jnp.sum(jnp.exp(top - mx), axis=0, keepdims=True)
        s1_ref[h] = s1
        s2_ref[h] = s2
        e1_ref[h] = jnp.exp(s1 - v1[0:1])
        e2_ref[h] = jnp.exp(s2 - v2[0:1]) * (1.0 / z)
        tau_ref[h] = top[TOPK - 1:TOPK]


def _route(h2, wq, kk):
    n = h2.shape[0]
    hps = ROUTE_HEADS_PER_STEP
    tok = lambda rows: pl.BlockSpec((hps, rows, TM), lambda i, g: (g, 0, i))
    big = jax.ShapeDtypeStruct((PEER_HEADS, NKEYS, n), F32)
    return pl.pallas_call(
        _route_kernel,
        out_shape=(big, big, big, big, jax.ShapeDtypeStruct((PEER_HEADS, 1, n), F32)),
        grid=(n // TM, PEER_HEADS // hps),
        in_specs=[pl.BlockSpec((TM, D), lambda i, g: (i, 0)),
                  pl.BlockSpec((D, hps * LANES), lambda i, g: (0, g)),
                  pl.BlockSpec((hps, 2 * NKEYS, LANES), lambda i, g: (g, 0, 0))],
        out_specs=(tok(NKEYS), tok(NKEYS), tok(NKEYS), tok(NKEYS), tok(1)),
        scratch_shapes=[pltpu.VMEM((TOPK, TM), F32), pltpu.VMEM((TOPK, TM), F32), pltpu.VMEM((TOPK, TM), F32)],
        compiler_params=_cparams("arbitrary", "arbitrary"),
        name="peer_routing",
    )(h2, wq, kk)


def _expert_kernel(h_ref, u_ref, vt_ref, s1_ref, s2_ref, e1_ref, e2_ref, tau_ref, x_ref, mod_ref, o_ref, acc):
    eb = pl.program_id(1)

    @pl.when(eb == 0)
    def _():
        acc[...] = jnp.zeros_like(acc)

    hb = h_ref[...]
    per_sub = EXPERT_SUB // NKEYS

    def sub_body(sub, carry):
        rows = pl.ds(pl.multiple_of(sub * EXPERT_SUB, EXPERT_SUB), EXPERT_SUB)
        act = jax.nn.gelu(lax.dot_general(u_ref[rows, :], hb, NT, preferred_element_type=F32))
        ws = []
        for ii in range(per_sub):
            i1 = eb * (EXPERT_BLOCK // NKEYS) + sub * per_sub + ii
            gate = jnp.zeros((NKEYS, TT), F32)
            for h in range(PEER_HEADS):
                s1row = s1_ref[h, pl.ds(i1, 1), :]
                e1row = e1_ref[h, pl.ds(i1, 1), :]
                keep = (s2_ref[h] + s1row) >= tau_ref[h]
                gate = gate + jnp.where(keep, e2_ref[h] * e1row, 0.0)
            ws.append((gate * act[ii * NKEYS:(ii + 1) * NKEYS]).astype(BF16))
        w = jnp.concatenate(ws, axis=0)
        acc[...] += jnp.dot(vt_ref[sub], w, preferred_element_type=F32)
        return carry

    lax.fori_loop(0, EXPERT_BLOCK // EXPERT_SUB, sub_body, 0)

    @pl.when(eb == pl.num_programs(1) - 1)
    def _():
        o_ref[...] = x_ref[...] + mod_ref[0, 5:6, :] * acc[...].T


def _experts(h2, u, vt, route, x, mod, mod_row):
    n = h2.shape[0]
    s1, s2, e1, e2, tau = route
    nsub = EXPERT_BLOCK // EXPERT_SUB
    tok = lambda rows: pl.BlockSpec((PEER_HEADS, rows, TT), lambda i, e: (0, 0, i))
    return pl.pallas_call(
        _expert_kernel,
        out_shape=jax.ShapeDtypeStruct((n, D), F32),
        grid=(n // TT, u.shape[0] // EXPERT_BLOCK),
        in_specs=[pl.BlockSpec((TT, D), lambda i, e: (i, 0)),
                  pl.BlockSpec((EXPERT_BLOCK, D), lambda i, e: (e, 0)),
                  pl.BlockSpec((nsub, D, EXPERT_SUB), lambda i, e: (e, 0, 0)),
                  tok(NKEYS), tok(NKEYS), tok(NKEYS), tok(NKEYS), tok(1),
                  pl.BlockSpec((TT, D), lambda i, e: (i, 0)),
                  pl.BlockSpec((1, 6, D), lambda i, e: (mod_row(i), 0, 0))],
        out_specs=pl.BlockSpec((TT, D), lambda i, e: (i, 0)),
        scratch_shapes=[pltpu.VMEM((D, TT), F32)],
        compiler_params=_cparams("arbitrary", "arbitrary"),
        name="peer_experts",
    )(h2, u, vt, s1, s2, e1, e2, tau, x, mod)


def _final_norm_kernel(x_ref, w_ref, o_ref):
    x = x_ref[...]
    o_ref[...] = x * lax.rsqrt(jnp.mean(x * x, axis=-1, keepdims=True) + EPS) * w_ref[...]


def _final_norm(x, w):
    n = x.shape[0]
    return pl.pallas_call(
        _final_norm_kernel,
        out_shape=jax.ShapeDtypeStruct((n, D), F32),
        grid=(n // TM,),
        in_specs=[pl.BlockSpec((TM, D), lambda i: (i, 0)), _const_spec((1, D))],
        out_specs=pl.BlockSpec((TM, D), lambda i: (i, 0)),
        compiler_params=_cparams("arbitrary"),
        name="final_norm",
    )(x, w)


def _layer_weights(l, w_in, w_af, b_af, w_ab, b_ab, w_out, peer_wq, peer_k1, peer_k2, peer_u, peer_v):
    o_lr = 4 * GLA_W
    o_conv = o_lr + 2 * LOWRANK
    o_na = o_conv + 3 * CONV_W
    wi = w_in[l]
    wg = wi[:, :o_lr].astype(BF16)
    wl = jnp.pad(wi[:, o_lr:o_conv], ((0, 0), (0, LANES - 2 * LOWRANK))).astype(BF16)
    wc = wi[:, o_conv:o_na].astype(BF16)
    wn = wi[:, o_na:].astype(BF16)
    wab = jnp.zeros((LANES, 2 * GLA_W), F32)
    wab = wab.at[:LOWRANK, :GLA_W].set(w_af[l]).at[LOWRANK:2 * LOWRANK, GLA_W:].set(w_ab[l])
    bab = jnp.concatenate([b_af[l], b_ab[l]])[None, :]
    wo = w_out[l].astype(BF16)
    wo = (wo[:GLA_W], wo[GLA_W:GLA_W + CONV_W], wo[GLA_W + CONV_W:])
    half = peer_k1.shape[-1]
    kk = jnp.concatenate([jnp.pad(peer_k1[l], ((0, 0), (0, 0), (0, half))),
                          jnp.pad(peer_k2[l], ((0, 0), (0, 0), (half, 0)))], axis=1)
    u = peer_u[l].astype(BF16)
    vt = jnp.swapaxes(peer_v[l].astype(BF16).reshape(-1, EXPERT_SUB, D), 1, 2)
    return (wg, wc, wn, wl, wab, bab), wo, peer_wq[l].astype(BF16), kk, u, vt


def kernel(x_prompt, x_sample, cache_na_k, cache_na_v, state_gla_fwd, state_gla_bwd, c, c_ctx, w_ada, b_ada, norm1_w, norm2_w, w_in, w_af, b_af, w_ab, b_ab, gla_norm_w, conv_w, conv_b, na_rpb, w_out, peer_wq, peer_k1, peer_k2, peer_u, peer_v, final_norm_w):
    bp, sp, _ = x_prompt.shape
    bs, ss, _ = x_sample.shape
    depth = w_ada.shape[0]
    xp = x_prompt.reshape(bp * sp, D)
    xs = x_sample.reshape(bs * ss, D)

    cvec = jnp.zeros((8, D), F32).at[0].set(c_ctx).at[1:1 + bs].set(c)
    mods = _modulation(cvec, w_ada, b_ada)
    kctx = jnp.swapaxes(cache_na_k, 2, 3).reshape(bs, depth, -1, NA_W)
    vctx = jnp.swapaxes(cache_na_v, 2, 3).reshape(bs, depth, -1, NA_W)

    prompt_row = lambda i: 0
    sample_row = lambda tile: (lambda i: 1 + i // (ss // tile))

    new_k, new_v, new_sf, new_sb = [], [], [], []
    for l in range(depth):
        inw, wo, wq, kk, u, vt = _layer_weights(l, w_in, w_af, b_af, w_ab, b_ab, w_out,
                                                peer_wq, peer_k1, peer_k2, peer_u, peer_v)
        mod = mods[l]
        n1, n2 = norm1_w[l][None, :], norm2_w[l][None, :]
        gnw = gla_norm_w[l][None, :]
        cw, cbias = conv_w[l], conv_b[l][None, :]
        bias = _na_bias_table(na_rpb[l])
        s0 = (_states_to_blockdiag(state_gla_fwd[:, l]), _states_to_blockdiag(state_gla_bwd[:, l]))

        zg, la, zc, zn = _inproj(xp, mod, prompt_row, n1, inw)
        yg, sf, sb = _gla(zg, la, gnw, bp, sp, None)
        yc = _conv(zc, cw, cbias, bp, sp)
        yn = _dense_attn(zn, bp, sp)
        xp, h2 = _outproj(xp, yg, yc, yn, mod, prompt_row, n2, wo)
        xp = _experts(h2, u, vt, _route(h2, wq, kk), xp, mod, prompt_row)
        heads = lambda a: a.reshape(bp, sp, NA_W // HD, HD).transpose(0, 2, 1, 3)
        new_k.append(heads(zn[:, NA_W:2 * NA_W]))
        new_v.append(heads(zn[:, 2 * NA_W:]))
        new_sf.append(_blockdiag_to_states(sf))
        new_sb.append(_blockdiag_to_states(sb))

        zg, la, zc, zn = _inproj(xs, mod, sample_row(TM), n1, inw)
        yg = _gla(zg, la, gnw, bs, ss, s0)
        yc = _conv(zc, cw, cbias, bs, ss)
        yn = _neighbourhood_attn(zn, kctx[:, l:l + 1], vctx[:, l:l + 1], bias, bs, ss)
        xs, h2 = _outproj(xs, yg, yc, yn, mod, sample_row(TM), n2, wo)
        xs = _experts(h2, u, vt, _route(h2, wq, kk), xs, mod, sample_row(TT))

    fw = final_norm_w[None, :]
    y_prompt = _final_norm(xp, fw).reshape(bp, sp, D)
    y_sample = _final_norm(xs, fw).reshape(bs, ss, D)
    return (y_prompt, y_sample, jnp.stack(new_k, axis=1), jnp.stack(new_v, axis=1),
            jnp.stack(new_sf, axis=1), jnp.stack(new_sb, axis=1))
```

```python
import functools

import numpy as np
import jax
import jax.numpy as jnp
from jax import lax
from jax.experimental import pallas as pl
from jax.experimental.pallas import tpu as pltpu

F32 = jnp.float32
BF16 = jnp.bfloat16
HI = lax.Precision.HIGHEST
NT = (((1,), (1,)), ((), ()))
TN = (((0,), (0,)), ((), ()))

LANES = 128
EPS = 1e-6
D = 1024
HD = 64
PAIR = 2 * HD
GLA_W = 384
CONV_W = 256
NA_W = 384
LOWRANK = 16
GLA_CHUNK = 64
GLA_SUB = 16
GLA_OUT_ROWS = 256
GLA_TAU = 16.0
EXP_CLAMP = 80.0
GRID_W = 64
NA_ROWS = 8
NA_COLS = 16
NA_ROWS_PER_STEP = 4
PEER_HEADS = 8
NKEYS = 128
TOPK = 16
TM = 256
TT = 512
EXPERT_BLOCK = 2048
EXPERT_SUB = 1024
ROUTE_HEADS_PER_STEP = 2
VMEM_LIMIT = 56 * 1024 * 1024


def _cparams(*sem):
    return pltpu.CompilerParams(dimension_semantics=sem, vmem_limit_bytes=VMEM_LIMIT)


def _silu(x):
    return x * jax.nn.sigmoid(x)


def _gelu_tanh(x):
    c0 = float(np.sqrt(2.0 / np.pi))
    z = x * (c0 + (0.044715 * c0) * (x * x))
    hx = 0.5 * x
    return hx + hx * jnp.tanh(z)


def _norm_mod(x, w, scale, shift):
    ms = jnp.mean(x * x, axis=-1, keepdims=True)
    return x * lax.rsqrt(ms + EPS) * w * (1.0 + scale) + shift


def _head_masks():
    lane = lax.broadcasted_iota(jnp.int32, (1, PAIR), 1)
    m0 = (lane < HD).astype(F32)
    return m0, 1.0 - m0


def _mod_kernel(c_ref, w_ref, b_ref, o_ref):
    s = _silu(c_ref[...]).astype(BF16)
    o_ref[0] = jnp.dot(s, w_ref[0].astype(BF16), preferred_element_type=F32) + b_ref[0]


def _modulation(cvec, w_ada, b_ada):
    depth = w_ada.shape[0]
    nb = w_ada.shape[2] // D
    out = pl.pallas_call(
        _mod_kernel,
        out_shape=jax.ShapeDtypeStruct((depth, 8, nb * D), F32),
        grid=(depth, nb),
        in_specs=[pl.BlockSpec((8, D), lambda l, j: (0, 0)),
                  pl.BlockSpec((1, D, D), lambda l, j: (l, 0, j)),
                  pl.BlockSpec((1, 1, D), lambda l, j: (l, 0, j))],
        out_specs=pl.BlockSpec((1, 8, D), lambda l, j: (l, 0, j)),
        compiler_params=_cparams("arbitrary", "arbitrary"),
        name="adaln_modulation",
    )(cvec, w_ada, b_ada.reshape(depth, 1, nb * D))
    return out.reshape(depth, 8, nb, D)


def _inproj_kernel(x_ref, mod_ref, nw_ref, wg_ref, wc_ref, wn_ref, wl_ref, wab_ref, bab_ref,
                   zg_ref, la_ref, zc_ref, zn_ref):
    h = _norm_mod(x_ref[...], nw_ref[...], mod_ref[0, 1:2, :], mod_ref[0, 0:1, :])
    hb = h.astype(BF16)
    zg_ref[...] = jnp.dot(hb, wg_ref[...], preferred_element_type=F32)
    zc_ref[...] = jnp.dot(hb, wc_ref[...], preferred_element_type=F32)
    zn_ref[...] = jnp.dot(hb, wn_ref[...], preferred_element_type=F32)
    lr = jnp.dot(hb, wl_ref[...], preferred_element_type=F32)
    zz = jnp.dot(lr, wab_ref[...], precision=HI, preferred_element_type=F32) + bab_ref[...]
    la = (jnp.minimum(zz, 0.0) - jnp.log(1.0 + jnp.exp(-jnp.abs(zz)))) * (1.0 / GLA_TAU)
    r = lax.broadcasted_iota(jnp.int32, (TM, TM), 0)
    s = lax.broadcasted_iota(jnp.int32, (TM, TM), 1)
    shift = GLA_CHUNK.bit_length() - 1
    same = (r >> shift) == (s >> shift)
    tri_f = (same & (s <= r)).astype(F32)
    tri_b = (same & (s >= r)).astype(F32)
    la_ref[:, :GLA_W] = jnp.dot(tri_f, la[:, :GLA_W], precision=HI, preferred_element_type=F32)
    la_ref[:, GLA_W:] = jnp.dot(tri_b, la[:, GLA_W:], precision=HI, preferred_element_type=F32)


def _const_spec(shape):
    return pl.BlockSpec(shape, lambda *_: (0,) * len(shape))


def _inproj(x, mod, mod_row, nw, wts):
    n = x.shape[0]
    wg, wc, wn, wl, wab, bab = wts
    row = lambda w: pl.BlockSpec((TM, w), lambda i: (i, 0))
    return pl.pallas_call(
        _inproj_kernel,
        out_shape=(jax.ShapeDtypeStruct((n, 4 * GLA_W), F32), jax.ShapeDtypeStruct((n, 2 * GLA_W), F32),
                   jax.ShapeDtypeStruct((n, 3 * CONV_W), F32), jax.ShapeDtypeStruct((n, 3 * NA_W), F32)),
        grid=(n // TM,),
        in_specs=[row(D), pl.BlockSpec((1, 6, D), lambda i: (mod_row(i), 0, 0)), _const_spec((1, D)),
                  _const_spec(wg.shape), _const_spec(wc.shape), _const_spec(wn.shape), _const_spec(wl.shape),
                  _const_spec(wab.shape), _const_spec(bab.shape)],
        out_specs=(row(4 * GLA_W), row(2 * GLA_W), row(3 * CONV_W), row(3 * NA_W)),
        compiler_params=_cparams("arbitrary"),
        name="in_projection",
    )(x, mod, nw, wg, wc, wn, wl, wab, bab)


def _gla_consts(fwd):
    c, sb = GLA_CHUNK, GLA_SUB
    r = lax.broadcasted_iota(jnp.int32, (2 * c, c), 0) & (c - 1)
    s = lax.broadcasted_iota(jnp.int32, (2 * c, c), 1)
    caus = (s <= r) if fwd else (s >= r)
    rowid = lax.broadcasted_iota(jnp.int32, (c, PAIR), 0)
    seen = [(rowid < (i + 1) * sb) if fwd else (rowid >= i * sb) for i in range(c // sb)]
    m0, m1 = _head_masks()
    qsel = [[((rowid >= i * sb) & (rowid < (i + 1) * sb)).astype(F32) * m for i in range(c // sb)]
            for m in (m0, m1)]
    return caus, seen, qsel


def _gla_chunk(qc, kc, vc, cum, st, fwd, consts, masks, bd):
    caus, seen, qsel = consts
    m0, m1 = masks
    c, sb = GLA_CHUNK, GLA_SUB
    nsb = c // sb
    zero_row = jnp.zeros((1, PAIR), F32)
    if fwd:
        last = cum[c - 1:c]
        starts = [zero_row] + [cum[i * sb - 1:i * sb] for i in range(1, nsb)]
    else:
        last = cum[0:1]
        starts = [cum[(i + 1) * sb:(i + 1) * sb + 1] for i in range(nsb - 1)] + [zero_row]
    bm = jnp.concatenate([jnp.broadcast_to(b, (sb, PAIR)) for b in starts], axis=0)
    qt = qc * jnp.exp(cum - bm)
    kbig = jnp.concatenate(
        [(kc * jnp.exp(jnp.where(seen[i], jnp.minimum(starts[i] - cum, EXP_CLAMP), 0.0))).astype(BF16)
         for i in range(nsb)], axis=1)
    qbig = jnp.concatenate(
        [jnp.concatenate([(qt * qsel[h][i]).astype(BF16) for i in range(nsb)], axis=1) for h in range(2)],
        axis=0)
    att = lax.dot_general(qbig, kbig, NT, preferred_element_type=F32)
    att = jnp.where(caus, att, 0.0).astype(BF16)
    vb = vc.astype(BF16)
    intra = (jnp.dot(att[:c], vb, preferred_element_type=F32) * m0
             + jnp.dot(att[c:], vb, preferred_element_type=F32) * m1)
    inter = lax.dot_general((qc * jnp.exp(cum)).astype(BF16), st.astype(BF16), NT,
                            preferred_element_type=F32)
    khat = (kc * jnp.exp(last - cum)).astype(BF16)
    upd = lax.dot_general(vb, khat, TN, preferred_element_type=F32)
    return inter + intra, st * jnp.exp(last) + upd * bd


def _gla_kernel(*refs, seq, state_in):
    if state_in:
        q_ref, k_ref, v_ref, g_ref, laf_ref, lab_ref, nw_ref, s0f_ref, s0b_ref, y_ref, of_scr, ob_scr = refs
    else:
        q_ref, k_ref, v_ref, g_ref, laf_ref, lab_ref, nw_ref, y_ref, sf_ref, sb_ref, of_scr, ob_scr = refs
    c = GLA_CHUNK
    nc = seq // c
    masks = _head_masks()
    m0, m1 = masks
    rr = lax.broadcasted_iota(jnp.int32, (PAIR, PAIR), 0)
    cc = lax.broadcasted_iota(jnp.int32, (PAIR, PAIR), 1)
    bd = ((rr < HD) == (cc < HD)).astype(F32)
    cf = _gla_consts(True)
    cb = _gla_consts(False)
    scale = HD ** -0.5
    nw = nw_ref[...]

    def rows(i):
        return pl.ds(pl.multiple_of(i * c, c), c)

    def scan_body(j, carry):
        stf, stb = carry
        slf = rows(j)
        slb = rows(nc - 1 - j)
        of, stf = _gla_chunk(q_ref[slf, :] * scale, k_ref[slf, :], v_ref[slf, :], laf_ref[slf, :], stf,
                             True, cf, masks, bd)
        ob, stb = _gla_chunk(q_ref[slb, :] * scale, k_ref[slb, :], v_ref[slb, :], lab_ref[slb, :], stb,
                             False, cb, masks, bd)
        of_scr[slf, :] = of
        ob_scr[slb, :] = ob
        return stf, stb

    def out_body(i, carry):
        sl = pl.ds(pl.multiple_of(i * GLA_OUT_ROWS, GLA_OUT_ROWS), GLA_OUT_ROWS)
        tot = of_scr[sl, :] + ob_scr[sl, :]
        sq = tot * tot
        ms = (jnp.sum(sq * m0, axis=-1, keepdims=True) * m0
              + jnp.sum(sq * m1, axis=-1, keepdims=True) * m1) * (1.0 / HD)
        y_ref[sl, :] = tot * lax.rsqrt(ms + EPS) * nw * _silu(g_ref[sl, :])
        return carry

    zero = jnp.zeros((PAIR, PAIR), F32)
    init = (s0f_ref[0, 0], s0b_ref[0, 0]) if state_in else (zero, zero)
    sf, sb = lax.fori_loop(0, nc, scan_body, init, unroll=4)
    lax.fori_loop(0, seq // GLA_OUT_ROWS, out_body, 0)
    if not state_in:
        sf_ref[0, 0] = sf
        sb_ref[0, 0] = sb


def _gla(zg, la, nw, batch, seq, states):
    n = zg.shape[0]
    npair = GLA_W // PAIR
    col = lambda off: pl.BlockSpec((seq, PAIR), lambda b, p: (b, off + p))
    st_spec = pl.BlockSpec((1, 1, PAIR, PAIR), lambda b, p: (b, p, 0, 0))
    in_specs = [col(0), col(npair), col(2 * npair), col(3 * npair), col(0), col(npair),
                pl.BlockSpec((1, PAIR), lambda b, p: (0, p))]
    args = [zg, zg, zg, zg, la, la, nw]
    y_shape = jax.ShapeDtypeStruct((n, GLA_W), F32)
    y_spec = pl.BlockSpec((seq, PAIR), lambda b, p: (b, p))
    if states is None:
        st_shape = jax.ShapeDtypeStruct((batch, npair, PAIR, PAIR), F32)
        out_shape, out_specs = (y_shape, st_shape, st_shape), (y_spec, st_spec, st_spec)
    else:
        in_specs += [st_spec, st_spec]
        args += list(states)
        out_shape, out_specs = y_shape, y_spec
    return pl.pallas_call(
        functools.partial(_gla_kernel, seq=seq, state_in=states is not None),
        out_shape=out_shape,
        grid=(batch, npair),
        in_specs=in_specs,
        out_specs=out_specs,
        scratch_shapes=[pltpu.VMEM((seq, PAIR), F32), pltpu.VMEM((seq, PAIR), F32)],
        compiler_params=_cparams("arbitrary", "arbitrary"),
        name="gla_bidir",
    )(*args)


def _states_to_blockdiag(s):
    b, h = s.shape[:2]
    st = jnp.swapaxes(s, -1, -2).reshape(b, h // 2, 2, HD, HD)
    eye = jnp.eye(2, dtype=s.dtype)
    return jnp.einsum('bpivk,ij->bpivjk', st, eye).reshape(b, h // 2, PAIR, PAIR)


def _blockdiag_to_states(sbd):
    b, p = sbd.shape[:2]
    s6 = sbd.reshape(b, p, 2, HD, 2, HD)
    diag = jnp.stack([s6[:, :, 0, :, 0, :], s6[:, :, 1, :, 1, :]], axis=2)
    return jnp.swapaxes(diag, -1, -2).reshape(b, 2 * p, HD, HD)


def _conv_kernel(ch_ref, cb_ref, cc_ref, w_ref, b_ref, y_ref, *, seq):
    u = cc_ref[...] * ch_ref[...]
    row = lax.broadcasted_iota(jnp.int32, u.shape, 0)
    prev = jnp.where(row == 0, 0.0, pltpu.roll(u, 1, 0))
    nxt = jnp.where(row == seq - 1, 0.0, pltpu.roll(u, seq - 1, 0))
    y_ref[...] = cb_ref[...] * (w_ref[0:1, :] * prev + w_ref[1:2, :] * u + w_ref[2:3, :] * nxt + b_ref[...])


def _conv(zc, w, b, batch, seq):
    col = lambda j: pl.BlockSpec((seq, CONV_W), lambda i: (i, j))
    return pl.pallas_call(
        functools.partial(_conv_kernel, seq=seq),
        out_shape=jax.ShapeDtypeStruct((zc.shape[0], CONV_W), F32),
        grid=(batch,),
        in_specs=[col(0), col(1), col(2), _const_spec(w.shape), _const_spec(b.shape)],
        out_specs=col(0),
        compiler_params=_cparams("arbitrary"),
        name="gated_conv",
    )(zc, zc, zc, w, b)


def _dense_attn_kernel(q_ref, k_ref, v_ref, o_ref):
    q = q_ref[...] * (HD ** -0.5)
    kb = k_ref[...].astype(BF16)
    vb = v_ref[...].astype(BF16)
    out = jnp.zeros(q.shape, F32)
    for m in _head_masks():
        s = lax.dot_general((q * m).astype(BF16), kb, NT, preferred_element_type=F32)
        p = jnp.exp(s - jnp.max(s, axis=-1, keepdims=True))
        o = jnp.dot(p.astype(BF16), vb, preferred_element_type=F32)
        out = out + o * (m / jnp.sum(p, axis=-1, keepdims=True))
    o_ref[...] = out


def _dense_attn(zn, batch, seq):
    npair = NA_W // PAIR
    col = lambda off: pl.BlockSpec((seq, PAIR), lambda b, p: (b, off + p))
    return pl.pallas_call(
        _dense_attn_kernel,
        out_shape=jax.ShapeDtypeStruct((zn.shape[0], NA_W), F32),
        grid=(batch, npair),
        in_specs=[col(0), col(npair), col(2 * npair)],
        out_specs=col(0),
        compiler_params=_cparams("arbitrary", "arbitrary"),
        name="context_attention",
    )(zn, zn, zn)


def _na_kernel(q_ref, k_ref, v_ref, kc_ref, vc_ref, bias_ref, o_ref, *, rows):
    kc = kc_ref[0, 0].astype(BF16)
    vc = vc_ref[0, 0].astype(BF16)
    masks = _head_masks()
    for rr in range(NA_ROWS_PER_STEP):
        r = pl.program_id(2) * NA_ROWS_PER_STEP + rr
        rs = jnp.clip(r - NA_ROWS // 2, 0, rows - NA_ROWS)
        win = pl.ds(pl.multiple_of(rs * GRID_W, GRID_W), NA_ROWS * GRID_W)
        qrows = slice(rr * GRID_W, (rr + 1) * GRID_W)
        q = q_ref[qrows, :] * (HD ** -0.5)
        kw = k_ref[win, :].astype(BF16)
        vw = v_ref[win, :].astype(BF16)
        out = jnp.zeros(q.shape, F32)
        for h, m in enumerate(masks):
            qh = (q * m).astype(BF16)
            sw = lax.dot_general(qh, kw, NT, preferred_element_type=F32) + bias_ref[h, r - rs]
            sc = lax.dot_general(qh, kc, NT, preferred_element_type=F32)
            mx = jnp.maximum(jnp.max(sw, axis=-1, keepdims=True), jnp.max(sc, axis=-1, keepdims=True))
            pw = jnp.exp(sw - mx)
            pc = jnp.exp(sc - mx)
            den = jnp.sum(pw, axis=-1, keepdims=True) + jnp.sum(pc, axis=-1, keepdims=True)
            o = (jnp.dot(pw.astype(BF16), vw, preferred_element_type=F32)
                 + jnp.dot(pc.astype(BF16), vc, preferred_element_type=F32))
            out = out + o * (m / den)
        o_ref[qrows, :] = out


def _na_bias_table(rpb):
    nh = rpb.shape[0]
    nwin = NA_ROWS * GRID_W
    cols = np.arange(GRID_W)
    cs = np.clip(cols - NA_COLS // 2, 0, GRID_W - NA_COLS)
    kcol = np.arange(nwin) % GRID_W
    col_mask = (kcol[None, :] >= cs[:, None]) & (kcol[None, :] < cs[:, None] + NA_COLS)
    dc = np.clip(cols[None, :] - cols[:, None], -(NA_COLS - 1), NA_COLS - 1) + NA_COLS - 1
    onehot = (dc[:, :, None] == np.arange(2 * NA_COLS - 1)).astype(np.float32)
    toep = jnp.einsum('hab,qkb->haqk', rpb.astype(F32), onehot, precision=HI)
    slabs = [toep[:, NA_ROWS - 1 - o:2 * NA_ROWS - 1 - o] for o in range(NA_ROWS)]
    bias = jnp.stack(slabs, axis=1).transpose(0, 1, 3, 2, 4).reshape(nh, NA_ROWS, GRID_W, nwin)
    return jnp.where(col_mask[None, None], bias, -1e30)


def _neighbourhood_attn(zn, kctx, vctx, bias, batch, seq):
    npair = NA_W // PAIR
    rows = seq // GRID_W
    ctx = kctx.shape[2]
    nwin = NA_ROWS * GRID_W
    steps = rows // NA_ROWS_PER_STEP
    qcol = pl.BlockSpec((NA_ROWS_PER_STEP * GRID_W, PAIR), lambda p, b, r: (b * steps + r, p))
    seqcol = lambda off: pl.BlockSpec((seq, PAIR), lambda p, b, r: (b, off + p))
    ctxcol = pl.BlockSpec((1, 1, ctx, PAIR), lambda p, b, r: (b, 0, 0, p))
    bias_spec = pl.BlockSpec((2, NA_ROWS, GRID_W, nwin), lambda p, b, r: (p, 0, 0, 0))
    return pl.pallas_call(
        functools.partial(_na_kernel, rows=rows),
        out_shape=jax.ShapeDtypeStruct((zn.shape[0], NA_W), F32),
        grid=(npair, batch, steps),
        in_specs=[qcol, seqcol(npair), seqcol(2 * npair), ctxcol, ctxcol, bias_spec],
        out_specs=qcol,
        compiler_params=_cparams("arbitrary", "arbitrary", "arbitrary"),
        name="neighbourhood_attention",
    )(zn, zn, zn, kctx, vctx, bias)


def _outproj_kernel(x_ref, yg_ref, yc_ref, yn_ref, mod_ref, nw_ref, wg_ref, wc_ref, wn_ref, xo_ref, h_ref):
    y = (jnp.dot(yg_ref[...].astype(BF16), wg_ref[...], preferred_element_type=F32)
         + jnp.dot(yc_ref[...].astype(BF16), wc_ref[...], preferred_element_type=F32)
         + jnp.dot(yn_ref[...].astype(BF16), wn_ref[...], preferred_element_type=F32))
    x = x_ref[...] + mod_ref[0, 2:3, :] * y
    xo_ref[...] = x
    h_ref[...] = _norm_mod(x, nw_ref[...], mod_ref[0, 4:5, :], mod_ref[0, 3:4, :]).astype(BF16)


def _outproj(x, yg, yc, yn, mod, mod_row, nw, wts):
    n = x.shape[0]
    wg, wc, wn = wts
    row = lambda w: pl.BlockSpec((TM, w), lambda i: (i, 0))
    return pl.pallas_call(
        _outproj_kernel,
        out_shape=(jax.ShapeDtypeStruct((n, D), F32), jax.ShapeDtypeStruct((n, D), BF16)),
        grid=(n // TM,),
        in_specs=[row(D), row(GLA_W), row(CONV_W), row(NA_W),
                  pl.BlockSpec((1, 6, D), lambda i: (mod_row(i), 0, 0)), _const_spec((1, D)),
                  _const_spec(wg.shape), _const_spec(wc.shape), _const_spec(wn.shape)],
        out_specs=(row(D), row(D)),
        compiler_params=_cparams("arbitrary"),
        name="out_projection",
    )(x, yg, yc, yn, mod, nw, wg, wc, wn)


def _extract_top(s, dst_ref, want_rank=False):
    rank = jnp.full(s.shape, float(TOPK), F32) if want_rank else None
    for k in range(TOPK):
        m = jnp.max(s, axis=0, keepdims=True)
        dst_ref[k:k + 1, :] = m
        hit = s == m
        if want_rank:
            rank = jnp.where(hit, float(k), rank)
        s = jnp.where(hit, -jnp.inf, s)
    return rank


def _route_kernel(h_ref, wq_ref, kk_ref, r1_ref, e1_ref, c2_ref, e2_ref, v1_scr, v2_scr, c_scr):
    q = jnp.dot(h_ref[...], wq_ref[...], preferred_element_type=F32)
    for h in range(ROUTE_HEADS_PER_STEP):
        qh = q[:, h * LANES:(h + 1) * LANES]
        st = lax.dot_general(kk_ref[h], qh, NT, precision=HI, preferred_element_type=F32)
        s1 = st[:NKEYS]
        s2 = st[NKEYS:]
        r1_ref[h] = _extract_top(s1, v1_scr, want_rank=True)
        _extract_top(s2, v2_scr)
        v1 = v1_scr[...]
        v2 = v2_scr[...]
        cand = jnp.concatenate([v1[a:a + 1] + v2 for a in range(8)] + [v1[8:] + v2[0:1]], axis=0)
        _extract_top(cand, c_scr)
        top = c_scr[...]
        z = jnp.sum(jnp.exp(top - top[0:1]), axis=0, keepdims=True)
        tau = top[TOPK - 1:TOPK]
        count = jnp.zeros(s2.shape, F32)
        for a in range(TOPK):
            count = count + jnp.where(v1[a:a + 1] + s2 >= tau, 1.0, 0.0)
        e1_ref[h] = jnp.exp(s1 - v1[0:1])
        c2_ref[h] = count.astype(BF16)
        e2_ref[h] = (jnp.exp(s2 - v2[0:1]) * (1.0 / z)).astype(BF16)


def _route(h2, wq, kk):
    n = h2.shape[0]
    hps = ROUTE_HEADS_PER_STEP
    tok = pl.BlockSpec((hps, NKEYS, TM), lambda i, g: (g, 0, i))
    rows = jax.ShapeDtypeStruct((PEER_HEADS, NKEYS, n), F32)
    cols = jax.ShapeDtypeStruct((PEER_HEADS, NKEYS, n), BF16)
    return pl.pallas_call(
        _route_kernel,
        out_shape=(rows, rows, cols, cols),
        grid=(n // TM, PEER_HEADS // hps),
        in_specs=[pl.BlockSpec((TM, D), lambda i, g: (i, 0)),
                  pl.BlockSpec((D, hps * LANES), lambda i, g: (0, g)),
                  pl.BlockSpec((hps, 2 * NKEYS, LANES), lambda i, g: (g, 0, 0))],
        out_specs=(tok, tok, tok, tok),
        scratch_shapes=[pltpu.VMEM((TOPK, TM), F32), pltpu.VMEM((TOPK, TM), F32), pltpu.VMEM((TOPK, TM), F32)],
        compiler_params=_cparams("arbitrary", "arbitrary"),
        name="peer_routing",
    )(h2, wq, kk)


def _expert_kernel(h_ref, u_ref, vt_ref, r1_ref, e1_ref, c2_ref, e2_ref, x_ref, mod_ref, o_ref, acc):
    eb = pl.program_id(1)

    @pl.when(eb == 0)
    def _():
        acc[...] = jnp.zeros_like(acc)

    hb = h_ref[...]
    pack = 16
    keys_per_sub = EXPERT_SUB // NKEYS

    def sub_body(sub, carry):
        rows = pl.ds(pl.multiple_of(sub * EXPERT_SUB, EXPERT_SUB), EXPERT_SUB)
        pre = lax.dot_general(u_ref[rows, :], hb, NT, preferred_element_type=F32).astype(BF16)
        act = _gelu_tanh(pre)
        ws = []
        for ii in range(keys_per_sub):
            i1 = eb * (EXPERT_BLOCK // NKEYS) + sub * keys_per_sub + ii
            gate = jnp.zeros((NKEYS // pack, pack, TT), BF16)
            for h in range(PEER_HEADS):
                r1 = jnp.broadcast_to(r1_ref[h, pl.ds(i1, 1), :], (pack, TT)).astype(BF16)
                e1 = jnp.broadcast_to(e1_ref[h, pl.ds(i1, 1), :], (pack, TT)).astype(BF16)
                keep = c2_ref[h].reshape(NKEYS // pack, pack, TT) > r1[None]
                gate = gate + jnp.where(keep, e1[None], 0.0) * e2_ref[h].reshape(NKEYS // pack, pack, TT)
            ws.append(gate.reshape(NKEYS, TT) * act[ii * NKEYS:(ii + 1) * NKEYS])
        w = jnp.concatenate(ws, axis=0)
        acc[...] += jnp.dot(vt_ref[sub], w, preferred_element_type=F32)
        return carry

    lax.fori_loop(0, EXPERT_BLOCK // EXPERT_SUB, sub_body, 0, unroll=2)

    @pl.when(eb == pl.num_programs(1) - 1)
    def _():
        o_ref[...] = x_ref[...] + mod_ref[0, 5:6, :] * acc[...].T


def _experts(h2, u, vt, route, x, mod, mod_row):
    n = h2.shape[0]
    tok = pl.BlockSpec((PEER_HEADS, NKEYS, TT), lambda i, e: (0, 0, i))
    return pl.pallas_call(
        _expert_kernel,
        out_shape=jax.ShapeDtypeStruct((n, D), F32),
        grid=(n // TT, u.shape[0] // EXPERT_BLOCK),
        in_specs=[pl.BlockSpec((TT, D), lambda i, e: (i, 0)),
                  pl.BlockSpec((EXPERT_BLOCK, D), lambda i, e: (e, 0)),
                  pl.BlockSpec((EXPERT_BLOCK // EXPERT_SUB, D, EXPERT_SUB), lambda i, e: (e, 0, 0)),
                  tok, tok, tok, tok,
                  pl.BlockSpec((TT, D), lambda i, e: (i, 0)),
                  pl.BlockSpec((1, 6, D), lambda i, e: (mod_row(i), 0, 0))],
        out_specs=pl.BlockSpec((TT, D), lambda i, e: (i, 0)),
        scratch_shapes=[pltpu.VMEM((D, TT), F32)],
        compiler_params=_cparams("arbitrary", "arbitrary"),
        name="peer_experts",
    )(h2, u, vt, *route, x, mod)


def _final_norm_kernel(x_ref, w_ref, o_ref):
    x = x_ref[...]
    o_ref[...] = x * lax.rsqrt(jnp.mean(x * x, axis=-1, keepdims=True) + EPS) * w_ref[...]


def _final_norm(x, w):
    n = x.shape[0]
    return pl.pallas_call(
        _final_norm_kernel,
        out_shape=jax.ShapeDtypeStruct((n, D), F32),
        grid=(n // TM,),
        in_specs=[pl.BlockSpec((TM, D), lambda i: (i, 0)), _const_spec((1, D))],
        out_specs=pl.BlockSpec((TM, D), lambda i: (i, 0)),
        compiler_params=_cparams("arbitrary"),
        name="final_norm",
    )(x, w)


def _layer_weights(l, w_in, w_af, b_af, w_ab, b_ab, w_out, peer_wq, peer_k1, peer_k2, peer_u, peer_v):
    o_lr = 4 * GLA_W
    o_conv = o_lr + 2 * LOWRANK
    o_na = o_conv + 3 * CONV_W
    wi = w_in[l]
    wg = wi[:, :o_lr].astype(BF16)
    wl = jnp.pad(wi[:, o_lr:o_conv], ((0, 0), (0, LANES - 2 * LOWRANK))).astype(BF16)
    wc = wi[:, o_conv:o_na].astype(BF16)
    wn = wi[:, o_na:].astype(BF16)
    wab = jnp.zeros((LANES, 2 * GLA_W), F32)
    wab = wab.at[:LOWRANK, :GLA_W].set(w_af[l]).at[LOWRANK:2 * LOWRANK, GLA_W:].set(w_ab[l])
    bab = jnp.concatenate([b_af[l], b_ab[l]])[None, :]
    wo = w_out[l].astype(BF16)
    wo = (wo[:GLA_W], wo[GLA_W:GLA_W + CONV_W], wo[GLA_W + CONV_W:])
    half = peer_k1.shape[-1]
    kk = jnp.concatenate([jnp.pad(peer_k1[l], ((0, 0), (0, 0), (0, half))),
                          jnp.pad(peer_k2[l], ((0, 0), (0, 0), (half, 0)))], axis=1)
    u = peer_u[l].astype(BF16)
    vt = jnp.swapaxes(peer_v[l].astype(BF16).reshape(-1, EXPERT_SUB, D), 1, 2)
    return (wg, wc, wn, wl, wab, bab), wo, peer_wq[l].astype(BF16), kk, u, vt


def kernel(x_prompt, x_sample, cache_na_k, cache_na_v, state_gla_fwd, state_gla_bwd, c, c_ctx, w_ada, b_ada, norm1_w, norm2_w, w_in, w_af, b_af, w_ab, b_ab, gla_norm_w, conv_w, conv_b, na_rpb, w_out, peer_wq, peer_k1, peer_k2, peer_u, peer_v, final_norm_w):
    bp, sp, _ = x_prompt.shape
    bs, ss, _ = x_sample.shape
    depth = w_ada.shape[0]
    xp = x_prompt.reshape(bp * sp, D)
    xs = x_sample.reshape(bs * ss, D)

    cvec = jnp.zeros((8, D), F32).at[0].set(c_ctx).at[1:1 + bs].set(c)
    mods = _modulation(cvec, w_ada, b_ada)
    kctx = jnp.swapaxes(cache_na_k, 2, 3).reshape(bs, depth, -1, NA_W)
    vctx = jnp.swapaxes(cache_na_v, 2, 3).reshape(bs, depth, -1, NA_W)

    prompt_row = lambda i: 0
    sample_row = lambda tile: (lambda i: 1 + i // (ss // tile))

    new_k, new_v, new_sf, new_sb = [], [], [], []
    for l in range(depth):
        inw, wo, wq, kk, u, vt = _layer_weights(l, w_in, w_af, b_af, w_ab, b_ab, w_out,
                                                peer_wq, peer_k1, peer_k2, peer_u, peer_v)
        mod = mods[l]
        n1, n2 = norm1_w[l][None, :], norm2_w[l][None, :]
        gnw = gla_norm_w[l][None, :]
        cw, cbias = conv_w[l], conv_b[l][None, :]
        bias = _na_bias_table(na_rpb[l])
        s0 = (_states_to_blockdiag(state_gla_fwd[:, l]), _states_to_blockdiag(state_gla_bwd[:, l]))

        zg, la, zc, zn = _inproj(xp, mod, prompt_row, n1, inw)
        yg, sf, sb = _gla(zg, la, gnw, bp, sp, None)
        yc = _conv(zc, cw, cbias, bp, sp)
        yn = _dense_attn(zn, bp, sp)
        xp, h2 = _outproj(xp, yg, yc, yn, mod, prompt_row, n2, wo)
        xp = _experts(h2, u, vt, _route(h2, wq, kk), xp, mod, prompt_row)
        heads = lambda a: a.reshape(bp, sp, NA_W // HD, HD).transpose(0, 2, 1, 3)
        new_k.append(heads(zn[:, NA_W:2 * NA_W]))
        new_v.append(heads(zn[:, 2 * NA_W:]))
        new_sf.append(_blockdiag_to_states(sf))
        new_sb.append(_blockdiag_to_states(sb))

        zg, la, zc, zn = _inproj(xs, mod, sample_row(TM), n1, inw)
        yg = _gla(zg, la, gnw, bs, ss, s0)
        yc = _conv(zc, cw, cbias, bs, ss)
        yn = _neighbourhood_attn(zn, kctx[:, l:l + 1], vctx[:, l:l + 1], bias, bs, ss)
        xs, h2 = _outproj(xs, yg, yc, yn, mod, sample_row(TM), n2, wo)
        xs = _experts(h2, u, vt, _route(h2, wq, kk), xs, mod, sample_row(TT))

    fw = final_norm_w[None, :]
    y_prompt = _final_norm(xp, fw).reshape(bp, sp, D)
    y_sample = _final_norm(xs, fw).reshape(bs, ss, D)
    return (y_prompt, y_sample, jnp.stack(new_k, axis=1), jnp.stack(new_v, axis=1),
            jnp.stack(new_sf, axis=1), jnp.stack(new_sb, axis=1))
```

```python
import functools

import numpy as np
import jax
import jax.numpy as jnp
from jax import lax
from jax.experimental import pallas as pl
from jax.experimental.pallas import tpu as pltpu

F32 = jnp.float32
BF16 = jnp.bfloat16
HI = lax.Precision.HIGHEST
NT = (((1,), (1,)), ((), ()))
TN = (((0,), (0,)), ((), ()))

LANES = 128
EPS = 1e-6
D = 1024
HD = 64
PAIR = 2 * HD
GLA_W = 384
CONV_W = 256
NA_W = 384
LOWRANK = 16
GLA_CHUNK = 64
GLA_SUB = 16
GLA_OUT_ROWS = 256
GLA_TAU = 16.0
EXP_CLAMP = 80.0
GRID_W = 64
NA_ROWS = 8
NA_COLS = 16
NA_ROWS_PER_STEP = 4
NA_BLOCKS_PER_STEP = 4
PEER_HEADS = 8
NKEYS = 128
TOPK = 16
TM = 512
CUM_ROWS = 256
TT = 512
EXPERT_BLOCK = 2048
EXPERT_SUB = 512
ROUTE_HEADS_PER_STEP = 8
ROUTE_TM = 128
VMEM_LIMIT = 56 * 1024 * 1024


def _cparams(*sem):
    return pltpu.CompilerParams(dimension_semantics=sem, vmem_limit_bytes=VMEM_LIMIT)


def _silu(x):
    return x * jax.nn.sigmoid(x)


def _gelu_tanh(x):
    c0 = float(np.sqrt(2.0 / np.pi))
    z = x * (c0 + (0.044715 * c0) * (x * x))
    hx = 0.5 * x
    return hx + hx * jnp.tanh(z)


def _norm_mod(x, w, scale, shift):
    ms = jnp.mean(x * x, axis=-1, keepdims=True)
    return x * lax.rsqrt(ms + EPS) * w * (1.0 + scale) + shift


def _head_masks():
    lane = lax.broadcasted_iota(jnp.int32, (1, PAIR), 1)
    m0 = (lane < HD).astype(F32)
    return m0, 1.0 - m0


def _mod_kernel(c_ref, w_ref, b_ref, o_ref):
    s = _silu(c_ref[...]).astype(BF16)
    o_ref[0] = jnp.dot(s, w_ref[0].astype(BF16), preferred_element_type=F32) + b_ref[0]


def _modulation(cvec, w_ada, b_ada):
    depth = w_ada.shape[0]
    nb = w_ada.shape[2] // D
    out = pl.pallas_call(
        _mod_kernel,
        out_shape=jax.ShapeDtypeStruct((depth, 8, nb * D), F32),
        grid=(depth, nb),
        in_specs=[pl.BlockSpec((8, D), lambda l, j: (0, 0)),
                  pl.BlockSpec((1, D, D), lambda l, j: (l, 0, j)),
                  pl.BlockSpec((1, 1, D), lambda l, j: (l, 0, j))],
        out_specs=pl.BlockSpec((1, 8, D), lambda l, j: (l, 0, j)),
        compiler_params=_cparams("arbitrary", "arbitrary"),
        name="adaln_modulation",
    )(cvec, w_ada, b_ada.reshape(depth, 1, nb * D))
    return out.reshape(depth, 8, nb, D)


def _inproj_kernel(x_ref, mod_ref, nw_ref, wg_ref, wc_ref, wn_ref, wl_ref, wab_ref, bab_ref,
                   zg_ref, la_ref, zc_ref, zn_ref):
    h = _norm_mod(x_ref[...], nw_ref[...], mod_ref[0, 1:2, :], mod_ref[0, 0:1, :])
    hb = h.astype(BF16)
    zg_ref[...] = jnp.dot(hb, wg_ref[...], preferred_element_type=F32)
    zc_ref[...] = jnp.dot(hb, wc_ref[...], preferred_element_type=F32)
    zn_ref[...] = jnp.dot(hb, wn_ref[...], preferred_element_type=F32)
    lr = jnp.dot(hb, wl_ref[...], preferred_element_type=F32)
    zz = jnp.dot(lr, wab_ref[...], precision=HI, preferred_element_type=F32) + bab_ref[...]
    la = (jnp.minimum(zz, 0.0) - jnp.log(1.0 + jnp.exp(-jnp.abs(zz)))) * (1.0 / GLA_TAU)
    r = lax.broadcasted_iota(jnp.int32, (CUM_ROWS, CUM_ROWS), 0)
    s = lax.broadcasted_iota(jnp.int32, (CUM_ROWS, CUM_ROWS), 1)
    shift = GLA_CHUNK.bit_length() - 1
    same = (r >> shift) == (s >> shift)
    tri_f = (same & (s <= r)).astype(BF16)
    tri_b = (same & (s >= r)).astype(BF16)
    hi = la.astype(BF16)
    rest = la - hi.astype(F32)
    mid = rest.astype(BF16)
    lo = (rest - mid.astype(F32)).astype(BF16)
    for blk in range(TM // CUM_ROWS):
        rows = slice(blk * CUM_ROWS, (blk + 1) * CUM_ROWS)
        for tri, cols in ((tri_f, slice(0, GLA_W)), (tri_b, slice(GLA_W, 2 * GLA_W))):
            la_ref[rows, cols] = (jnp.dot(tri, hi[rows, cols], preferred_element_type=F32)
                                  + jnp.dot(tri, mid[rows, cols], preferred_element_type=F32)
                                  + jnp.dot(tri, lo[rows, cols], preferred_element_type=F32))


def _const_spec(shape):
    return pl.BlockSpec(shape, lambda *_: (0,) * len(shape))


def _inproj(x, mod, mod_row, nw, wts):
    n = x.shape[0]
    wg, wc, wn, wl, wab, bab = wts
    row = lambda w: pl.BlockSpec((TM, w), lambda i: (i, 0))
    return pl.pallas_call(
        _inproj_kernel,
        out_shape=(jax.ShapeDtypeStruct((n, 4 * GLA_W), F32), jax.ShapeDtypeStruct((n, 2 * GLA_W), F32),
                   jax.ShapeDtypeStruct((n, 3 * CONV_W), F32), jax.ShapeDtypeStruct((n, 3 * NA_W), F32)),
        grid=(n // TM,),
        in_specs=[row(D), pl.BlockSpec((1, 6, D), lambda i: (mod_row(i), 0, 0)), _const_spec((1, D)),
                  _const_spec(wg.shape), _const_spec(wc.shape), _const_spec(wn.shape), _const_spec(wl.shape),
                  _const_spec(wab.shape), _const_spec(bab.shape)],
        out_specs=(row(4 * GLA_W), row(2 * GLA_W), row(3 * CONV_W), row(3 * NA_W)),
        compiler_params=_cparams("arbitrary"),
        name="in_projection",
    )(x, mod, nw, wg, wc, wn, wl, wab, bab)


def _gla_consts(fwd):
    c, sb = GLA_CHUNK, GLA_SUB
    r = lax.broadcasted_iota(jnp.int32, (2 * c, c), 0) & (c - 1)
    s = lax.broadcasted_iota(jnp.int32, (2 * c, c), 1)
    caus = (s <= r) if fwd else (s >= r)
    rowid = lax.broadcasted_iota(jnp.int32, (c, PAIR), 0)
    seen = [(rowid < (i + 1) * sb) if fwd else (rowid >= i * sb) for i in range(c // sb)]
    m0, m1 = _head_masks()
    qsel = [[((rowid >= i * sb) & (rowid < (i + 1) * sb)).astype(F32) * m for i in range(c // sb)]
            for m in (m0, m1)]
    return caus, seen, qsel


def _gla_chunk(qc, kc, vc, cum, st, fwd, consts, masks, bd):
    caus, seen, qsel = consts
    m0, m1 = masks
    c, sb = GLA_CHUNK, GLA_SUB
    nsb = c // sb
    zero_row = jnp.zeros((1, PAIR), F32)
    if fwd:
        last = cum[c - 1:c]
        starts = [zero_row] + [cum[i * sb - 1:i * sb] for i in range(1, nsb)]
    else:
        last = cum[0:1]
        starts = [cum[(i + 1) * sb:(i + 1) * sb + 1] for i in range(nsb - 1)] + [zero_row]
    bm = jnp.concatenate([jnp.broadcast_to(b, (sb, PAIR)) for b in starts], axis=0)
    qt = qc * jnp.exp(cum - bm)
    kbig = jnp.concatenate(
        [(kc * jnp.exp(jnp.where(seen[i], jnp.minimum(starts[i] - cum, EXP_CLAMP), 0.0))).astype(BF16)
         for i in range(nsb)], axis=1)
    qbig = jnp.concatenate(
        [jnp.concatenate([(qt * qsel[h][i]).astype(BF16) for i in range(nsb)], axis=1) for h in range(2)],
        axis=0)
    att = lax.dot_general(qbig, kbig, NT, preferred_element_type=F32)
    att = jnp.where(caus, att, 0.0).astype(BF16)
    vb = vc.astype(BF16)
    intra = (jnp.dot(att[:c], vb, preferred_element_type=F32) * m0
             + jnp.dot(att[c:], vb, preferred_element_type=F32) * m1)
    inter = lax.dot_general((qc * jnp.exp(cum)).astype(BF16), st.astype(BF16), NT,
                            preferred_element_type=F32)
    khat = (kc * jnp.exp(last - cum)).astype(BF16)
    upd = lax.dot_general(vb, khat, TN, preferred_element_type=F32)
    return inter + intra, st * jnp.exp(last) + upd * bd


def _gla_kernel(*refs, seq, state_in):
    if state_in:
        q_ref, k_ref, v_ref, g_ref, laf_ref, lab_ref, nw_ref, s0f_ref, s0b_ref, y_ref, of_scr, ob_scr = refs
    else:
        q_ref, k_ref, v_ref, g_ref, laf_ref, lab_ref, nw_ref, y_ref, sf_ref, sb_ref, of_scr, ob_scr = refs
    c = GLA_CHUNK
    nc = seq // c
    masks = _head_masks()
    m0, m1 = masks
    rr = lax.broadcasted_iota(jnp.int32, (PAIR, PAIR), 0)
    cc = lax.broadcasted_iota(jnp.int32, (PAIR, PAIR), 1)
    bd = ((rr < HD) == (cc < HD)).astype(F32)
    cf = _gla_consts(True)
    cb = _gla_consts(False)
    scale = HD ** -0.5
    nw = nw_ref[...]

    def rows(i):
        return pl.ds(pl.multiple_of(i * c, c), c)

    def scan_body(j, carry):
        stf, stb = carry
        slf = rows(j)
        slb = rows(nc - 1 - j)
        of, stf = _gla_chunk(q_ref[slf, :] * scale, k_ref[slf, :], v_ref[slf, :], laf_ref[slf, :], stf,
                             True, cf, masks, bd)
        ob, stb = _gla_chunk(q_ref[slb, :] * scale, k_ref[slb, :], v_ref[slb, :], lab_ref[slb, :], stb,
                             False, cb, masks, bd)
        of_scr[slf, :] = of
        ob_scr[slb, :] = ob
        return stf, stb

    def out_body(i, carry):
        sl = pl.ds(pl.multiple_of(i * GLA_OUT_ROWS, GLA_OUT_ROWS), GLA_OUT_ROWS)
        tot = of_scr[sl, :] + ob_scr[sl, :]
        sq = tot * tot
        ms = (jnp.sum(sq * m0, axis=-1, keepdims=True) * m0
              + jnp.sum(sq * m1, axis=-1, keepdims=True) * m1) * (1.0 / HD)
        y_ref[sl, :] = tot * lax.rsqrt(ms + EPS) * nw * _silu(g_ref[sl, :])
        return carry

    zero = jnp.zeros((PAIR, PAIR), F32)
    init = (s0f_ref[0, 0], s0b_ref[0, 0]) if state_in else (zero, zero)
    sf, sb = lax.fori_loop(0, nc, scan_body, init, unroll=4)
    lax.fori_loop(0, seq // GLA_OUT_ROWS, out_body, 0)
    if not state_in:
        sf_ref[0, 0] = sf
        sb_ref[0, 0] = sb


def _gla(zg, la, nw, batch, seq, states):
    n = zg.shape[0]
    npair = GLA_W // PAIR
    col = lambda off: pl.BlockSpec((seq, PAIR), lambda b, p: (b, off + p))
    st_spec = pl.BlockSpec((1, 1, PAIR, PAIR), lambda b, p: (b, p, 0, 0))
    in_specs = [col(0), col(npair), col(2 * npair), col(3 * npair), col(0), col(npair),
                pl.BlockSpec((1, PAIR), lambda b, p: (0, p))]
    args = [zg, zg, zg, zg, la, la, nw]
    y_shape = jax.ShapeDtypeStruct((n, GLA_W), F32)
    y_spec = pl.BlockSpec((seq, PAIR), lambda b, p: (b, p))
    if states is None:
        st_shape = jax.ShapeDtypeStruct((batch, npair, PAIR, PAIR), F32)
        out_shape, out_specs = (y_shape, st_shape, st_shape), (y_spec, st_spec, st_spec)
    else:
        in_specs += [st_spec, st_spec]
        args += list(states)
        out_shape, out_specs = y_shape, y_spec
    return pl.pallas_call(
        functools.partial(_gla_kernel, seq=seq, state_in=states is not None),
        out_shape=out_shape,
        grid=(batch, npair),
        in_specs=in_specs,
        out_specs=out_specs,
        scratch_shapes=[pltpu.VMEM((seq, PAIR), F32), pltpu.VMEM((seq, PAIR), F32)],
        compiler_params=_cparams("arbitrary", "arbitrary"),
        name="gla_bidir",
    )(*args)


def _states_to_blockdiag(s):
    b, h = s.shape[:2]
    st = jnp.swapaxes(s, -1, -2).reshape(b, h // 2, 2, HD, HD)
    eye = jnp.eye(2, dtype=s.dtype)
    return jnp.einsum('bpivk,ij->bpivjk', st, eye).reshape(b, h // 2, PAIR, PAIR)


def _blockdiag_to_states(sbd):
    b, p = sbd.shape[:2]
    s6 = sbd.reshape(b, p, 2, HD, 2, HD)
    diag = jnp.stack([s6[:, :, 0, :, 0, :], s6[:, :, 1, :, 1, :]], axis=2)
    return jnp.swapaxes(diag, -1, -2).reshape(b, 2 * p, HD, HD)


def _conv_kernel(ch_ref, cb_ref, cc_ref, w_ref, b_ref, y_ref, *, seq):
    u = cc_ref[...] * ch_ref[...]
    row = lax.broadcasted_iota(jnp.int32, u.shape, 0)
    prev = jnp.where(row == 0, 0.0, pltpu.roll(u, 1, 0))
    nxt = jnp.where(row == seq - 1, 0.0, pltpu.roll(u, seq - 1, 0))
    y_ref[...] = cb_ref[...] * (w_ref[0:1, :] * prev + w_ref[1:2, :] * u + w_ref[2:3, :] * nxt + b_ref[...])


def _conv(zc, w, b, batch, seq):
    col = lambda j: pl.BlockSpec((seq, CONV_W), lambda i: (i, j))
    return pl.pallas_call(
        functools.partial(_conv_kernel, seq=seq),
        out_shape=jax.ShapeDtypeStruct((zc.shape[0], CONV_W), F32),
        grid=(batch,),
        in_specs=[col(0), col(1), col(2), _const_spec(w.shape), _const_spec(b.shape)],
        out_specs=col(0),
        compiler_params=_cparams("arbitrary"),
        name="gated_conv",
    )(zc, zc, zc, w, b)


def _dense_attn_kernel(q_ref, k_ref, v_ref, o_ref):
    q = q_ref[...] * (HD ** -0.5)
    kb = k_ref[...].astype(BF16)
    vb = v_ref[...].astype(BF16)
    out = jnp.zeros(q.shape, F32)
    for m in _head_masks():
        s = lax.dot_general((q * m).astype(BF16), kb, NT, preferred_element_type=F32)
        p = jnp.exp(s - jnp.max(s, axis=-1, keepdims=True))
        o = jnp.dot(p.astype(BF16), vb, preferred_element_type=F32)
        out = out + o * (m / jnp.sum(p, axis=-1, keepdims=True))
    o_ref[...] = out


def _dense_attn(zn, batch, seq):
    npair = NA_W // PAIR
    col = lambda off: pl.BlockSpec((seq, PAIR), lambda b, p: (b, off + p))
    return pl.pallas_call(
        _dense_attn_kernel,
        out_shape=jax.ShapeDtypeStruct((zn.shape[0], NA_W), F32),
        grid=(batch, npair),
        in_specs=[col(0), col(npair), col(2 * npair)],
        out_specs=col(0),
        compiler_params=_cparams("arbitrary", "arbitrary"),
        name="context_attention",
    )(zn, zn, zn)


def _na_kernel(q_ref, k_ref, v_ref, kc_ref, vc_ref, tp_ref, o_ref, *, rows, span):
    rb = NA_ROWS_PER_STEP
    kc = kc_ref[0, 0].astype(BF16)
    vc = vc_ref[0, 0].astype(BF16)
    m0, m1 = _head_masks()
    shift = GRID_W.bit_length() - 1
    keyrow = lax.broadcasted_iota(jnp.int32, (1, span * GRID_W), 1) >> shift

    def block(bi):
        g = pl.program_id(2) * NA_BLOCKS_PER_STEP + bi
        q0 = bi * rb * GRID_W
        us = jnp.clip(g * rb - NA_ROWS // 2, 0, rows - span)
        win = pl.ds(pl.multiple_of(us * GRID_W, GRID_W), span * GRID_W)
        kw = k_ref[win, :].astype(BF16)
        vw = v_ref[win, :].astype(BF16)
        qs = []
        for rr in range(rb):
            q = q_ref[q0 + rr * GRID_W:q0 + (rr + 1) * GRID_W, :] * (HD ** -0.5)
            qs += [q * m0, q * m1]
        qstack = jnp.concatenate(qs, axis=0).astype(BF16)
        sw = lax.dot_general(qstack, kw, NT, preferred_element_type=F32)
        sc = lax.dot_general(qstack, kc, NT, preferred_element_type=F32)
        parts = []
        for rr in range(rb):
            r = g * rb + rr
            lo = jnp.clip(r - NA_ROWS // 2, 0, rows - NA_ROWS) - us
            inside = (keyrow >= lo) & (keyrow < lo + NA_ROWS)
            blk = sw[rr * PAIR:(rr + 1) * PAIR]
            tiles = [blk[:, jp * PAIR:(jp + 1) * PAIR]
                     + tp_ref[0, jnp.clip(us + 2 * jp - r + NA_ROWS, 0, 2 * NA_ROWS - 1)]
                     for jp in range(span // 2)]
            parts.append(jnp.where(inside, jnp.concatenate(tiles, axis=1), -1e30))
        sw = jnp.concatenate(parts, axis=0)
        mx = jnp.maximum(jnp.max(sw, axis=-1, keepdims=True), jnp.max(sc, axis=-1, keepdims=True))
        pw = jnp.exp(sw - mx)
        pc = jnp.exp(sc - mx)
        den = jnp.sum(pw, axis=-1, keepdims=True) + jnp.sum(pc, axis=-1, keepdims=True)
        o = (jnp.dot(pw.astype(BF16), vw, preferred_element_type=F32)
             + jnp.dot(pc.astype(BF16), vc, preferred_element_type=F32)) / den
        for rr in range(rb):
            o_ref[q0 + rr * GRID_W:q0 + (rr + 1) * GRID_W, :] = (o[rr * PAIR:rr * PAIR + HD] * m0
                                                                 + o[rr * PAIR + HD:(rr + 1) * PAIR] * m1)

    for bi in range(NA_BLOCKS_PER_STEP):
        block(bi)


def _na_bias_table(rpb):
    nh = rpb.shape[0]
    cols = np.arange(GRID_W)
    cs = np.clip(cols - NA_COLS // 2, 0, GRID_W - NA_COLS)
    col_mask = (cols[None, :] >= cs[:, None]) & (cols[None, :] < cs[:, None] + NA_COLS)
    dc = np.clip(cols[None, :] - cols[:, None], -(NA_COLS - 1), NA_COLS - 1) + NA_COLS - 1
    onehot = (dc[:, :, None] == np.arange(2 * NA_COLS - 1)).astype(np.float32)
    toep = jnp.einsum('hab,qkb->haqk', rpb.astype(F32), onehot, precision=HI)
    toep = jnp.where(col_mask[None, None], toep, -1e30)
    ext = jnp.pad(toep, ((0, 0), (1, 1), (0, 0), (0, 0)), constant_values=-1e30)
    two = jnp.concatenate([ext[:, :-1], ext[:, 1:]], axis=-1)
    two = two.reshape(nh // 2, 2, 2 * NA_ROWS, GRID_W, 2 * GRID_W).transpose(0, 2, 1, 3, 4)
    return two.reshape(nh // 2, 2 * NA_ROWS, PAIR, PAIR)


def _neighbourhood_attn(zn, kctx, vctx, bias, batch, seq):
    npair = NA_W // PAIR
    rows = seq // GRID_W
    ctx = kctx.shape[2]
    rb = NA_ROWS_PER_STEP
    nblk = NA_BLOCKS_PER_STEP
    assert rows % (rb * nblk) == 0 and rows >= NA_ROWS
    steps = rows // (rb * nblk)
    span = min(rows, NA_ROWS + rb)
    qcol = pl.BlockSpec((nblk * rb * GRID_W, PAIR), lambda p, b, r: (b * steps + r, p))
    seqcol = lambda off: pl.BlockSpec((seq, PAIR), lambda p, b, r: (b, off + p))
    ctxcol = pl.BlockSpec((1, 1, ctx, PAIR), lambda p, b, r: (b, 0, 0, p))
    bias_spec = pl.BlockSpec((1, 2 * NA_ROWS, PAIR, PAIR), lambda p, b, r: (p, 0, 0, 0))
    return pl.pallas_call(
        functools.partial(_na_kernel, rows=rows, span=span),
        out_shape=jax.ShapeDtypeStruct((zn.shape[0], NA_W), F32),
        grid=(npair, batch, steps),
        in_specs=[qcol, seqcol(npair), seqcol(2 * npair), ctxcol, ctxcol, bias_spec],
        out_specs=qcol,
        compiler_params=_cparams("arbitrary", "arbitrary", "arbitrary"),
        name="neighbourhood_attention",
    )(zn, zn, zn, kctx, vctx, bias)


def _outproj_kernel(x_ref, yg_ref, yc_ref, yn_ref, mod_ref, nw_ref, wg_ref, wc_ref, wn_ref, wq_ref,
                    xo_ref, h_ref, q_ref):
    y = (jnp.dot(yg_ref[...].astype(BF16), wg_ref[...], preferred_element_type=F32)
         + jnp.dot(yc_ref[...].astype(BF16), wc_ref[...], preferred_element_type=F32)
         + jnp.dot(yn_ref[...].astype(BF16), wn_ref[...], preferred_element_type=F32))
    x = x_ref[...] + mod_ref[0, 2:3, :] * y
    xo_ref[...] = x
    hb = _norm_mod(x, nw_ref[...], mod_ref[0, 4:5, :], mod_ref[0, 3:4, :]).astype(BF16)
    h_ref[...] = hb
    q_ref[...] = jnp.dot(hb, wq_ref[...], preferred_element_type=F32)


def _outproj(x, yg, yc, yn, mod, mod_row, nw, wts, wq):
    n = x.shape[0]
    wg, wc, wn = wts
    row = lambda w: pl.BlockSpec((TM, w), lambda i: (i, 0))
    return pl.pallas_call(
        _outproj_kernel,
        out_shape=(jax.ShapeDtypeStruct((n, D), F32), jax.ShapeDtypeStruct((n, D), BF16),
                   jax.ShapeDtypeStruct((n, wq.shape[1]), F32)),
        grid=(n // TM,),
        in_specs=[row(D), row(GLA_W), row(CONV_W), row(NA_W),
                  pl.BlockSpec((1, 6, D), lambda i: (mod_row(i), 0, 0)), _const_spec((1, D)),
                  _const_spec(wg.shape), _const_spec(wc.shape), _const_spec(wn.shape), _const_spec(wq.shape)],
        out_specs=(row(D), row(D), row(wq.shape[1])),
        compiler_params=_cparams("arbitrary"),
        name="out_projection",
    )(x, yg, yc, yn, mod, nw, wg, wc, wn, wq)


def _extract_top(s, dst_ref, want_rank=False):
    rank = jnp.full(s.shape, float(TOPK), F32) if want_rank else None
    for k in range(TOPK):
        m = jnp.max(s, axis=0, keepdims=True)
        dst_ref[k:k + 1, :] = m
        hit = s == m
        if want_rank:
            rank = jnp.where(hit, float(k), rank)
        s = jnp.where(hit, -jnp.inf, s)
    return rank


def _route_kernel(q_ref, kk_ref, r1_ref, e1_ref, c2_ref, e2_ref, v1_scr, v2_scr, c_scr):
    for h in range(ROUTE_HEADS_PER_STEP):
        qh = q_ref[:, h * LANES:(h + 1) * LANES]
        st = lax.dot_general(kk_ref[h], qh, NT, precision=HI, preferred_element_type=F32)
        s1 = st[:NKEYS]
        s2 = st[NKEYS:]
        r1_ref[h] = _extract_top(s1, v1_scr, want_rank=True)
        _extract_top(s2, v2_scr)
        v1 = v1_scr[...]
        v2 = v2_scr[...]
        half = TOPK // 2
        cand = jnp.concatenate([v1[0:1] + v2] + [v1[a:a + 1] + v2[:half] for a in range(1, half)]
                               + [v1[half:] + v2[0:1]], axis=0)
        _extract_top(cand, c_scr)
        top = c_scr[...]
        z = jnp.sum(jnp.exp(top - top[0:1]), axis=0, keepdims=True)
        tau = top[TOPK - 1:TOPK]
        count = jnp.zeros(s2.shape, F32)
        for a in range(half):
            count = count + jnp.where(v1[a:a + 1] + s2 >= tau, 1.0, 0.0)
        tail = jnp.sum(jnp.where(v1[half:] + v2[0:1] >= tau, 1.0, 0.0), axis=0, keepdims=True)
        count = count + jnp.where(s2 == v2[0:1], tail, 0.0)
        e1_ref[h] = jnp.exp(s1 - v1[0:1])
        c2_ref[h] = count.astype(BF16)
        e2_ref[h] = (jnp.exp(s2 - v2[0:1]) * (1.0 / z)).astype(BF16)


def _route(q, kk):
    n = q.shape[0]
    hps = ROUTE_HEADS_PER_STEP
    tok = pl.BlockSpec((hps, NKEYS, ROUTE_TM), lambda i, g: (g, 0, i))
    rows = jax.ShapeDtypeStruct((PEER_HEADS, NKEYS, n), F32)
    cols = jax.ShapeDtypeStruct((PEER_HEADS, NKEYS, n), BF16)
    top = pltpu.VMEM((TOPK, ROUTE_TM), F32)
    return pl.pallas_call(
        _route_kernel,
        out_shape=(rows, rows, cols, cols),
        grid=(n // ROUTE_TM, PEER_HEADS // hps),
        in_specs=[pl.BlockSpec((ROUTE_TM, hps * LANES), lambda i, g: (i, g)),
                  pl.BlockSpec((hps, 2 * NKEYS, LANES), lambda i, g: (g, 0, 0))],
        out_specs=(tok, tok, tok, tok),
        scratch_shapes=[top, top, top],
        compiler_params=_cparams("arbitrary", "arbitrary"),
        name="peer_routing",
    )(q, kk)


def _expert_kernel(h_ref, u_ref, vt_ref, r1_ref, e1_ref, c2_ref, e2_ref, x_ref, mod_ref, o_ref, acc, w_scr):
    eb = pl.program_id(1)

    @pl.when(eb == 0)
    def _():
        acc[...] = jnp.zeros_like(acc)

    hb = h_ref[...]
    pack = 16
    keys_per_sub = EXPERT_SUB // NKEYS

    nsub = EXPERT_BLOCK // EXPERT_SUB

    def pre_act(sub):
        rows = slice(sub * EXPERT_SUB, (sub + 1) * EXPERT_SUB)
        return lax.dot_general(u_ref[rows, :], hb, NT, preferred_element_type=F32).astype(BF16)

    def gated(sub, pre):
        act = _gelu_tanh(pre)
        tile = (NKEYS // pack, pack, TT)
        key0 = eb * (EXPERT_BLOCK // NKEYS) + sub * keys_per_sub
        for ii in range(keys_per_sub):
            gate = jnp.zeros(tile, BF16)
            for h in range(PEER_HEADS):
                r1 = jnp.broadcast_to(r1_ref[h, pl.ds(key0 + ii, 1), :], (pack, TT)).astype(BF16)
                e1 = jnp.broadcast_to(e1_ref[h, pl.ds(key0 + ii, 1), :], (pack, TT)).astype(BF16)
                keep = c2_ref[h].reshape(tile) > r1[None]
                gate = gate + jnp.where(keep, e1[None], 0.0) * e2_ref[h].reshape(tile)
            rows = slice(ii * NKEYS, (ii + 1) * NKEYS)
            w_scr[sub, rows, :] = gate.reshape(NKEYS, TT) * act[rows]
        return w_scr[sub]

    pre = pre_act(0)
    out = None
    for sub in range(nsub):
        nxt = pre_act(sub + 1) if sub + 1 < nsub else None
        part = jnp.dot(vt_ref[sub], gated(sub, pre), preferred_element_type=F32)
        out = part if out is None else out + part
        pre = nxt
    acc[...] += out

    @pl.when(eb == pl.num_programs(1) - 1)
    def _():
        o_ref[...] = x_ref[...] + mod_ref[0, 5:6, :] * acc[...].T


def _experts(h2, u, vt, route, x, mod, mod_row):
    n = h2.shape[0]
    tok = pl.BlockSpec((PEER_HEADS, NKEYS, TT), lambda i, e: (0, 0, i))
    return pl.pallas_call(
        _expert_kernel,
        out_shape=jax.ShapeDtypeStruct((n, D), F32),
        grid=(n // TT, u.shape[0] // EXPERT_BLOCK),
        in_specs=[pl.BlockSpec((TT, D), lambda i, e: (i, 0)),
                  pl.BlockSpec((EXPERT_BLOCK, D), lambda i, e: (e, 0)),
                  pl.BlockSpec((EXPERT_BLOCK // EXPERT_SUB, D, EXPERT_SUB), lambda i, e: (e, 0, 0)),
                  tok, tok, tok, tok,
                  pl.BlockSpec((TT, D), lambda i, e: (i, 0)),
                  pl.BlockSpec((1, 6, D), lambda i, e: (mod_row(i), 0, 0))],
        out_specs=pl.BlockSpec((TT, D), lambda i, e: (i, 0)),
        scratch_shapes=[pltpu.VMEM((D, TT), F32),
                        pltpu.VMEM((EXPERT_BLOCK // EXPERT_SUB, EXPERT_SUB, TT), BF16)],
        compiler_params=_cparams("arbitrary", "arbitrary"),
        name="peer_experts",
    )(h2, u, vt, *route, x, mod)


def _final_norm_kernel(x_ref, w_ref, o_ref):
    x = x_ref[...]
    o_ref[...] = x * lax.rsqrt(jnp.mean(x * x, axis=-1, keepdims=True) + EPS) * w_ref[...]


def _final_norm(x, w):
    n = x.shape[0]
    return pl.pallas_call(
        _final_norm_kernel,
        out_shape=jax.ShapeDtypeStruct((n, D), F32),
        grid=(n // TM,),
        in_specs=[pl.BlockSpec((TM, D), lambda i: (i, 0)), _const_spec((1, D))],
        out_specs=pl.BlockSpec((TM, D), lambda i: (i, 0)),
        compiler_params=_cparams("arbitrary"),
        name="final_norm",
    )(x, w)


def _layer_weights(l, w_in, w_af, b_af, w_ab, b_ab, w_out, peer_wq, peer_k1, peer_k2, peer_u, peer_v):
    o_lr = 4 * GLA_W
    o_conv = o_lr + 2 * LOWRANK
    o_na = o_conv + 3 * CONV_W
    wi = w_in[l]
    wg = wi[:, :o_lr].astype(BF16)
    wl = jnp.pad(wi[:, o_lr:o_conv], ((0, 0), (0, LANES - 2 * LOWRANK))).astype(BF16)
    wc = wi[:, o_conv:o_na].astype(BF16)
    wn = wi[:, o_na:].astype(BF16)
    wab = jnp.zeros((LANES, 2 * GLA_W), F32)
    wab = wab.at[:LOWRANK, :GLA_W].set(w_af[l]).at[LOWRANK:2 * LOWRANK, GLA_W:].set(w_ab[l])
    bab = jnp.concatenate([b_af[l], b_ab[l]])[None, :]
    wo = w_out[l].astype(BF16)
    wo = (wo[:GLA_W], wo[GLA_W:GLA_W + CONV_W], wo[GLA_W + CONV_W:])
    half = peer_k1.shape[-1]
    kk = jnp.concatenate([jnp.pad(peer_k1[l], ((0, 0), (0, 0), (0, half))),
                          jnp.pad(peer_k2[l], ((0, 0), (0, 0), (half, 0)))], axis=1)
    u = peer_u[l].astype(BF16)
    vt = jnp.swapaxes(peer_v[l].astype(BF16).reshape(-1, EXPERT_SUB, D), 1, 2)
    return (wg, wc, wn, wl, wab, bab), wo, peer_wq[l].astype(BF16), kk, u, vt


def kernel(x_prompt, x_sample, cache_na_k, cache_na_v, state_gla_fwd, state_gla_bwd, c, c_ctx, w_ada, b_ada, norm1_w, norm2_w, w_in, w_af, b_af, w_ab, b_ab, gla_norm_w, conv_w, conv_b, na_rpb, w_out, peer_wq, peer_k1, peer_k2, peer_u, peer_v, final_norm_w):
    bp, sp, _ = x_prompt.shape
    bs, ss, _ = x_sample.shape
    depth = w_ada.shape[0]
    xp = x_prompt.reshape(bp * sp, D)
    xs = x_sample.reshape(bs * ss, D)

    cvec = jnp.zeros((8, D), F32).at[0].set(c_ctx).at[1:1 + bs].set(c)
    mods = _modulation(cvec, w_ada, b_ada)
    kctx = jnp.swapaxes(cache_na_k, 2, 3).reshape(bs, depth, -1, NA_W)
    vctx = jnp.swapaxes(cache_na_v, 2, 3).reshape(bs, depth, -1, NA_W)

    prompt_row = lambda i: 0
    sample_row = lambda tile: (lambda i: 1 + i // (ss // tile))

    new_k, new_v, new_sf, new_sb = [], [], [], []
    for l in range(depth):
        inw, wo, wq, kk, u, vt = _layer_weights(l, w_in, w_af, b_af, w_ab, b_ab, w_out,
                                                peer_wq, peer_k1, peer_k2, peer_u, peer_v)
        mod = mods[l]
        n1, n2 = norm1_w[l][None, :], norm2_w[l][None, :]
        gnw = gla_norm_w[l][None, :]
        cw, cbias = conv_w[l], conv_b[l][None, :]
        bias = _na_bias_table(na_rpb[l])
        s0 = (_states_to_blockdiag(state_gla_fwd[:, l]), _states_to_blockdiag(state_gla_bwd[:, l]))

        zg, la, zc, zn = _inproj(xp, mod, prompt_row, n1, inw)
        yg, sf, sb = _gla(zg, la, gnw, bp, sp, None)
        yc = _conv(zc, cw, cbias, bp, sp)
        yn = _dense_attn(zn, bp, sp)
        xp, h2, q = _outproj(xp, yg, yc, yn, mod, prompt_row, n2, wo, wq)
        xp = _experts(h2, u, vt, _route(q, kk), xp, mod, prompt_row)
        heads = lambda a: a.reshape(bp, sp, NA_W // HD, HD).transpose(0, 2, 1, 3)
        new_k.append(heads(zn[:, NA_W:2 * NA_W]))
        new_v.append(heads(zn[:, 2 * NA_W:]))
        new_sf.append(_blockdiag_to_states(sf))
        new_sb.append(_blockdiag_to_states(sb))

        zg, la, zc, zn = _inproj(xs, mod, sample_row(TM), n1, inw)
        yg = _gla(zg, la, gnw, bs, ss, s0)
        yc = _conv(zc, cw, cbias, bs, ss)
        yn = _neighbourhood_attn(zn, kctx[:, l:l + 1], vctx[:, l:l + 1], bias, bs, ss)
        xs, h2, q = _outproj(xs, yg, yc, yn, mod, sample_row(TM), n2, wo, wq)
        xs = _experts(h2, u, vt, _route(q, kk), xs, mod, sample_row(TT))

    fw = final_norm_w[None, :]
    y_prompt = _final_norm(xp, fw).reshape(bp, sp, D)
    y_sample = _final_norm(xs, fw).reshape(bs, ss, D)
    return (y_prompt, y_sample, jnp.stack(new_k, axis=1), jnp.stack(new_v, axis=1),
            jnp.stack(new_sf, axis=1), jnp.stack(new_sb, axis=1))
```

```python
import functools

import numpy as np
import jax
import jax.numpy as jnp
from jax import lax
from jax.experimental import pallas as pl
from jax.experimental.pallas import tpu as pltpu

F32 = jnp.float32
BF16 = jnp.bfloat16
HI = lax.Precision.HIGHEST
NT = (((1,), (1,)), ((), ()))
TN = (((0,), (0,)), ((), ()))

LANES = 128
EPS = 1e-6
D = 1024
HD = 64
PAIR = 2 * HD
GLA_W = 384
CONV_W = 256
NA_W = 384
LOWRANK = 16
GLA_CHUNK = 64
GLA_SUB = 16
GLA_OUT_ROWS = 256
GLA_GROUP = 8
GLA_TAU = 16.0
EXP_CLAMP = 80.0
GRID_W = 64
NA_ROWS = 8
NA_COLS = 16
NA_ROWS_PER_STEP = 4
NA_BLOCKS_PER_STEP = 4
PEER_HEADS = 8
NKEYS = 128
TOPK = 16
TM = 512
CUM_ROWS = 256
TT = 512
EXPERT_BLOCK = 2048
EXPERT_SUB = 512
ROUTE_HEADS_PER_STEP = 8
ROUTE_TM = 128
VMEM_LIMIT = 56 * 1024 * 1024


def _cparams(*sem):
    return pltpu.CompilerParams(dimension_semantics=sem, vmem_limit_bytes=VMEM_LIMIT)


def _silu(x):
    return x * jax.nn.sigmoid(x)


def _gelu_tanh(x):
    c0 = float(np.sqrt(2.0 / np.pi))
    z = x * (c0 + (0.044715 * c0) * (x * x))
    hx = 0.5 * x
    return hx + hx * jnp.tanh(z)


def _norm_mod(x, w, scale, shift):
    ms = jnp.mean(x * x, axis=-1, keepdims=True)
    return x * lax.rsqrt(ms + EPS) * w * (1.0 + scale) + shift


def _head_masks():
    lane = lax.broadcasted_iota(jnp.int32, (1, PAIR), 1)
    m0 = (lane < HD).astype(F32)
    return m0, 1.0 - m0


def _mod_kernel(c_ref, w_ref, b_ref, o_ref):
    s = _silu(c_ref[...]).astype(BF16)
    o_ref[0] = jnp.dot(s, w_ref[0].astype(BF16), preferred_element_type=F32) + b_ref[0]


def _modulation(cvec, w_ada, b_ada):
    depth = w_ada.shape[0]
    nb = w_ada.shape[2] // D
    out = pl.pallas_call(
        _mod_kernel,
        out_shape=jax.ShapeDtypeStruct((depth, 8, nb * D), F32),
        grid=(depth, nb),
        in_specs=[pl.BlockSpec((8, D), lambda l, j: (0, 0)),
                  pl.BlockSpec((1, D, D), lambda l, j: (l, 0, j)),
                  pl.BlockSpec((1, 1, D), lambda l, j: (l, 0, j))],
        out_specs=pl.BlockSpec((1, 8, D), lambda l, j: (l, 0, j)),
        compiler_params=_cparams("arbitrary", "arbitrary"),
        name="adaln_modulation",
    )(cvec, w_ada, b_ada.reshape(depth, 1, nb * D))
    return out.reshape(depth, 8, nb, D)


def _inproj_kernel(x_ref, mod_ref, nw_ref, wg_ref, wc_ref, wn_ref, wl_ref, wab_ref, bab_ref,
                   zg_ref, la_ref, zc_ref, zn_ref):
    h = _norm_mod(x_ref[...], nw_ref[...], mod_ref[0, 1:2, :], mod_ref[0, 0:1, :])
    hb = h.astype(BF16)
    zg_ref[...] = jnp.dot(hb, wg_ref[...], preferred_element_type=F32)
    zc_ref[...] = jnp.dot(hb, wc_ref[...], preferred_element_type=F32)
    zn_ref[...] = jnp.dot(hb, wn_ref[...], preferred_element_type=F32)
    lr = jnp.dot(hb, wl_ref[...], preferred_element_type=F32)
    zz = jnp.dot(lr, wab_ref[...], precision=HI, preferred_element_type=F32) + bab_ref[...]
    la = (jnp.minimum(zz, 0.0) - jnp.log(1.0 + jnp.exp(-jnp.abs(zz)))) * (1.0 / GLA_TAU)
    r = lax.broadcasted_iota(jnp.int32, (CUM_ROWS, CUM_ROWS), 0)
    s = lax.broadcasted_iota(jnp.int32, (CUM_ROWS, CUM_ROWS), 1)
    shift = GLA_CHUNK.bit_length() - 1
    same = (r >> shift) == (s >> shift)
    tri_f = (same & (s <= r)).astype(BF16)
    tri_b = (same & (s >= r)).astype(BF16)
    hi = la.astype(BF16)
    rest = la - hi.astype(F32)
    mid = rest.astype(BF16)
    lo = (rest - mid.astype(F32)).astype(BF16)
    for blk in range(TM // CUM_ROWS):
        rows = slice(blk * CUM_ROWS, (blk + 1) * CUM_ROWS)
        for tri, cols in ((tri_f, slice(0, GLA_W)), (tri_b, slice(GLA_W, 2 * GLA_W))):
            la_ref[rows, cols] = (jnp.dot(tri, hi[rows, cols], preferred_element_type=F32)
                                  + jnp.dot(tri, mid[rows, cols], preferred_element_type=F32)
                                  + jnp.dot(tri, lo[rows, cols], preferred_element_type=F32))


def _const_spec(shape):
    return pl.BlockSpec(shape, lambda *_: (0,) * len(shape))


def _inproj(x, mod, mod_row, nw, wts):
    n = x.shape[0]
    wg, wc, wn, wl, wab, bab = wts
    row = lambda w: pl.BlockSpec((TM, w), lambda i: (i, 0))
    return pl.pallas_call(
        _inproj_kernel,
        out_shape=(jax.ShapeDtypeStruct((n, 4 * GLA_W), F32), jax.ShapeDtypeStruct((n, 2 * GLA_W), F32),
                   jax.ShapeDtypeStruct((n, 3 * CONV_W), F32), jax.ShapeDtypeStruct((n, 3 * NA_W), F32)),
        grid=(n // TM,),
        in_specs=[row(D), pl.BlockSpec((1, 6, D), lambda i: (mod_row(i), 0, 0)), _const_spec((1, D)),
                  _const_spec(wg.shape), _const_spec(wc.shape), _const_spec(wn.shape), _const_spec(wl.shape),
                  _const_spec(wab.shape), _const_spec(bab.shape)],
        out_specs=(row(4 * GLA_W), row(2 * GLA_W), row(3 * CONV_W), row(3 * NA_W)),
        compiler_params=_cparams("arbitrary"),
        name="in_projection",
    )(x, mod, nw, wg, wc, wn, wl, wab, bab)


def _gla_consts(fwd):
    c, sb = GLA_CHUNK, GLA_SUB
    r = lax.broadcasted_iota(jnp.int32, (2 * c, c), 0) & (c - 1)
    s = lax.broadcasted_iota(jnp.int32, (2 * c, c), 1)
    caus = (s <= r) if fwd else (s >= r)
    rowid = lax.broadcasted_iota(jnp.int32, (c, PAIR), 0)
    seen = [(rowid < (i + 1) * sb) if fwd else (rowid >= i * sb) for i in range(c // sb)]
    m0, m1 = _head_masks()
    qsel = [[((rowid >= i * sb) & (rowid < (i + 1) * sb)).astype(F32) * m for i in range(c // sb)]
            for m in (m0, m1)]
    return caus, seen, qsel


def _gla_operands(qc, kc, vc, cum, fwd, consts):
    _, seen, qsel = consts
    c, sb = GLA_CHUNK, GLA_SUB
    nsb = c // sb
    zero_row = jnp.zeros((1, PAIR), F32)
    if fwd:
        last = cum[c - 1:c]
        starts = [zero_row] + [cum[i * sb - 1:i * sb] for i in range(1, nsb)]
    else:
        last = cum[0:1]
        starts = [cum[(i + 1) * sb:(i + 1) * sb + 1] for i in range(nsb - 1)] + [zero_row]
    bm = jnp.concatenate([jnp.broadcast_to(b, (sb, PAIR)) for b in starts], axis=0)
    qt = qc * jnp.exp(cum - bm)
    kbig = jnp.concatenate(
        [(kc * jnp.exp(jnp.where(seen[i], jnp.minimum(starts[i] - cum, EXP_CLAMP), 0.0))).astype(BF16)
         for i in range(nsb)], axis=1)
    qbig = jnp.concatenate(
        [jnp.concatenate([(qt * qsel[h][i]).astype(BF16) for i in range(nsb)], axis=1) for h in range(2)],
        axis=0)
    return dict(qbig=qbig, kbig=kbig, vb=vc.astype(BF16), qdec=(qc * jnp.exp(cum)).astype(BF16),
                khat=(kc * jnp.exp(last - cum)).astype(BF16), decay=jnp.exp(last))


def _gla_group(chunks, st, consts, masks, bd):
    caus = consts[0]
    m0, m1 = masks
    c = GLA_CHUNK
    atts = [lax.dot_general(ch['qbig'], ch['kbig'], NT, preferred_element_type=F32) for ch in chunks]
    upds = [lax.dot_general(ch['vb'], ch['khat'], TN, preferred_element_type=F32) for ch in chunks]
    atts = [jnp.where(caus, a, 0.0).astype(BF16) for a in atts]
    intras = [jnp.dot(a[:c], ch['vb'], preferred_element_type=F32) * m0
              + jnp.dot(a[c:], ch['vb'], preferred_element_type=F32) * m1 for a, ch in zip(atts, chunks)]
    outs = []
    for ch, upd, intra in zip(chunks, upds, intras):
        outs.append(intra + lax.dot_general(ch['qdec'], st.astype(BF16), NT, preferred_element_type=F32))
        st = st * ch['decay'] + upd * bd
    return outs, st


def _gla_kernel(*refs, seq, state_in):
    if state_in:
        q_ref, k_ref, v_ref, g_ref, laf_ref, lab_ref, nw_ref, s0f_ref, s0b_ref, y_ref, of_scr, ob_scr = refs
    else:
        q_ref, k_ref, v_ref, g_ref, laf_ref, lab_ref, nw_ref, y_ref, sf_ref, sb_ref, of_scr, ob_scr = refs
    c = GLA_CHUNK
    nc = seq // c
    masks = _head_masks()
    m0, m1 = masks
    rr = lax.broadcasted_iota(jnp.int32, (PAIR, PAIR), 0)
    cc = lax.broadcasted_iota(jnp.int32, (PAIR, PAIR), 1)
    bd = ((rr < HD) == (cc < HD)).astype(F32)
    cf = _gla_consts(True)
    cb = _gla_consts(False)
    scale = HD ** -0.5
    nw = nw_ref[...]

    def rows(i):
        return pl.ds(pl.multiple_of(i * c, c), c)

    grp = min(GLA_GROUP, nc)

    def scan_body(j, carry):
        stf, stb = carry
        slf = [rows(j * grp + k) for k in range(grp)]
        slb = [rows(nc - 1 - (j * grp + k)) for k in range(grp)]
        chf = [_gla_operands(q_ref[s, :] * scale, k_ref[s, :], v_ref[s, :], laf_ref[s, :], True, cf)
               for s in slf]
        chb = [_gla_operands(q_ref[s, :] * scale, k_ref[s, :], v_ref[s, :], lab_ref[s, :], False, cb)
               for s in slb]
        of, stf = _gla_group(chf, stf, cf, masks, bd)
        ob, stb = _gla_group(chb, stb, cb, masks, bd)
        for s, o in zip(slf, of):
            of_scr[s, :] = o
        for s, o in zip(slb, ob):
            ob_scr[s, :] = o
        return stf, stb

    def out_body(i, carry):
        sl = pl.ds(pl.multiple_of(i * GLA_OUT_ROWS, GLA_OUT_ROWS), GLA_OUT_ROWS)
        tot = of_scr[sl, :] + ob_scr[sl, :]
        sq = tot * tot
        ms = (jnp.sum(sq * m0, axis=-1, keepdims=True) * m0
              + jnp.sum(sq * m1, axis=-1, keepdims=True) * m1) * (1.0 / HD)
        y_ref[sl, :] = tot * lax.rsqrt(ms + EPS) * nw * _silu(g_ref[sl, :])
        return carry

    zero = jnp.zeros((PAIR, PAIR), F32)
    init = (s0f_ref[0, 0], s0b_ref[0, 0]) if state_in else (zero, zero)
    sf, sb = lax.fori_loop(0, nc // grp, scan_body, init)
    lax.fori_loop(0, seq // GLA_OUT_ROWS, out_body, 0)
    if not state_in:
        sf_ref[0, 0] = sf
        sb_ref[0, 0] = sb


def _gla(zg, la, nw, batch, seq, states):
    n = zg.shape[0]
    npair = GLA_W // PAIR
    col = lambda off: pl.BlockSpec((seq, PAIR), lambda b, p: (b, off + p))
    st_spec = pl.BlockSpec((1, 1, PAIR, PAIR), lambda b, p: (b, p, 0, 0))
    in_specs = [col(0), col(npair), col(2 * npair), col(3 * npair), col(0), col(npair),
                pl.BlockSpec((1, PAIR), lambda b, p: (0, p))]
    args = [zg, zg, zg, zg, la, la, nw]
    y_shape = jax.ShapeDtypeStruct((n, GLA_W), F32)
    y_spec = pl.BlockSpec((seq, PAIR), lambda b, p: (b, p))
    if states is None:
        st_shape = jax.ShapeDtypeStruct((batch, npair, PAIR, PAIR), F32)
        out_shape, out_specs = (y_shape, st_shape, st_shape), (y_spec, st_spec, st_spec)
    else:
        in_specs += [st_spec, st_spec]
        args += list(states)
        out_shape, out_specs = y_shape, y_spec
    return pl.pallas_call(
        functools.partial(_gla_kernel, seq=seq, state_in=states is not None),
        out_shape=out_shape,
        grid=(batch, npair),
        in_specs=in_specs,
        out_specs=out_specs,
        scratch_shapes=[pltpu.VMEM((seq, PAIR), F32), pltpu.VMEM((seq, PAIR), F32)],
        compiler_params=_cparams("arbitrary", "arbitrary"),
        name="gla_bidir",
    )(*args)


def _states_to_blockdiag(s):
    b, h = s.shape[:2]
    st = jnp.swapaxes(s, -1, -2).reshape(b, h // 2, 2, HD, HD)
    eye = jnp.eye(2, dtype=s.dtype)
    return jnp.einsum('bpivk,ij->bpivjk', st, eye).reshape(b, h // 2, PAIR, PAIR)


def _blockdiag_to_states(sbd):
    b, p = sbd.shape[:2]
    s6 = sbd.reshape(b, p, 2, HD, 2, HD)
    diag = jnp.stack([s6[:, :, 0, :, 0, :], s6[:, :, 1, :, 1, :]], axis=2)
    return jnp.swapaxes(diag, -1, -2).reshape(b, 2 * p, HD, HD)


def _conv_kernel(ch_ref, cb_ref, cc_ref, w_ref, b_ref, y_ref, *, seq):
    u = cc_ref[...] * ch_ref[...]
    row = lax.broadcasted_iota(jnp.int32, u.shape, 0)
    prev = jnp.where(row == 0, 0.0, pltpu.roll(u, 1, 0))
    nxt = jnp.where(row == seq - 1, 0.0, pltpu.roll(u, seq - 1, 0))
    y_ref[...] = cb_ref[...] * (w_ref[0:1, :] * prev + w_ref[1:2, :] * u + w_ref[2:3, :] * nxt + b_ref[...])


def _conv(zc, w, b, batch, seq):
    col = lambda j: pl.BlockSpec((seq, CONV_W), lambda i: (i, j))
    return pl.pallas_call(
        functools.partial(_conv_kernel, seq=seq),
        out_shape=jax.ShapeDtypeStruct((zc.shape[0], CONV_W), F32),
        grid=(batch,),
        in_specs=[col(0), col(1), col(2), _const_spec(w.shape), _const_spec(b.shape)],
        out_specs=col(0),
        compiler_params=_cparams("arbitrary"),
        name="gated_conv",
    )(zc, zc, zc, w, b)


def _dense_attn_kernel(q_ref, k_ref, v_ref, o_ref):
    q = q_ref[...] * (HD ** -0.5)
    kb = k_ref[...].astype(BF16)
    vb = v_ref[...].astype(BF16)
    out = jnp.zeros(q.shape, F32)
    for m in _head_masks():
        s = lax.dot_general((q * m).astype(BF16), kb, NT, preferred_element_type=F32)
        p = jnp.exp(s - jnp.max(s, axis=-1, keepdims=True))
        o = jnp.dot(p.astype(BF16), vb, preferred_element_type=F32)
        out = out + o * (m / jnp.sum(p, axis=-1, keepdims=True))
    o_ref[...] = out


def _dense_attn(zn, batch, seq):
    npair = NA_W // PAIR
    col = lambda off: pl.BlockSpec((seq, PAIR), lambda b, p: (b, off + p))
    return pl.pallas_call(
        _dense_attn_kernel,
        out_shape=jax.ShapeDtypeStruct((zn.shape[0], NA_W), F32),
        grid=(batch, npair),
        in_specs=[col(0), col(npair), col(2 * npair)],
        out_specs=col(0),
        compiler_params=_cparams("arbitrary", "arbitrary"),
        name="context_attention",
    )(zn, zn, zn)


def _na_kernel(q_ref, k_ref, v_ref, kc_ref, vc_ref, tp_ref, o_ref, *, rows, span):
    rb = NA_ROWS_PER_STEP
    kc = kc_ref[0, 0].astype(BF16)
    vc = vc_ref[0, 0].astype(BF16)
    m0, m1 = _head_masks()
    shift = GRID_W.bit_length() - 1
    keyrow = lax.broadcasted_iota(jnp.int32, (1, span * GRID_W), 1) >> shift

    def block(bi):
        g = pl.program_id(2) * NA_BLOCKS_PER_STEP + bi
        q0 = bi * rb * GRID_W
        us = jnp.clip(g * rb - NA_ROWS // 2, 0, rows - span)
        win = pl.ds(pl.multiple_of(us * GRID_W, GRID_W), span * GRID_W)
        kw = k_ref[win, :].astype(BF16)
        vw = v_ref[win, :].astype(BF16)
        qs = []
        for rr in range(rb):
            q = q_ref[q0 + rr * GRID_W:q0 + (rr + 1) * GRID_W, :] * (HD ** -0.5)
            qs += [q * m0, q * m1]
        qstack = jnp.concatenate(qs, axis=0).astype(BF16)
        sw = lax.dot_general(qstack, kw, NT, preferred_element_type=F32)
        sc = lax.dot_general(qstack, kc, NT, preferred_element_type=F32)
        parts = []
        for rr in range(rb):
            r = g * rb + rr
            lo = jnp.clip(r - NA_ROWS // 2, 0, rows - NA_ROWS) - us
            inside = (keyrow >= lo) & (keyrow < lo + NA_ROWS)
            blk = sw[rr * PAIR:(rr + 1) * PAIR]
            tiles = [blk[:, jp * PAIR:(jp + 1) * PAIR]
                     + tp_ref[0, jnp.clip(us + 2 * jp - r + NA_ROWS, 0, 2 * NA_ROWS - 1)]
                     for jp in range(span // 2)]
            parts.append(jnp.where(inside, jnp.concatenate(tiles, axis=1), -1e30))
        sw = jnp.concatenate(parts, axis=0)
        mx = jnp.maximum(jnp.max(sw, axis=-1, keepdims=True), jnp.max(sc, axis=-1, keepdims=True))
        pw = jnp.exp(sw - mx)
        pc = jnp.exp(sc - mx)
        den = jnp.sum(pw, axis=-1, keepdims=True) + jnp.sum(pc, axis=-1, keepdims=True)
        o = (jnp.dot(pw.astype(BF16), vw, preferred_element_type=F32)
             + jnp.dot(pc.astype(BF16), vc, preferred_element_type=F32)) / den
        for rr in range(rb):
            o_ref[q0 + rr * GRID_W:q0 + (rr + 1) * GRID_W, :] = (o[rr * PAIR:rr * PAIR + HD] * m0
                                                                 + o[rr * PAIR + HD:(rr + 1) * PAIR] * m1)

    for bi in range(NA_BLOCKS_PER_STEP):
        block(bi)


def _na_bias_table(rpb):
    nh = rpb.shape[0]
    cols = np.arange(GRID_W)
    cs = np.clip(cols - NA_COLS // 2, 0, GRID_W - NA_COLS)
    col_mask = (cols[None, :] >= cs[:, None]) & (cols[None, :] < cs[:, None] + NA_COLS)
    dc = np.clip(cols[None, :] - cols[:, None], -(NA_COLS - 1), NA_COLS - 1) + NA_COLS - 1
    onehot = (dc[:, :, None] == np.arange(2 * NA_COLS - 1)).astype(np.float32)
    toep = jnp.einsum('hab,qkb->haqk', rpb.astype(F32), onehot, precision=HI)
    toep = jnp.where(col_mask[None, None], toep, -1e30)
    ext = jnp.pad(toep, ((0, 0), (1, 1), (0, 0), (0, 0)), constant_values=-1e30)
    two = jnp.concatenate([ext[:, :-1], ext[:, 1:]], axis=-1)
    two = two.reshape(nh // 2, 2, 2 * NA_ROWS, GRID_W, 2 * GRID_W).transpose(0, 2, 1, 3, 4)
    return two.reshape(nh // 2, 2 * NA_ROWS, PAIR, PAIR)


def _neighbourhood_attn(zn, kctx, vctx, bias, batch, seq):
    npair = NA_W // PAIR
    rows = seq // GRID_W
    ctx = kctx.shape[2]
    rb = NA_ROWS_PER_STEP
    nblk = NA_BLOCKS_PER_STEP
    assert rows % (rb * nblk) == 0 and rows >= NA_ROWS
    steps = rows // (rb * nblk)
    span = min(rows, NA_ROWS + rb)
    qcol = pl.BlockSpec((nblk * rb * GRID_W, PAIR), lambda p, b, r: (b * steps + r, p))
    seqcol = lambda off: pl.BlockSpec((seq, PAIR), lambda p, b, r: (b, off + p))
    ctxcol = pl.BlockSpec((1, 1, ctx, PAIR), lambda p, b, r: (b, 0, 0, p))
    bias_spec = pl.BlockSpec((1, 2 * NA_ROWS, PAIR, PAIR), lambda p, b, r: (p, 0, 0, 0))
    return pl.pallas_call(
        functools.partial(_na_kernel, rows=rows, span=span),
        out_shape=jax.ShapeDtypeStruct((zn.shape[0], NA_W), F32),
        grid=(npair, batch, steps),
        in_specs=[qcol, seqcol(npair), seqcol(2 * npair), ctxcol, ctxcol, bias_spec],
        out_specs=qcol,
        compiler_params=_cparams("arbitrary", "arbitrary", "arbitrary"),
        name="neighbourhood_attention",
    )(zn, zn, zn, kctx, vctx, bias)


def _outproj_kernel(x_ref, yg_ref, yc_ref, yn_ref, mod_ref, nw_ref, wg_ref, wc_ref, wn_ref, wq_ref,
                    xo_ref, h_ref, q_ref):
    y = (jnp.dot(yg_ref[...].astype(BF16), wg_ref[...], preferred_element_type=F32)
         + jnp.dot(yc_ref[...].astype(BF16), wc_ref[...], preferred_element_type=F32)
         + jnp.dot(yn_ref[...].astype(BF16), wn_ref[...], preferred_element_type=F32))
    x = x_ref[...] + mod_ref[0, 2:3, :] * y
    xo_ref[...] = x
    hb = _norm_mod(x, nw_ref[...], mod_ref[0, 4:5, :], mod_ref[0, 3:4, :]).astype(BF16)
    h_ref[...] = hb
    q_ref[...] = jnp.dot(hb, wq_ref[...], preferred_element_type=F32)


def _outproj(x, yg, yc, yn, mod, mod_row, nw, wts, wq):
    n = x.shape[0]
    wg, wc, wn = wts
    row = lambda w: pl.BlockSpec((TM, w), lambda i: (i, 0))
    return pl.pallas_call(
        _outproj_kernel,
        out_shape=(jax.ShapeDtypeStruct((n, D), F32), jax.ShapeDtypeStruct((n, D), BF16),
                   jax.ShapeDtypeStruct((n, wq.shape[1]), F32)),
        grid=(n // TM,),
        in_specs=[row(D), row(GLA_W), row(CONV_W), row(NA_W),
                  pl.BlockSpec((1, 6, D), lambda i: (mod_row(i), 0, 0)), _const_spec((1, D)),
                  _const_spec(wg.shape), _const_spec(wc.shape), _const_spec(wn.shape), _const_spec(wq.shape)],
        out_specs=(row(D), row(D), row(wq.shape[1])),
        compiler_params=_cparams("arbitrary"),
        name="out_projection",
    )(x, yg, yc, yn, mod, nw, wg, wc, wn, wq)


def _extract_top(s, dst_ref, want_rank=False):
    rank = jnp.full(s.shape, float(TOPK), F32) if want_rank else None
    for k in range(TOPK):
        m = jnp.max(s, axis=0, keepdims=True)
        dst_ref[k:k + 1, :] = m
        hit = s == m
        if want_rank:
            rank = jnp.where(hit, float(k), rank)
        s = jnp.where(hit, -jnp.inf, s)
    return rank


def _route_kernel(q_ref, kk_ref, r1_ref, e1_ref, c2_ref, e2_ref, v1_scr, v2_scr, c_scr):
    for h in range(ROUTE_HEADS_PER_STEP):
        qh = q_ref[:, h * LANES:(h + 1) * LANES]
        st = lax.dot_general(kk_ref[h], qh, NT, precision=HI, preferred_element_type=F32)
        s1 = st[:NKEYS]
        s2 = st[NKEYS:]
        r1_ref[h] = _extract_top(s1, v1_scr, want_rank=True)
        _extract_top(s2, v2_scr)
        v1 = v1_scr[...]
        v2 = v2_scr[...]
        half = TOPK // 2
        cand = jnp.concatenate([v1[0:1] + v2] + [v1[a:a + 1] + v2[:half] for a in range(1, half)]
                               + [v1[half:] + v2[0:1]], axis=0)
        _extract_top(cand, c_scr)
        top = c_scr[...]
        z = jnp.sum(jnp.exp(top - top[0:1]), axis=0, keepdims=True)
        tau = top[TOPK - 1:TOPK]
        count = jnp.zeros(s2.shape, F32)
        for a in range(half):
            count = count + jnp.where(v1[a:a + 1] + s2 >= tau, 1.0, 0.0)
        tail = jnp.sum(jnp.where(v1[half:] + v2[0:1] >= tau, 1.0, 0.0), axis=0, keepdims=True)
        count = count + jnp.where(s2 == v2[0:1], tail, 0.0)
        e1_ref[h] = jnp.exp(s1 - v1[0:1])
        c2_ref[h] = count.astype(BF16)
        e2_ref[h] = (jnp.exp(s2 - v2[0:1]) * (1.0 / z)).astype(BF16)


def _route(q, kk):
    n = q.shape[0]
    hps = ROUTE_HEADS_PER_STEP
    tok = pl.BlockSpec((hps, NKEYS, ROUTE_TM), lambda i, g: (g, 0, i))
    rows = jax.ShapeDtypeStruct((PEER_HEADS, NKEYS, n), F32)
    cols = jax.ShapeDtypeStruct((PEER_HEADS, NKEYS, n), BF16)
    top = pltpu.VMEM((TOPK, ROUTE_TM), F32)
    return pl.pallas_call(
        _route_kernel,
        out_shape=(rows, rows, cols, cols),
        grid=(n // ROUTE_TM, PEER_HEADS // hps),
        in_specs=[pl.BlockSpec((ROUTE_TM, hps * LANES), lambda i, g: (i, g)),
                  pl.BlockSpec((hps, 2 * NKEYS, LANES), lambda i, g: (g, 0, 0))],
        out_specs=(tok, tok, tok, tok),
        scratch_shapes=[top, top, top],
        compiler_params=_cparams("arbitrary", "arbitrary"),
        name="peer_routing",
    )(q, kk)


def _expert_kernel(h_ref, u_ref, vt_ref, r1_ref, e1_ref, c2_ref, e2_ref, x_ref, mod_ref, o_ref, acc, w_scr):
    eb = pl.program_id(1)

    @pl.when(eb == 0)
    def _():
        acc[...] = jnp.zeros_like(acc)

    hb = h_ref[...]
    pack = 16
    keys_per_sub = EXPERT_SUB // NKEYS

    nsub = EXPERT_BLOCK // EXPERT_SUB

    def pre_act(sub):
        rows = slice(sub * EXPERT_SUB, (sub + 1) * EXPERT_SUB)
        return lax.dot_general(u_ref[rows, :], hb, NT, preferred_element_type=F32).astype(BF16)

    def gated(sub, pre):
        act = _gelu_tanh(pre)
        tile = (NKEYS // pack, pack, TT)
        key0 = eb * (EXPERT_BLOCK // NKEYS) + sub * keys_per_sub
        for ii in range(keys_per_sub):
            gate = jnp.zeros(tile, BF16)
            for h in range(PEER_HEADS):
                r1 = jnp.broadcast_to(r1_ref[h, pl.ds(key0 + ii, 1), :], (pack, TT)).astype(BF16)
                e1 = jnp.broadcast_to(e1_ref[h, pl.ds(key0 + ii, 1), :], (pack, TT)).astype(BF16)
                keep = c2_ref[h].reshape(tile) > r1[None]
                gate = gate + jnp.where(keep, e1[None], 0.0) * e2_ref[h].reshape(tile)
            rows = slice(ii * NKEYS, (ii + 1) * NKEYS)
            w_scr[sub, rows, :] = gate.reshape(NKEYS, TT) * act[rows]
        return w_scr[sub]

    pre = pre_act(0)
    out = None
    for sub in range(nsub):
        nxt = pre_act(sub + 1) if sub + 1 < nsub else None
        part = jnp.dot(vt_ref[sub], gated(sub, pre), preferred_element_type=F32)
        out = part if out is None else out + part
        pre = nxt
    acc[...] += out

    @pl.when(eb == pl.num_programs(1) - 1)
    def _():
        o_ref[...] = x_ref[...] + mod_ref[0, 5:6, :] * acc[...].T


def _experts(h2, u, vt, route, x, mod, mod_row):
    n = h2.shape[0]
    tok = pl.BlockSpec((PEER_HEADS, NKEYS, TT), lambda i, e: (0, 0, i))
    return pl.pallas_call(
        _expert_kernel,
        out_shape=jax.ShapeDtypeStruct((n, D), F32),
        grid=(n // TT, u.shape[0] // EXPERT_BLOCK),
        in_specs=[pl.BlockSpec((TT, D), lambda i, e: (i, 0)),
                  pl.BlockSpec((EXPERT_BLOCK, D), lambda i, e: (e, 0)),
                  pl.BlockSpec((EXPERT_BLOCK // EXPERT_SUB, D, EXPERT_SUB), lambda i, e: (e, 0, 0)),
                  tok, tok, tok, tok,
                  pl.BlockSpec((TT, D), lambda i, e: (i, 0)),
                  pl.BlockSpec((1, 6, D), lambda i, e: (mod_row(i), 0, 0))],
        out_specs=pl.BlockSpec((TT, D), lambda i, e: (i, 0)),
        scratch_shapes=[pltpu.VMEM((D, TT), F32),
                        pltpu.VMEM((EXPERT_BLOCK // EXPERT_SUB, EXPERT_SUB, TT), BF16)],
        compiler_params=_cparams("arbitrary", "arbitrary"),
        name="peer_experts",
    )(h2, u, vt, *route, x, mod)


def _final_norm_kernel(x_ref, w_ref, o_ref):
    x = x_ref[...]
    o_ref[...] = x * lax.rsqrt(jnp.mean(x * x, axis=-1, keepdims=True) + EPS) * w_ref[...]


def _final_norm(x, w):
    n = x.shape[0]
    return pl.pallas_call(
        _final_norm_kernel,
        out_shape=jax.ShapeDtypeStruct((n, D), F32),
        grid=(n // TM,),
        in_specs=[pl.BlockSpec((TM, D), lambda i: (i, 0)), _const_spec((1, D))],
        out_specs=pl.BlockSpec((TM, D), lambda i: (i, 0)),
        compiler_params=_cparams("arbitrary"),
        name="final_norm",
    )(x, w)


def _layer_weights(l, w_in, w_af, b_af, w_ab, b_ab, w_out, peer_wq, peer_k1, peer_k2, peer_u, peer_v):
    o_lr = 4 * GLA_W
    o_conv = o_lr + 2 * LOWRANK
    o_na = o_conv + 3 * CONV_W
    wi = w_in[l]
    wg = wi[:, :o_lr].astype(BF16)
    wl = jnp.pad(wi[:, o_lr:o_conv], ((0, 0), (0, LANES - 2 * LOWRANK))).astype(BF16)
    wc = wi[:, o_conv:o_na].astype(BF16)
    wn = wi[:, o_na:].astype(BF16)
    wab = jnp.zeros((LANES, 2 * GLA_W), F32)
    wab = wab.at[:LOWRANK, :GLA_W].set(w_af[l]).at[LOWRANK:2 * LOWRANK, GLA_W:].set(w_ab[l])
    bab = jnp.concatenate([b_af[l], b_ab[l]])[None, :]
    wo = w_out[l].astype(BF16)
    wo = (wo[:GLA_W], wo[GLA_W:GLA_W + CONV_W], wo[GLA_W + CONV_W:])
    half = peer_k1.shape[-1]
    kk = jnp.concatenate([jnp.pad(peer_k1[l], ((0, 0), (0, 0), (0, half))),
                          jnp.pad(peer_k2[l], ((0, 0), (0, 0), (half, 0)))], axis=1)
    u = peer_u[l].astype(BF16)
    vt = jnp.swapaxes(peer_v[l].reshape(-1, EXPERT_SUB, D), 1, 2).astype(BF16)
    return (wg, wc, wn, wl, wab, bab), wo, peer_wq[l].astype(BF16), kk, u, vt


def kernel(x_prompt, x_sample, cache_na_k, cache_na_v, state_gla_fwd, state_gla_bwd, c, c_ctx, w_ada, b_ada, norm1_w, norm2_w, w_in, w_af, b_af, w_ab, b_ab, gla_norm_w, conv_w, conv_b, na_rpb, w_out, peer_wq, peer_k1, peer_k2, peer_u, peer_v, final_norm_w):
    bp, sp, _ = x_prompt.shape
    bs, ss, _ = x_sample.shape
    depth = w_ada.shape[0]
    xp = x_prompt.reshape(bp * sp, D)
    xs = x_sample.reshape(bs * ss, D)

    cvec = jnp.zeros((8, D), F32).at[0].set(c_ctx).at[1:1 + bs].set(c)
    mods = _modulation(cvec, w_ada, b_ada)
    kctx = jnp.swapaxes(cache_na_k, 2, 3).reshape(bs, depth, -1, NA_W)
    vctx = jnp.swapaxes(cache_na_v, 2, 3).reshape(bs, depth, -1, NA_W)

    prompt_row = lambda i: 0
    sample_row = lambda tile: (lambda i: 1 + i // (ss // tile))

    new_k, new_v, new_sf, new_sb = [], [], [], []
    for l in range(depth):
        inw, wo, wq, kk, u, vt = _layer_weights(l, w_in, w_af, b_af, w_ab, b_ab, w_out,
                                                peer_wq, peer_k1, peer_k2, peer_u, peer_v)
        mod = mods[l]
        n1, n2 = norm1_w[l][None, :], norm2_w[l][None, :]
        gnw = gla_norm_w[l][None, :]
        cw, cbias = conv_w[l], conv_b[l][None, :]
        bias = _na_bias_table(na_rpb[l])
        s0 = (_states_to_blockdiag(state_gla_fwd[:, l]), _states_to_blockdiag(state_gla_bwd[:, l]))

        zg, la, zc, zn = _inproj(xp, mod, prompt_row, n1, inw)
        yg, sf, sb = _gla(zg, la, gnw, bp, sp, None)
        yc = _conv(zc, cw, cbias, bp, sp)
        yn = _dense_attn(zn, bp, sp)
        xp, h2, q = _outproj(xp, yg, yc, yn, mod, prompt_row, n2, wo, wq)
        xp = _experts(h2, u, vt, _route(q, kk), xp, mod, prompt_row)
        heads = lambda a: a.reshape(bp, sp, NA_W // HD, HD).transpose(0, 2, 1, 3)
        new_k.append(heads(zn[:, NA_W:2 * NA_W]))
        new_v.append(heads(zn[:, 2 * NA_W:]))
        new_sf.append(_blockdiag_to_states(sf))
        new_sb.append(_blockdiag_to_states(sb))

        zg, la, zc, zn = _inproj(xs, mod, sample_row(TM), n1, inw)
        yg = _gla(zg, la, gnw, bs, ss, s0)
        yc = _conv(zc, cw, cbias, bs, ss)
        yn = _neighbourhood_attn(zn, kctx[:, l:l + 1], vctx[:, l:l + 1], bias, bs, ss)
        xs, h2, q = _outproj(xs, yg, yc, yn, mod, sample_row(TM), n2, wo, wq)
        xs = _experts(h2, u, vt, _route(q, kk), xs, mod, sample_row(TT))

    fw = final_norm_w[None, :]
    y_prompt = _final_norm(xp, fw).reshape(bp, sp, D)
    y_sample = _final_norm(xs, fw).reshape(bs, ss, D)
    return (y_prompt, y_sample, jnp.stack(new_k, axis=1), jnp.stack(new_v, axis=1),
            jnp.stack(new_sf, axis=1), jnp.stack(new_sb, axis=1))
```

```python
import functools

import numpy as np
import jax
import jax.numpy as jnp
from jax import lax
from jax.experimental import pallas as pl
from jax.experimental.pallas import tpu as pltpu

F32 = jnp.float32
BF16 = jnp.bfloat16
HI = lax.Precision.HIGHEST
NT = (((1,), (1,)), ((), ()))
TN = (((0,), (0,)), ((), ()))

LANES = 128
EPS = 1e-6
D = 1024
HD = 64
PAIR = 2 * HD
GLA_W = 384
CONV_W = 256
NA_W = 384
LOWRANK = 16
GLA_CHUNK = 64
GLA_SUB = 16
GLA_OUT_ROWS = 256
GLA_GROUP = 8
GLA_TAU = 16.0
EXP_CLAMP = 80.0
GRID_W = 64
NA_ROWS = 8
NA_COLS = 16
NA_ROWS_PER_STEP = 4
NA_BLOCKS_PER_STEP = 4
PEER_HEADS = 8
NKEYS = 128
TOPK = 16
TM = 512
CUM_ROWS = 256
TT = 512
EXPERT_BLOCK = 2048
EXPERT_SUB = 512
VALUE_CHUNK = 2048
ROUTE_HEADS_PER_STEP = 8
ROUTE_TM = 128
VMEM_LIMIT = 56 * 1024 * 1024


def _cparams(*sem):
    return pltpu.CompilerParams(dimension_semantics=sem, vmem_limit_bytes=VMEM_LIMIT)


def _silu(x):
    return x * jax.nn.sigmoid(x)


def _gelu_tanh(x):
    c0 = float(np.sqrt(2.0 / np.pi))
    z = x * (c0 + (0.044715 * c0) * (x * x))
    hx = 0.5 * x
    return hx + hx * jnp.tanh(z)


def _norm_mod(x, w, scale, shift):
    ms = jnp.mean(x * x, axis=-1, keepdims=True)
    return x * lax.rsqrt(ms + EPS) * w * (1.0 + scale) + shift


def _head_masks():
    lane = lax.broadcasted_iota(jnp.int32, (1, PAIR), 1)
    m0 = (lane < HD).astype(F32)
    return m0, 1.0 - m0


def _mod_kernel(c_ref, w_ref, b_ref, o_ref):
    s = _silu(c_ref[...]).astype(BF16)
    o_ref[0] = jnp.dot(s, w_ref[0].astype(BF16), preferred_element_type=F32) + b_ref[0]


def _modulation(cvec, w_ada, b_ada):
    depth = w_ada.shape[0]
    nb = w_ada.shape[2] // D
    out = pl.pallas_call(
        _mod_kernel,
        out_shape=jax.ShapeDtypeStruct((depth, 8, nb * D), F32),
        grid=(depth, nb),
        in_specs=[pl.BlockSpec((8, D), lambda l, j: (0, 0)),
                  pl.BlockSpec((1, D, D), lambda l, j: (l, 0, j)),
                  pl.BlockSpec((1, 1, D), lambda l, j: (l, 0, j))],
        out_specs=pl.BlockSpec((1, 8, D), lambda l, j: (l, 0, j)),
        compiler_params=_cparams("arbitrary", "arbitrary"),
        name="adaln_modulation",
    )(cvec, w_ada, b_ada.reshape(depth, 1, nb * D))
    return out.reshape(depth, 8, nb, D)


def _inproj_kernel(x_ref, mod_ref, nw_ref, wg_ref, wc_ref, wn_ref, wl_ref, wab_ref, bab_ref,
                   zg_ref, la_ref, zc_ref, zn_ref):
    h = _norm_mod(x_ref[...], nw_ref[...], mod_ref[0, 1:2, :], mod_ref[0, 0:1, :])
    hb = h.astype(BF16)
    zg_ref[...] = jnp.dot(hb, wg_ref[...], preferred_element_type=F32)
    zc_ref[...] = jnp.dot(hb, wc_ref[...], preferred_element_type=F32)
    zn_ref[...] = jnp.dot(hb, wn_ref[...], preferred_element_type=F32)
    lr = jnp.dot(hb, wl_ref[...], preferred_element_type=F32)
    hi = lr.astype(BF16)
    rest = lr - hi.astype(F32)
    mid = rest.astype(BF16)
    lo = (rest - mid.astype(F32)).astype(BF16)
    grp = lax.broadcasted_iota(jnp.int32, (1, lr.shape[1]), 1) >> 5
    pieces = jnp.where((grp == 2) | (grp == 5), mid, jnp.where(grp == 4, lo, hi))
    zz = jnp.dot(pieces, wab_ref[...], preferred_element_type=F32) + bab_ref[...]
    la = (jnp.minimum(zz, 0.0) - jnp.log(1.0 + jnp.exp(-jnp.abs(zz)))) * (1.0 / GLA_TAU)
    r = lax.broadcasted_iota(jnp.int32, (CUM_ROWS, CUM_ROWS), 0)
    s = lax.broadcasted_iota(jnp.int32, (CUM_ROWS, CUM_ROWS), 1)
    shift = GLA_CHUNK.bit_length() - 1
    same = (r >> shift) == (s >> shift)
    tri_f = (same & (s <= r)).astype(BF16)
    tri_b = (same & (s >= r)).astype(BF16)
    hi = la.astype(BF16)
    rest = la - hi.astype(F32)
    mid = rest.astype(BF16)
    lo = (rest - mid.astype(F32)).astype(BF16)
    for blk in range(TM // CUM_ROWS):
        rows = slice(blk * CUM_ROWS, (blk + 1) * CUM_ROWS)
        for tri, cols in ((tri_f, slice(0, GLA_W)), (tri_b, slice(GLA_W, 2 * GLA_W))):
            la_ref[rows, cols] = (jnp.dot(tri, hi[rows, cols], preferred_element_type=F32)
                                  + jnp.dot(tri, mid[rows, cols], preferred_element_type=F32)
                                  + jnp.dot(tri, lo[rows, cols], preferred_element_type=F32))


def _const_spec(shape):
    return pl.BlockSpec(shape, lambda *_: (0,) * len(shape))


def _inproj(x, mod, mod_row, nw, wts):
    n = x.shape[0]
    wg, wc, wn, wl, wab, bab = wts
    row = lambda w: pl.BlockSpec((TM, w), lambda i: (i, 0))
    return pl.pallas_call(
        _inproj_kernel,
        out_shape=(jax.ShapeDtypeStruct((n, 4 * GLA_W), F32), jax.ShapeDtypeStruct((n, 2 * GLA_W), F32),
                   jax.ShapeDtypeStruct((n, 3 * CONV_W), F32), jax.ShapeDtypeStruct((n, 3 * NA_W), F32)),
        grid=(n // TM,),
        in_specs=[row(D), pl.BlockSpec((1, 6, D), lambda i: (mod_row(i), 0, 0)), _const_spec((1, D)),
                  _const_spec(wg.shape), _const_spec(wc.shape), _const_spec(wn.shape), _const_spec(wl.shape),
                  _const_spec(wab.shape), _const_spec(bab.shape)],
        out_specs=(row(4 * GLA_W), row(2 * GLA_W), row(3 * CONV_W), row(3 * NA_W)),
        compiler_params=_cparams("arbitrary"),
        name="in_projection",
    )(x, mod, nw, wg, wc, wn, wl, wab, bab)


def _gla_consts(fwd):
    c, sb = GLA_CHUNK, GLA_SUB
    r = lax.broadcasted_iota(jnp.int32, (2 * c, c), 0) & (c - 1)
    s = lax.broadcasted_iota(jnp.int32, (2 * c, c), 1)
    caus = (s <= r) if fwd else (s >= r)
    rowid = lax.broadcasted_iota(jnp.int32, (c, PAIR), 0)
    seen = [(rowid < (i + 1) * sb) if fwd else (rowid >= i * sb) for i in range(c // sb)]
    m0, m1 = _head_masks()
    qsel = [[((rowid >= i * sb) & (rowid < (i + 1) * sb)).astype(F32) * m for i in range(c // sb)]
            for m in (m0, m1)]
    return caus, seen, qsel


def _gla_operands(qc, kc, vc, cum, fwd, consts):
    _, seen, qsel = consts
    c, sb = GLA_CHUNK, GLA_SUB
    nsb = c // sb
    zero_row = jnp.zeros((1, PAIR), F32)
    if fwd:
        last = cum[c - 1:c]
        starts = [zero_row] + [cum[i * sb - 1:i * sb] for i in range(1, nsb)]
    else:
        last = cum[0:1]
        starts = [cum[(i + 1) * sb:(i + 1) * sb + 1] for i in range(nsb - 1)] + [zero_row]
    bm = jnp.concatenate([jnp.broadcast_to(b, (sb, PAIR)) for b in starts], axis=0)
    qt = qc * jnp.exp(cum - bm)
    kbig = jnp.concatenate(
        [(kc * jnp.exp(jnp.where(seen[i], jnp.minimum(starts[i] - cum, EXP_CLAMP), 0.0))).astype(BF16)
         for i in range(nsb)], axis=1)
    qbig = jnp.concatenate(
        [jnp.concatenate([(qt * qsel[h][i]).astype(BF16) for i in range(nsb)], axis=1) for h in range(2)],
        axis=0)
    return dict(qbig=qbig, kbig=kbig, vb=vc.astype(BF16), qdec=(qc * jnp.exp(cum)).astype(BF16),
                khat=(kc * jnp.exp(last - cum)).astype(BF16), decay=jnp.exp(last))


def _gla_group(chunks, st, consts, masks, bd):
    caus = consts[0]
    m0, m1 = masks
    c = GLA_CHUNK
    atts = [lax.dot_general(ch['qbig'], ch['kbig'], NT, preferred_element_type=F32) for ch in chunks]
    upds = [lax.dot_general(ch['vb'], ch['khat'], TN, preferred_element_type=F32) for ch in chunks]
    atts = [jnp.where(caus, a, 0.0).astype(BF16) for a in atts]
    intras = [jnp.dot(a[:c], ch['vb'], preferred_element_type=F32) * m0
              + jnp.dot(a[c:], ch['vb'], preferred_element_type=F32) * m1 for a, ch in zip(atts, chunks)]
    outs = []
    for ch, upd, intra in zip(chunks, upds, intras):
        outs.append(intra + lax.dot_general(ch['qdec'], st.astype(BF16), NT, preferred_element_type=F32))
        st = st * ch['decay'] + upd * bd
    return outs, st


def _gla_kernel(*refs, seq, state_in):
    if state_in:
        q_ref, k_ref, v_ref, g_ref, laf_ref, lab_ref, nw_ref, s0f_ref, s0b_ref, y_ref, of_scr, ob_scr = refs
    else:
        q_ref, k_ref, v_ref, g_ref, laf_ref, lab_ref, nw_ref, y_ref, sf_ref, sb_ref, of_scr, ob_scr = refs
    c = GLA_CHUNK
    nc = seq // c
    masks = _head_masks()
    m0, m1 = masks
    rr = lax.broadcasted_iota(jnp.int32, (PAIR, PAIR), 0)
    cc = lax.broadcasted_iota(jnp.int32, (PAIR, PAIR), 1)
    bd = ((rr < HD) == (cc < HD)).astype(F32)
    cf = _gla_consts(True)
    cb = _gla_consts(False)
    scale = HD ** -0.5
    nw = nw_ref[...]

    def rows(i):
        return pl.ds(pl.multiple_of(i * c, c), c)

    grp = min(GLA_GROUP, nc)

    def scan_body(j, carry):
        stf, stb = carry
        slf = [rows(j * grp + k) for k in range(grp)]
        slb = [rows(nc - 1 - (j * grp + k)) for k in range(grp)]
        chf = [_gla_operands(q_ref[s, :] * scale, k_ref[s, :], v_ref[s, :], laf_ref[s, :], True, cf)
               for s in slf]
        chb = [_gla_operands(q_ref[s, :] * scale, k_ref[s, :], v_ref[s, :], lab_ref[s, :], False, cb)
               for s in slb]
        of, stf = _gla_group(chf, stf, cf, masks, bd)
        ob, stb = _gla_group(chb, stb, cb, masks, bd)
        for s, o in zip(slf, of):
            of_scr[s, :] = o
        for s, o in zip(slb, ob):
            ob_scr[s, :] = o
        return stf, stb

    def out_body(i, carry):
        sl = pl.ds(pl.multiple_of(i * GLA_OUT_ROWS, GLA_OUT_ROWS), GLA_OUT_ROWS)
        tot = of_scr[sl, :] + ob_scr[sl, :]
        sq = tot * tot
        ms = (jnp.sum(sq * m0, axis=-1, keepdims=True) * m0
              + jnp.sum(sq * m1, axis=-1, keepdims=True) * m1) * (1.0 / HD)
        y_ref[sl, :] = tot * lax.rsqrt(ms + EPS) * nw * _silu(g_ref[sl, :])
        return carry

    zero = jnp.zeros((PAIR, PAIR), F32)
    init = (s0f_ref[0, 0], s0b_ref[0, 0]) if state_in else (zero, zero)
    sf, sb = lax.fori_loop(0, nc // grp, scan_body, init)
    lax.fori_loop(0, seq // GLA_OUT_ROWS, out_body, 0)
    if not state_in:
        sf_ref[0, 0] = sf
        sb_ref[0, 0] = sb


def _gla(zg, la, nw, batch, seq, states):
    n = zg.shape[0]
    npair = GLA_W // PAIR
    col = lambda off: pl.BlockSpec((seq, PAIR), lambda b, p: (b, off + p))
    st_spec = pl.BlockSpec((1, 1, PAIR, PAIR), lambda b, p: (b, p, 0, 0))
    in_specs = [col(0), col(npair), col(2 * npair), col(3 * npair), col(0), col(npair),
                pl.BlockSpec((1, PAIR), lambda b, p: (0, p))]
    args = [zg, zg, zg, zg, la, la, nw]
    y_shape = jax.ShapeDtypeStruct((n, GLA_W), F32)
    y_spec = pl.BlockSpec((seq, PAIR), lambda b, p: (b, p))
    if states is None:
        st_shape = jax.ShapeDtypeStruct((batch, npair, PAIR, PAIR), F32)
        out_shape, out_specs = (y_shape, st_shape, st_shape), (y_spec, st_spec, st_spec)
    else:
        in_specs += [st_spec, st_spec]
        args += list(states)
        out_shape, out_specs = y_shape, y_spec
    return pl.pallas_call(
        functools.partial(_gla_kernel, seq=seq, state_in=states is not None),
        out_shape=out_shape,
        grid=(batch, npair),
        in_specs=in_specs,
        out_specs=out_specs,
        scratch_shapes=[pltpu.VMEM((seq, PAIR), F32), pltpu.VMEM((seq, PAIR), F32)],
        compiler_params=_cparams("arbitrary", "arbitrary"),
        name="gla_bidir",
    )(*args)


def _states_to_blockdiag(s):
    b, h = s.shape[:2]
    st = jnp.swapaxes(s, -1, -2).reshape(b, h // 2, 2, HD, HD)
    eye = jnp.eye(2, dtype=s.dtype)
    return jnp.einsum('bpivk,ij->bpivjk', st, eye).reshape(b, h // 2, PAIR, PAIR)


def _blockdiag_to_states(sbd):
    b, p = sbd.shape[:2]
    s6 = sbd.reshape(b, p, 2, HD, 2, HD)
    diag = jnp.stack([s6[:, :, 0, :, 0, :], s6[:, :, 1, :, 1, :]], axis=2)
    return jnp.swapaxes(diag, -1, -2).reshape(b, 2 * p, HD, HD)


def _conv_kernel(ch_ref, cb_ref, cc_ref, w_ref, b_ref, y_ref, *, seq):
    u = cc_ref[...] * ch_ref[...]
    row = lax.broadcasted_iota(jnp.int32, u.shape, 0)
    prev = jnp.where(row == 0, 0.0, pltpu.roll(u, 1, 0))
    nxt = jnp.where(row == seq - 1, 0.0, pltpu.roll(u, seq - 1, 0))
    y_ref[...] = cb_ref[...] * (w_ref[0:1, :] * prev + w_ref[1:2, :] * u + w_ref[2:3, :] * nxt + b_ref[...])


def _conv(zc, w, b, batch, seq):
    col = lambda j: pl.BlockSpec((seq, CONV_W), lambda i: (i, j))
    return pl.pallas_call(
        functools.partial(_conv_kernel, seq=seq),
        out_shape=jax.ShapeDtypeStruct((zc.shape[0], CONV_W), F32),
        grid=(batch,),
        in_specs=[col(0), col(1), col(2), _const_spec(w.shape), _const_spec(b.shape)],
        out_specs=col(0),
        compiler_params=_cparams("arbitrary"),
        name="gated_conv",
    )(zc, zc, zc, w, b)


def _dense_attn_kernel(q_ref, k_ref, v_ref, o_ref):
    q = q_ref[...] * (HD ** -0.5)
    kb = k_ref[...].astype(BF16)
    vb = v_ref[...].astype(BF16)
    out = jnp.zeros(q.shape, F32)
    for m in _head_masks():
        s = lax.dot_general((q * m).astype(BF16), kb, NT, preferred_element_type=F32)
        p = jnp.exp(s - jnp.max(s, axis=-1, keepdims=True))
        o = jnp.dot(p.astype(BF16), vb, preferred_element_type=F32)
        out = out + o * (m / jnp.sum(p, axis=-1, keepdims=True))
    o_ref[...] = out


def _dense_attn(zn, batch, seq):
    npair = NA_W // PAIR
    col = lambda off: pl.BlockSpec((seq, PAIR), lambda b, p: (b, off + p))
    return pl.pallas_call(
        _dense_attn_kernel,
        out_shape=jax.ShapeDtypeStruct((zn.shape[0], NA_W), F32),
        grid=(batch, npair),
        in_specs=[col(0), col(npair), col(2 * npair)],
        out_specs=col(0),
        compiler_params=_cparams("arbitrary", "arbitrary"),
        name="context_attention",
    )(zn, zn, zn)


def _na_kernel(q_ref, k_ref, v_ref, kc_ref, vc_ref, tp_ref, o_ref, *, rows, span):
    rb = NA_ROWS_PER_STEP
    kc = kc_ref[0, 0].astype(BF16)
    vc = vc_ref[0, 0].astype(BF16)
    m0, m1 = _head_masks()
    shift = GRID_W.bit_length() - 1
    keyrow = lax.broadcasted_iota(jnp.int32, (1, span * GRID_W), 1) >> shift

    def block(bi):
        g = pl.program_id(2) * NA_BLOCKS_PER_STEP + bi
        q0 = bi * rb * GRID_W
        us = jnp.clip(g * rb - NA_ROWS // 2, 0, rows - span)
        win = pl.ds(pl.multiple_of(us * GRID_W, GRID_W), span * GRID_W)
        kw = k_ref[win, :].astype(BF16)
        vw = v_ref[win, :].astype(BF16)
        qs = []
        for rr in range(rb):
            q = q_ref[q0 + rr * GRID_W:q0 + (rr + 1) * GRID_W, :] * (HD ** -0.5)
            qs += [q * m0, q * m1]
        qstack = jnp.concatenate(qs, axis=0).astype(BF16)
        sw = lax.dot_general(qstack, kw, NT, preferred_element_type=F32)
        sc = lax.dot_general(qstack, kc, NT, preferred_element_type=F32)
        parts = []
        for rr in range(rb):
            r = g * rb + rr
            lo = jnp.clip(r - NA_ROWS // 2, 0, rows - NA_ROWS) - us
            inside = (keyrow >= lo) & (keyrow < lo + NA_ROWS)
            blk = sw[rr * PAIR:(rr + 1) * PAIR]
            tiles = [blk[:, jp * PAIR:(jp + 1) * PAIR]
                     + tp_ref[0, jnp.clip(us + 2 * jp - r + NA_ROWS, 0, 2 * NA_ROWS - 1)]
                     for jp in range(span // 2)]
            parts.append(jnp.where(inside, jnp.concatenate(tiles, axis=1), -1e30))
        sw = jnp.concatenate(parts, axis=0)
        mx = jnp.maximum(jnp.max(sw, axis=-1, keepdims=True), jnp.max(sc, axis=-1, keepdims=True))
        pw = jnp.exp(sw - mx)
        pc = jnp.exp(sc - mx)
        den = jnp.sum(pw, axis=-1, keepdims=True) + jnp.sum(pc, axis=-1, keepdims=True)
        o = (jnp.dot(pw.astype(BF16), vw, preferred_element_type=F32)
             + jnp.dot(pc.astype(BF16), vc, preferred_element_type=F32)) / den
        for rr in range(rb):
            o_ref[q0 + rr * GRID_W:q0 + (rr + 1) * GRID_W, :] = (o[rr * PAIR:rr * PAIR + HD] * m0
                                                                 + o[rr * PAIR + HD:(rr + 1) * PAIR] * m1)

    for bi in range(NA_BLOCKS_PER_STEP):
        block(bi)


def _na_bias_table(rpb):
    nh = rpb.shape[0]
    cols = np.arange(GRID_W)
    cs = np.clip(cols - NA_COLS // 2, 0, GRID_W - NA_COLS)
    col_mask = (cols[None, :] >= cs[:, None]) & (cols[None, :] < cs[:, None] + NA_COLS)
    dc = np.clip(cols[None, :] - cols[:, None], -(NA_COLS - 1), NA_COLS - 1) + NA_COLS - 1
    onehot = (dc[:, :, None] == np.arange(2 * NA_COLS - 1)).astype(np.float32)
    toep = jnp.einsum('hab,qkb->haqk', rpb.astype(F32), onehot, precision=HI)
    toep = jnp.where(col_mask[None, None], toep, -1e30)
    ext = jnp.pad(toep, ((0, 0), (1, 1), (0, 0), (0, 0)), constant_values=-1e30)
    two = jnp.concatenate([ext[:, :-1], ext[:, 1:]], axis=-1)
    two = two.reshape(nh // 2, 2, 2 * NA_ROWS, GRID_W, 2 * GRID_W).transpose(0, 2, 1, 3, 4)
    return two.reshape(nh // 2, 2 * NA_ROWS, PAIR, PAIR)


def _neighbourhood_attn(zn, kctx, vctx, bias, batch, seq):
    npair = NA_W // PAIR
    rows = seq // GRID_W
    ctx = kctx.shape[2]
    rb = NA_ROWS_PER_STEP
    nblk = NA_BLOCKS_PER_STEP
    assert rows % (rb * nblk) == 0 and rows >= NA_ROWS
    steps = rows // (rb * nblk)
    span = min(rows, NA_ROWS + rb)
    qcol = pl.BlockSpec((nblk * rb * GRID_W, PAIR), lambda p, b, r: (b * steps + r, p))
    seqcol = lambda off: pl.BlockSpec((seq, PAIR), lambda p, b, r: (b, off + p))
    ctxcol = pl.BlockSpec((1, 1, ctx, PAIR), lambda p, b, r: (b, 0, 0, p))
    bias_spec = pl.BlockSpec((1, 2 * NA_ROWS, PAIR, PAIR), lambda p, b, r: (p, 0, 0, 0))
    return pl.pallas_call(
        functools.partial(_na_kernel, rows=rows, span=span),
        out_shape=jax.ShapeDtypeStruct((zn.shape[0], NA_W), F32),
        grid=(npair, batch, steps),
        in_specs=[qcol, seqcol(npair), seqcol(2 * npair), ctxcol, ctxcol, bias_spec],
        out_specs=qcol,
        compiler_params=_cparams("arbitrary", "arbitrary", "arbitrary"),
        name="neighbourhood_attention",
    )(zn, zn, zn, kctx, vctx, bias)


def _outproj_kernel(x_ref, yg_ref, yc_ref, yn_ref, mod_ref, nw_ref, wg_ref, wc_ref, wn_ref, wq_ref,
                    xo_ref, h_ref, q_ref):
    y = (jnp.dot(yg_ref[...].astype(BF16), wg_ref[...], preferred_element_type=F32)
         + jnp.dot(yc_ref[...].astype(BF16), wc_ref[...], preferred_element_type=F32)
         + jnp.dot(yn_ref[...].astype(BF16), wn_ref[...], preferred_element_type=F32))
    x = x_ref[...] + mod_ref[0, 2:3, :] * y
    xo_ref[...] = x
    hb = _norm_mod(x, nw_ref[...], mod_ref[0, 4:5, :], mod_ref[0, 3:4, :]).astype(BF16)
    h_ref[...] = hb
    q_ref[...] = jnp.dot(hb, wq_ref[...], preferred_element_type=F32)


def _outproj(x, yg, yc, yn, mod, mod_row, nw, wts, wq):
    n = x.shape[0]
    wg, wc, wn = wts
    row = lambda w: pl.BlockSpec((TM, w), lambda i: (i, 0))
    return pl.pallas_call(
        _outproj_kernel,
        out_shape=(jax.ShapeDtypeStruct((n, D), F32), jax.ShapeDtypeStruct((n, D), BF16),
                   jax.ShapeDtypeStruct((n, wq.shape[1]), F32)),
        grid=(n // TM,),
        in_specs=[row(D), row(GLA_W), row(CONV_W), row(NA_W),
                  pl.BlockSpec((1, 6, D), lambda i: (mod_row(i), 0, 0)), _const_spec((1, D)),
                  _const_spec(wg.shape), _const_spec(wc.shape), _const_spec(wn.shape), _const_spec(wq.shape)],
        out_specs=(row(D), row(D), row(wq.shape[1])),
        compiler_params=_cparams("arbitrary"),
        name="out_projection",
    )(x, yg, yc, yn, mod, nw, wg, wc, wn, wq)


def _extract_top(s, dst_ref, want_rank=False):
    rank = jnp.full(s.shape, float(TOPK), F32) if want_rank else None
    for k in range(TOPK):
        m = jnp.max(s, axis=0, keepdims=True)
        dst_ref[k:k + 1, :] = m
        hit = s == m
        if want_rank:
            rank = jnp.where(hit, float(k), rank)
        s = jnp.where(hit, -jnp.inf, s)
    return rank


def _sort16_network():
    def merge(lo, hi, r):
        step = r * 2
        if step < hi - lo:
            yield from merge(lo, hi, step)
            yield from merge(lo + r, hi, step)
            yield from [(i, i + r) for i in range(lo + r, hi - r, step)]
        else:
            yield (lo, lo + r)

    def sort(lo, hi):
        if hi - lo >= 1:
            mid = lo + (hi - lo) // 2
            yield from sort(lo, mid)
            yield from sort(mid + 1, hi)
            yield from merge(lo, hi, 1)

    return tuple(sort(0, TOPK - 1))


_SORT16 = _sort16_network()
_BITONIC16 = tuple((i, i + d) for d in (8, 4, 2, 1) for i in range(TOPK) if not i & d)


def _exchange(v, pairs):
    v = list(v)
    for i, j in pairs:
        a, b = v[i], v[j]
        if b is None:
            continue
        if a is None:
            v[i], v[j] = b, None
        else:
            v[i], v[j] = jnp.maximum(a, b), jnp.minimum(a, b)
    return v


def _top16_sorted(tiles):
    v = _exchange(list(tiles) + [None] * (TOPK - len(tiles)), _SORT16)
    for shift in (4, 2, 1):
        other = [None if t is None else pltpu.roll(t, shift, 0) for t in v]
        merged = []
        for k in range(TOPK):
            a, b = v[k], other[TOPK - 1 - k]
            merged.append(b if a is None else a if b is None else jnp.maximum(a, b))
        v = _exchange(merged, _BITONIC16)
    return v


def _route_kernel(q_ref, kk_ref, r1_ref, e1_ref, c2_ref, e2_ref, v1_scr, v2_scr):
    sub = 8
    for h in range(ROUTE_HEADS_PER_STEP):
        qh = q_ref[:, h * LANES:(h + 1) * LANES]
        st = lax.dot_general(kk_ref[h], qh, NT, precision=HI, preferred_element_type=F32)
        s1 = st[:NKEYS]
        s2 = st[NKEYS:]
        r1_ref[h] = _extract_top(s1, v1_scr, want_rank=True)
        top2 = _top16_sorted([s2[g * sub:(g + 1) * sub] for g in range(NKEYS // sub)])
        for k in range(TOPK):
            v2_scr[k:k + 1, :] = top2[k][0:1]
        v1 = v1_scr[...]
        v2 = v2_scr[...]
        half = TOPK // 2
        cand = ([v1[0:1] + v2[:half], v1[0:1] + v2[half:]] + [v1[a:a + 1] + v2[:half] for a in range(1, half)]
                + [v1[half:] + v2[0:1]])
        top = _top16_sorted(cand)
        z = sum(jnp.exp(t - top[0]) for t in top)[0:1]
        tau = top[TOPK - 1][0:1]
        count = jnp.zeros(s2.shape, F32)
        for a in range(half):
            count = count + jnp.where(v1[a:a + 1] + s2 >= tau, 1.0, 0.0)
        tail = jnp.sum(jnp.where(v1[half:] + v2[0:1] >= tau, 1.0, 0.0), axis=0, keepdims=True)
        count = count + jnp.where(s2 == v2[0:1], tail, 0.0)
        e1_ref[h] = jnp.exp(s1 - v1[0:1])
        c2_ref[h] = count.astype(BF16)
        e2_ref[h] = (jnp.exp(s2 - v2[0:1]) * (1.0 / z)).astype(BF16)


def _route(q, kk):
    n = q.shape[0]
    hps = ROUTE_HEADS_PER_STEP
    tok = pl.BlockSpec((hps, NKEYS, ROUTE_TM), lambda i, g: (g, 0, i))
    rows = jax.ShapeDtypeStruct((PEER_HEADS, NKEYS, n), F32)
    cols = jax.ShapeDtypeStruct((PEER_HEADS, NKEYS, n), BF16)
    top = pltpu.VMEM((TOPK, ROUTE_TM), F32)
    return pl.pallas_call(
        _route_kernel,
        out_shape=(rows, rows, cols, cols),
        grid=(n // ROUTE_TM, PEER_HEADS // hps),
        in_specs=[pl.BlockSpec((ROUTE_TM, hps * LANES), lambda i, g: (i, g)),
                  pl.BlockSpec((hps, 2 * NKEYS, LANES), lambda i, g: (g, 0, 0))],
        out_specs=(tok, tok, tok, tok),
        scratch_shapes=[top, top],
        compiler_params=_cparams("arbitrary", "arbitrary"),
        name="peer_routing",
    )(q, kk)


def _expert_kernel(h_ref, u_ref, vt_ref, r1_ref, e1_ref, c2_ref, e2_ref, x_ref, mod_ref, o_ref, acc, w_scr):
    eb = pl.program_id(1)

    @pl.when(eb == 0)
    def _():
        acc[...] = jnp.zeros_like(acc)

    hb = h_ref[...]
    pack = 16
    keys_per_sub = EXPERT_SUB // NKEYS

    nsub = EXPERT_BLOCK // EXPERT_SUB

    def pre_act(sub):
        rows = slice(sub * EXPERT_SUB, (sub + 1) * EXPERT_SUB)
        return lax.dot_general(u_ref[rows, :], hb, NT, preferred_element_type=F32).astype(BF16)

    def gated(sub, pre):
        act = _gelu_tanh(pre)
        tile = (NKEYS // pack, pack, TT)
        key0 = eb * (EXPERT_BLOCK // NKEYS) + sub * keys_per_sub
        for ii in range(keys_per_sub):
            gate = jnp.zeros(tile, BF16)
            for h in range(PEER_HEADS):
                r1 = jnp.broadcast_to(r1_ref[h, pl.ds(key0 + ii, 1), :], (pack, TT)).astype(BF16)
                e1 = jnp.broadcast_to(e1_ref[h, pl.ds(key0 + ii, 1), :], (pack, TT)).astype(BF16)
                keep = c2_ref[h].reshape(tile) > r1[None]
                gate = gate + jnp.where(keep, e1[None], 0.0) * e2_ref[h].reshape(tile)
            rows = slice(ii * NKEYS, (ii + 1) * NKEYS)
            w_scr[sub * EXPERT_SUB + ii * NKEYS:sub * EXPERT_SUB + (ii + 1) * NKEYS, :] = (
                gate.reshape(NKEYS, TT) * act[rows])

    per_chunk = VALUE_CHUNK // EXPERT_SUB
    pre = pre_act(0)
    out = None
    for sub in range(nsub):
        nxt = pre_act(sub + 1) if sub + 1 < nsub else None
        gated(sub, pre)
        if (sub + 1) % per_chunk == 0:
            ch = sub // per_chunk
            part = jnp.dot(vt_ref[ch], w_scr[ch * VALUE_CHUNK:(ch + 1) * VALUE_CHUNK, :],
                           preferred_element_type=F32)
            out = part if out is None else out + part
        pre = nxt
    acc[...] += out

    @pl.when(eb == pl.num_programs(1) - 1)
    def _():
        o_ref[...] = x_ref[...] + mod_ref[0, 5:6, :] * acc[...].T


def _experts(h2, u, vt, route, x, mod, mod_row):
    n = h2.shape[0]
    tok = pl.BlockSpec((PEER_HEADS, NKEYS, TT), lambda i, e: (0, 0, i))
    return pl.pallas_call(
        _expert_kernel,
        out_shape=jax.ShapeDtypeStruct((n, D), F32),
        grid=(n // TT, u.shape[0] // EXPERT_BLOCK),
        in_specs=[pl.BlockSpec((TT, D), lambda i, e: (i, 0)),
                  pl.BlockSpec((EXPERT_BLOCK, D), lambda i, e: (e, 0)),
                  pl.BlockSpec((EXPERT_BLOCK // VALUE_CHUNK, D, VALUE_CHUNK), lambda i, e: (e, 0, 0)),
                  tok, tok, tok, tok,
                  pl.BlockSpec((TT, D), lambda i, e: (i, 0)),
                  pl.BlockSpec((1, 6, D), lambda i, e: (mod_row(i), 0, 0))],
        out_specs=pl.BlockSpec((TT, D), lambda i, e: (i, 0)),
        scratch_shapes=[pltpu.VMEM((D, TT), F32),
                        pltpu.VMEM((EXPERT_BLOCK, TT), BF16)],
        compiler_params=_cparams("arbitrary", "arbitrary"),
        name="peer_experts",
    )(h2, u, vt, *route, x, mod)


def _vt_kernel(v_ref, o_ref):
    o_ref[0, 0] = v_ref[0].T.astype(BF16)


def _transposed_values(peer_v):
    depth, ne, _ = peer_v.shape
    return pl.pallas_call(
        _vt_kernel,
        out_shape=jax.ShapeDtypeStruct((depth, ne // VALUE_CHUNK, D, VALUE_CHUNK), BF16),
        grid=(depth, ne // VALUE_CHUNK),
        in_specs=[pl.BlockSpec((1, VALUE_CHUNK, D), lambda l, c: (l, c, 0))],
        out_specs=pl.BlockSpec((1, 1, D, VALUE_CHUNK), lambda l, c: (l, c, 0, 0)),
        compiler_params=_cparams("arbitrary", "arbitrary"),
        name="expert_value_layout",
    )(peer_v)


def _final_norm_kernel(x_ref, w_ref, o_ref):
    x = x_ref[...]
    o_ref[...] = x * lax.rsqrt(jnp.mean(x * x, axis=-1, keepdims=True) + EPS) * w_ref[...]


def _final_norm(x, w):
    n = x.shape[0]
    return pl.pallas_call(
        _final_norm_kernel,
        out_shape=jax.ShapeDtypeStruct((n, D), F32),
        grid=(n // TM,),
        in_specs=[pl.BlockSpec((TM, D), lambda i: (i, 0)), _const_spec((1, D))],
        out_specs=pl.BlockSpec((TM, D), lambda i: (i, 0)),
        compiler_params=_cparams("arbitrary"),
        name="final_norm",
    )(x, w)


def _layer_weights(l, w_in, w_af, b_af, w_ab, b_ab, w_out, peer_wq, peer_k1, peer_k2, peer_u):
    o_lr = 4 * GLA_W
    o_conv = o_lr + 2 * LOWRANK
    o_na = o_conv + 3 * CONV_W
    wi = w_in[l]
    wg = wi[:, :o_lr].astype(BF16)
    wl = jnp.tile(wi[:, o_lr:o_conv], (1, 8)).astype(BF16)
    wc = wi[:, o_conv:o_na].astype(BF16)
    wn = wi[:, o_na:].astype(BF16)
    wa = jnp.zeros((2 * LOWRANK, 2 * GLA_W), F32)
    wa = wa.at[:LOWRANK, :GLA_W].set(w_af[l]).at[LOWRANK:, GLA_W:].set(w_ab[l])
    a_hi = wa.astype(BF16)
    a_rest = wa - a_hi.astype(F32)
    a_mid = a_rest.astype(BF16)
    a_lo = (a_rest - a_mid.astype(F32)).astype(BF16)
    zero = jnp.zeros_like(a_hi)
    wab = jnp.concatenate([a_hi, a_mid, a_hi, a_lo, a_hi, a_mid, zero, zero], axis=0)
    bab = jnp.concatenate([b_af[l], b_ab[l]])[None, :]
    wo = w_out[l].astype(BF16)
    wo = (wo[:GLA_W], wo[GLA_W:GLA_W + CONV_W], wo[GLA_W + CONV_W:])
    half = peer_k1.shape[-1]
    kk = jnp.concatenate([jnp.pad(peer_k1[l], ((0, 0), (0, 0), (0, half))),
                          jnp.pad(peer_k2[l], ((0, 0), (0, 0), (half, 0)))], axis=1)
    u = peer_u[l].astype(BF16)
    return (wg, wc, wn, wl, wab, bab), wo, peer_wq[l].astype(BF16), kk, u


def kernel(x_prompt, x_sample, cache_na_k, cache_na_v, state_gla_fwd, state_gla_bwd, c, c_ctx, w_ada, b_ada, norm1_w, norm2_w, w_in, w_af, b_af, w_ab, b_ab, gla_norm_w, conv_w, conv_b, na_rpb, w_out, peer_wq, peer_k1, peer_k2, peer_u, peer_v, final_norm_w):
    bp, sp, _ = x_prompt.shape
    bs, ss, _ = x_sample.shape
    depth = w_ada.shape[0]
    xp = x_prompt.reshape(bp * sp, D)
    xs = x_sample.reshape(bs * ss, D)

    cvec = jnp.zeros((8, D), F32).at[0].set(c_ctx).at[1:1 + bs].set(c)
    mods = _modulation(cvec, w_ada, b_ada)
    kctx = jnp.swapaxes(cache_na_k, 2, 3).reshape(bs, depth, -1, NA_W)
    vctx = jnp.swapaxes(cache_na_v, 2, 3).reshape(bs, depth, -1, NA_W)

    prompt_row = lambda i: 0
    sample_row = lambda tile: (lambda i: 1 + i // (ss // tile))

    vts = _transposed_values(peer_v)
    new_k, new_v, new_sf, new_sb = [], [], [], []
    for l in range(depth):
        inw, wo, wq, kk, u = _layer_weights(l, w_in, w_af, b_af, w_ab, b_ab, w_out,
                                            peer_wq, peer_k1, peer_k2, peer_u)
        vt = vts[l]
        mod = mods[l]
        n1, n2 = norm1_w[l][None, :], norm2_w[l][None, :]
        gnw = gla_norm_w[l][None, :]
        cw, cbias = conv_w[l], conv_b[l][None, :]
        bias = _na_bias_table(na_rpb[l])
        s0 = (_states_to_blockdiag(state_gla_fwd[:, l]), _states_to_blockdiag(state_gla_bwd[:, l]))

        zg, la, zc, zn = _inproj(xp, mod, prompt_row, n1, inw)
        yg, sf, sb = _gla(zg, la, gnw, bp, sp, None)
        yc = _conv(zc, cw, cbias, bp, sp)
        yn = _dense_attn(zn, bp, sp)
        xp, h2, q = _outproj(xp, yg, yc, yn, mod, prompt_row, n2, wo, wq)
        xp = _experts(h2, u, vt, _route(q, kk), xp, mod, prompt_row)
        heads = lambda a: a.reshape(bp, sp, NA_W // HD, HD).transpose(0, 2, 1, 3)
        new_k.append(heads(zn[:, NA_W:2 * NA_W]))
        new_v.append(heads(zn[:, 2 * NA_W:]))
        new_sf.append(_blockdiag_to_states(sf))
        new_sb.append(_blockdiag_to_states(sb))

        zg, la, zc, zn = _inproj(xs, mod, sample_row(TM), n1, inw)
        yg = _gla(zg, la, gnw, bs, ss, s0)
        yc = _conv(zc, cw, cbias, bs, ss)
        yn = _neighbourhood_attn(zn, kctx[:, l:l + 1], vctx[:, l:l + 1], bias, bs, ss)
        xs, h2, q = _outproj(xs, yg, yc, yn, mod, sample_row(TM), n2, wo, wq)
        xs = _experts(h2, u, vt, _route(q, kk), xs, mod, sample_row(TT))

    fw = final_norm_w[None, :]
    y_prompt = _final_norm(xp, fw).reshape(bp, sp, D)
    y_sample = _final_norm(xs, fw).reshape(bs, ss, D)
    return (y_prompt, y_sample, jnp.stack(new_k, axis=1), jnp.stack(new_v, axis=1),
            jnp.stack(new_sf, axis=1), jnp.stack(new_sb, axis=1))
```

```python
import functools

import numpy as np
import jax
import jax.numpy as jnp
from jax import lax
from jax.experimental import pallas as pl
from jax.experimental.pallas import tpu as pltpu

F32 = jnp.float32
BF16 = jnp.bfloat16
HI = lax.Precision.HIGHEST
NT = (((1,), (1,)), ((), ()))
TN = (((0,), (0,)), ((), ()))

LANES = 128
EPS = 1e-6
D = 1024
HD = 64
PAIR = 2 * HD
GLA_W = 384
CONV_W = 256
NA_W = 384
LOWRANK = 16
GLA_CHUNK = 64
GLA_SUB = 16
GLA_OUT_ROWS = 256
GLA_GROUP = 8
GLA_TAU = 16.0
EXP_CLAMP = 80.0
GRID_W = 64
NA_ROWS = 8
NA_COLS = 16
NA_ROWS_PER_STEP = 4
NA_BLOCKS_PER_STEP = 4
PEER_HEADS = 8
NKEYS = 128
TOPK = 16
TM = 512
CUM_ROWS = 256
TT = 512
EXPERT_BLOCK = 2048
EXPERT_SUB = 512
VALUE_CHUNK = 2048
ROUTE_HEADS_PER_STEP = 8
ROUTE_TM = 128
VMEM_LIMIT = 56 * 1024 * 1024


def _cparams(*sem):
    return pltpu.CompilerParams(dimension_semantics=sem, vmem_limit_bytes=VMEM_LIMIT)


def _silu(x):
    return x * jax.nn.sigmoid(x)


def _gelu_tanh(x):
    c0 = float(np.sqrt(2.0 / np.pi))
    z = x * (c0 + (0.044715 * c0) * (x * x))
    hx = 0.5 * x
    return hx + hx * jnp.tanh(z)


def _norm_mod(x, w, scale, shift):
    ms = jnp.mean(x * x, axis=-1, keepdims=True)
    return x * lax.rsqrt(ms + EPS) * w * (1.0 + scale) + shift


def _head_masks():
    lane = lax.broadcasted_iota(jnp.int32, (1, PAIR), 1)
    m0 = (lane < HD).astype(F32)
    return m0, 1.0 - m0


def _mod_kernel(c_ref, w_ref, b_ref, o_ref):
    s = _silu(c_ref[...]).astype(BF16)
    o_ref[0] = jnp.dot(s, w_ref[0].astype(BF16), preferred_element_type=F32) + b_ref[0]


def _modulation(cvec, w_ada, b_ada):
    depth = w_ada.shape[0]
    nb = w_ada.shape[2] // D
    out = pl.pallas_call(
        _mod_kernel,
        out_shape=jax.ShapeDtypeStruct((depth, 8, nb * D), F32),
        grid=(depth, nb),
        in_specs=[pl.BlockSpec((8, D), lambda l, j: (0, 0)),
                  pl.BlockSpec((1, D, D), lambda l, j: (l, 0, j)),
                  pl.BlockSpec((1, 1, D), lambda l, j: (l, 0, j))],
        out_specs=pl.BlockSpec((1, 8, D), lambda l, j: (l, 0, j)),
        compiler_params=_cparams("arbitrary", "arbitrary"),
        name="adaln_modulation",
    )(cvec, w_ada, b_ada.reshape(depth, 1, nb * D))
    return out.reshape(depth, 8, nb, D)


def _inproj_kernel(x_ref, mod_ref, nw_ref, wg_ref, wc_ref, wn_ref, wl_ref, wab_ref, bab_ref,
                   zg_ref, la_ref, zc_ref, zn_ref):
    h = _norm_mod(x_ref[...], nw_ref[...], mod_ref[0, 1:2, :], mod_ref[0, 0:1, :])
    hb = h.astype(BF16)
    zg_ref[...] = jnp.dot(hb, wg_ref[...], preferred_element_type=F32)
    zc_ref[...] = jnp.dot(hb, wc_ref[...], preferred_element_type=F32)
    zn_ref[...] = jnp.dot(hb, wn_ref[...], preferred_element_type=F32)
    lr = jnp.dot(hb, wl_ref[...], preferred_element_type=F32)
    hi = lr.astype(BF16)
    rest = lr - hi.astype(F32)
    mid = rest.astype(BF16)
    lo = (rest - mid.astype(F32)).astype(BF16)
    grp = lax.broadcasted_iota(jnp.int32, (1, lr.shape[1]), 1) >> 5
    pieces = jnp.where((grp == 2) | (grp == 5), mid, jnp.where(grp == 4, lo, hi))
    zz = jnp.dot(pieces, wab_ref[...], preferred_element_type=F32) + bab_ref[...]
    la = (jnp.minimum(zz, 0.0) - jnp.log(1.0 + jnp.exp(-jnp.abs(zz)))) * (1.0 / GLA_TAU)
    r = lax.broadcasted_iota(jnp.int32, (CUM_ROWS, CUM_ROWS), 0)
    s = lax.broadcasted_iota(jnp.int32, (CUM_ROWS, CUM_ROWS), 1)
    shift = GLA_CHUNK.bit_length() - 1
    same = (r >> shift) == (s >> shift)
    tri_f = (same & (s <= r)).astype(BF16)
    tri_b = (same & (s >= r)).astype(BF16)
    hi = la.astype(BF16)
    rest = la - hi.astype(F32)
    mid = rest.astype(BF16)
    lo = (rest - mid.astype(F32)).astype(BF16)
    for blk in range(TM // CUM_ROWS):
        rows = slice(blk * CUM_ROWS, (blk + 1) * CUM_ROWS)
        for tri, cols in ((tri_f, slice(0, GLA_W)), (tri_b, slice(GLA_W, 2 * GLA_W))):
            la_ref[rows, cols] = (jnp.dot(tri, hi[rows, cols], preferred_element_type=F32)
                                  + jnp.dot(tri, mid[rows, cols], preferred_element_type=F32)
                                  + jnp.dot(tri, lo[rows, cols], preferred_element_type=F32))


def _const_spec(shape):
    return pl.BlockSpec(shape, lambda *_: (0,) * len(shape))


def _inproj(x, mod, mod_row, nw, wts):
    n = x.shape[0]
    wg, wc, wn, wl, wab, bab = wts
    row = lambda w: pl.BlockSpec((TM, w), lambda i: (i, 0))
    return pl.pallas_call(
        _inproj_kernel,
        out_shape=(jax.ShapeDtypeStruct((n, 4 * GLA_W), F32), jax.ShapeDtypeStruct((n, 2 * GLA_W), F32),
                   jax.ShapeDtypeStruct((n, 3 * CONV_W), F32), jax.ShapeDtypeStruct((n, 3 * NA_W), F32)),
        grid=(n // TM,),
        in_specs=[row(D), pl.BlockSpec((1, 6, D), lambda i: (mod_row(i), 0, 0)), _const_spec((1, D)),
                  _const_spec(wg.shape), _const_spec(wc.shape), _const_spec(wn.shape), _const_spec(wl.shape),
                  _const_spec(wab.shape), _const_spec(bab.shape)],
        out_specs=(row(4 * GLA_W), row(2 * GLA_W), row(3 * CONV_W), row(3 * NA_W)),
        compiler_params=_cparams("arbitrary"),
        name="in_projection",
    )(x, mod, nw, wg, wc, wn, wl, wab, bab)


def _gla_consts(fwd):
    c, sb = GLA_CHUNK, GLA_SUB
    r = lax.broadcasted_iota(jnp.int32, (2 * c, c), 0) & (c - 1)
    s = lax.broadcasted_iota(jnp.int32, (2 * c, c), 1)
    caus = (s <= r) if fwd else (s >= r)
    rowid = lax.broadcasted_iota(jnp.int32, (c, PAIR), 0)
    seen = [(rowid < (i + 1) * sb) if fwd else (rowid >= i * sb) for i in range(c // sb)]
    m0, m1 = _head_masks()
    qsel = [[((rowid >= i * sb) & (rowid < (i + 1) * sb)).astype(F32) * m for i in range(c // sb)]
            for m in (m0, m1)]
    return caus, seen, qsel


def _gla_operands(qc, kc, vc, cum, fwd, consts):
    _, seen, qsel = consts
    c, sb = GLA_CHUNK, GLA_SUB
    nsb = c // sb
    zero_row = jnp.zeros((1, PAIR), F32)
    if fwd:
        last = cum[c - 1:c]
        starts = [zero_row] + [cum[i * sb - 1:i * sb] for i in range(1, nsb)]
    else:
        last = cum[0:1]
        starts = [cum[(i + 1) * sb:(i + 1) * sb + 1] for i in range(nsb - 1)] + [zero_row]
    bm = jnp.concatenate([jnp.broadcast_to(b, (sb, PAIR)) for b in starts], axis=0)
    qt = qc * jnp.exp(cum - bm)
    kbig = jnp.concatenate(
        [(kc * jnp.exp(jnp.where(seen[i], jnp.minimum(starts[i] - cum, EXP_CLAMP), 0.0))).astype(BF16)
         for i in range(nsb)], axis=1)
    qbig = jnp.concatenate(
        [jnp.concatenate([(qt * qsel[h][i]).astype(BF16) for i in range(nsb)], axis=1) for h in range(2)],
        axis=0)
    return dict(qbig=qbig, kbig=kbig, vb=vc.astype(BF16), qdec=(qc * jnp.exp(cum)).astype(BF16),
                khat=(kc * jnp.exp(last - cum)).astype(BF16), decay=jnp.exp(last))


def _gla_group(chunks, st, consts, masks, bd):
    caus = consts[0]
    m0, m1 = masks
    c = GLA_CHUNK
    atts = [lax.dot_general(ch['qbig'], ch['kbig'], NT, preferred_element_type=F32) for ch in chunks]
    upds = [lax.dot_general(ch['vb'], ch['khat'], TN, preferred_element_type=F32) for ch in chunks]
    atts = [jnp.where(caus, a, 0.0).astype(BF16) for a in atts]
    intras = [jnp.dot(a[:c], ch['vb'], preferred_element_type=F32) * m0
              + jnp.dot(a[c:], ch['vb'], preferred_element_type=F32) * m1 for a, ch in zip(atts, chunks)]
    outs = []
    for ch, upd, intra in zip(chunks, upds, intras):
        outs.append(intra + lax.dot_general(ch['qdec'], st.astype(BF16), NT, preferred_element_type=F32))
        st = st * ch['decay'] + upd * bd
    return outs, st


def _gla_kernel(*refs, seq, state_in):
    if state_in:
        q_ref, k_ref, v_ref, g_ref, laf_ref, lab_ref, nw_ref, s0f_ref, s0b_ref, y_ref, of_scr, ob_scr = refs
    else:
        q_ref, k_ref, v_ref, g_ref, laf_ref, lab_ref, nw_ref, y_ref, sf_ref, sb_ref, of_scr, ob_scr = refs
    c = GLA_CHUNK
    nc = seq // c
    masks = _head_masks()
    m0, m1 = masks
    rr = lax.broadcasted_iota(jnp.int32, (PAIR, PAIR), 0)
    cc = lax.broadcasted_iota(jnp.int32, (PAIR, PAIR), 1)
    bd = ((rr < HD) == (cc < HD)).astype(F32)
    cf = _gla_consts(True)
    cb = _gla_consts(False)
    scale = HD ** -0.5
    nw = nw_ref[...]

    def rows(i):
        return pl.ds(pl.multiple_of(i * c, c), c)

    grp = min(GLA_GROUP, nc)

    def scan_body(j, carry):
        stf, stb = carry
        slf = [rows(j * grp + k) for k in range(grp)]
        slb = [rows(nc - 1 - (j * grp + k)) for k in range(grp)]
        chf = [_gla_operands(q_ref[s, :] * scale, k_ref[s, :], v_ref[s, :], laf_ref[s, :], True, cf)
               for s in slf]
        chb = [_gla_operands(q_ref[s, :] * scale, k_ref[s, :], v_ref[s, :], lab_ref[s, :], False, cb)
               for s in slb]
        of, stf = _gla_group(chf, stf, cf, masks, bd)
        ob, stb = _gla_group(chb, stb, cb, masks, bd)
        for s, o in zip(slf, of):
            of_scr[s, :] = o
        for s, o in zip(slb, ob):
            ob_scr[s, :] = o
        return stf, stb

    def out_body(i, carry):
        sl = pl.ds(pl.multiple_of(i * GLA_OUT_ROWS, GLA_OUT_ROWS), GLA_OUT_ROWS)
        tot = of_scr[sl, :] + ob_scr[sl, :]
        sq = tot * tot
        ms = (jnp.sum(sq * m0, axis=-1, keepdims=True) * m0
              + jnp.sum(sq * m1, axis=-1, keepdims=True) * m1) * (1.0 / HD)
        y_ref[sl, :] = tot * lax.rsqrt(ms + EPS) * nw * _silu(g_ref[sl, :])
        return carry

    zero = jnp.zeros((PAIR, PAIR), F32)
    init = (s0f_ref[0, 0], s0b_ref[0, 0]) if state_in else (zero, zero)
    sf, sb = lax.fori_loop(0, nc // grp, scan_body, init)
    lax.fori_loop(0, seq // GLA_OUT_ROWS, out_body, 0)
    if not state_in:
        sf_ref[0, 0] = sf
        sb_ref[0, 0] = sb


def _gla(zg, la, nw, batch, seq, states):
    n = zg.shape[0]
    npair = GLA_W // PAIR
    col = lambda off: pl.BlockSpec((seq, PAIR), lambda b, p: (b, off + p))
    st_spec = pl.BlockSpec((1, 1, PAIR, PAIR), lambda b, p: (b, p, 0, 0))
    in_specs = [col(0), col(npair), col(2 * npair), col(3 * npair), col(0), col(npair),
                pl.BlockSpec((1, PAIR), lambda b, p: (0, p))]
    args = [zg, zg, zg, zg, la, la, nw]
    y_shape = jax.ShapeDtypeStruct((n, GLA_W), F32)
    y_spec = pl.BlockSpec((seq, PAIR), lambda b, p: (b, p))
    if states is None:
        st_shape = jax.ShapeDtypeStruct((batch, npair, PAIR, PAIR), F32)
        out_shape, out_specs = (y_shape, st_shape, st_shape), (y_spec, st_spec, st_spec)
    else:
        in_specs += [st_spec, st_spec]
        args += list(states)
        out_shape, out_specs = y_shape, y_spec
    return pl.pallas_call(
        functools.partial(_gla_kernel, seq=seq, state_in=states is not None),
        out_shape=out_shape,
        grid=(batch, npair),
        in_specs=in_specs,
        out_specs=out_specs,
        scratch_shapes=[pltpu.VMEM((seq, PAIR), F32), pltpu.VMEM((seq, PAIR), F32)],
        compiler_params=_cparams("arbitrary", "arbitrary"),
        name="gla_bidir",
    )(*args)


def _states_to_blockdiag(s):
    b, h = s.shape[:2]
    st = jnp.swapaxes(s, -1, -2).reshape(b, h // 2, 2, HD, HD)
    eye = jnp.eye(2, dtype=s.dtype)
    return jnp.einsum('bpivk,ij->bpivjk', st, eye).reshape(b, h // 2, PAIR, PAIR)


def _blockdiag_to_states(sbd):
    b, p = sbd.shape[:2]
    s6 = sbd.reshape(b, p, 2, HD, 2, HD)
    diag = jnp.stack([s6[:, :, 0, :, 0, :], s6[:, :, 1, :, 1, :]], axis=2)
    return jnp.swapaxes(diag, -1, -2).reshape(b, 2 * p, HD, HD)


def _conv_kernel(ch_ref, cb_ref, cc_ref, w_ref, b_ref, y_ref, *, seq):
    u = cc_ref[...] * ch_ref[...]
    row = lax.broadcasted_iota(jnp.int32, u.shape, 0)
    prev = jnp.where(row == 0, 0.0, pltpu.roll(u, 1, 0))
    nxt = jnp.where(row == seq - 1, 0.0, pltpu.roll(u, seq - 1, 0))
    y_ref[...] = cb_ref[...] * (w_ref[0:1, :] * prev + w_ref[1:2, :] * u + w_ref[2:3, :] * nxt + b_ref[...])


def _conv(zc, w, b, batch, seq):
    col = lambda j: pl.BlockSpec((seq, CONV_W), lambda i: (i, j))
    return pl.pallas_call(
        functools.partial(_conv_kernel, seq=seq),
        out_shape=jax.ShapeDtypeStruct((zc.shape[0], CONV_W), F32),
        grid=(batch,),
        in_specs=[col(0), col(1), col(2), _const_spec(w.shape), _const_spec(b.shape)],
        out_specs=col(0),
        compiler_params=_cparams("arbitrary"),
        name="gated_conv",
    )(zc, zc, zc, w, b)


def _dense_attn_kernel(q_ref, k_ref, v_ref, o_ref):
    q = q_ref[...] * (HD ** -0.5)
    kb = k_ref[...].astype(BF16)
    vb = v_ref[...].astype(BF16)
    out = jnp.zeros(q.shape, F32)
    for m in _head_masks():
        s = lax.dot_general((q * m).astype(BF16), kb, NT, preferred_element_type=F32)
        p = jnp.exp(s - jnp.max(s, axis=-1, keepdims=True))
        o = jnp.dot(p.astype(BF16), vb, preferred_element_type=F32)
        out = out + o * (m / jnp.sum(p, axis=-1, keepdims=True))
    o_ref[...] = out


def _dense_attn(zn, batch, seq):
    npair = NA_W // PAIR
    col = lambda off: pl.BlockSpec((seq, PAIR), lambda b, p: (b, off + p))
    return pl.pallas_call(
        _dense_attn_kernel,
        out_shape=jax.ShapeDtypeStruct((zn.shape[0], NA_W), F32),
        grid=(batch, npair),
        in_specs=[col(0), col(npair), col(2 * npair)],
        out_specs=col(0),
        compiler_params=_cparams("arbitrary", "arbitrary"),
        name="context_attention",
    )(zn, zn, zn)


def _na_kernel(q_ref, k_ref, v_ref, kc_ref, vc_ref, tp_ref, o_ref, *, rows, span):
    rb = NA_ROWS_PER_STEP
    kc = kc_ref[0, 0].astype(BF16)
    vc = vc_ref[0, 0].astype(BF16)
    m0, m1 = _head_masks()
    shift = GRID_W.bit_length() - 1
    keyrow = lax.broadcasted_iota(jnp.int32, (1, span * GRID_W), 1) >> shift

    def block(bi):
        g = pl.program_id(2) * NA_BLOCKS_PER_STEP + bi
        q0 = bi * rb * GRID_W
        us = jnp.clip(g * rb - NA_ROWS // 2, 0, rows - span)
        win = pl.ds(pl.multiple_of(us * GRID_W, GRID_W), span * GRID_W)
        kw = k_ref[win, :].astype(BF16)
        vw = v_ref[win, :].astype(BF16)
        qs = []
        for rr in range(rb):
            q = q_ref[q0 + rr * GRID_W:q0 + (rr + 1) * GRID_W, :] * (HD ** -0.5)
            qs += [q * m0, q * m1]
        qstack = jnp.concatenate(qs, axis=0).astype(BF16)
        sw = lax.dot_general(qstack, kw, NT, preferred_element_type=F32)
        sc = lax.dot_general(qstack, kc, NT, preferred_element_type=F32)
        parts = []
        for rr in range(rb):
            r = g * rb + rr
            lo = jnp.clip(r - NA_ROWS // 2, 0, rows - NA_ROWS) - us
            inside = (keyrow >= lo) & (keyrow < lo + NA_ROWS)
            blk = sw[rr * PAIR:(rr + 1) * PAIR]
            tiles = [blk[:, jp * PAIR:(jp + 1) * PAIR]
                     + tp_ref[0, jnp.clip(us + 2 * jp - r + NA_ROWS, 0, 2 * NA_ROWS - 1)]
                     for jp in range(span // 2)]
            parts.append(jnp.where(inside, jnp.concatenate(tiles, axis=1), -1e30))
        sw = jnp.concatenate(parts, axis=0)
        mx = jnp.maximum(jnp.max(sw, axis=-1, keepdims=True), jnp.max(sc, axis=-1, keepdims=True))
        pw = jnp.exp(sw - mx)
        pc = jnp.exp(sc - mx)
        den = jnp.sum(pw, axis=-1, keepdims=True) + jnp.sum(pc, axis=-1, keepdims=True)
        o = (jnp.dot(pw.astype(BF16), vw, preferred_element_type=F32)
             + jnp.dot(pc.astype(BF16), vc, preferred_element_type=F32)) / den
        for rr in range(rb):
            o_ref[q0 + rr * GRID_W:q0 + (rr + 1) * GRID_W, :] = (o[rr * PAIR:rr * PAIR + HD] * m0
                                                                 + o[rr * PAIR + HD:(rr + 1) * PAIR] * m1)

    for bi in range(NA_BLOCKS_PER_STEP):
        block(bi)


def _na_bias_table(rpb):
    nh = rpb.shape[0]
    cols = np.arange(GRID_W)
    cs = np.clip(cols - NA_COLS // 2, 0, GRID_W - NA_COLS)
    col_mask = (cols[None, :] >= cs[:, None]) & (cols[None, :] < cs[:, None] + NA_COLS)
    dc = np.clip(cols[None, :] - cols[:, None], -(NA_COLS - 1), NA_COLS - 1) + NA_COLS - 1
    onehot = (dc[:, :, None] == np.arange(2 * NA_COLS - 1)).astype(np.float32)
    toep = jnp.einsum('hab,qkb->haqk', rpb.astype(F32), onehot, precision=HI)
    toep = jnp.where(col_mask[None, None], toep, -1e30)
    ext = jnp.pad(toep, ((0, 0), (1, 1), (0, 0), (0, 0)), constant_values=-1e30)
    two = jnp.concatenate([ext[:, :-1], ext[:, 1:]], axis=-1)
    two = two.reshape(nh // 2, 2, 2 * NA_ROWS, GRID_W, 2 * GRID_W).transpose(0, 2, 1, 3, 4)
    return two.reshape(nh // 2, 2 * NA_ROWS, PAIR, PAIR)


def _neighbourhood_attn(zn, kctx, vctx, bias, batch, seq):
    npair = NA_W // PAIR
    rows = seq // GRID_W
    ctx = kctx.shape[2]
    rb = NA_ROWS_PER_STEP
    nblk = NA_BLOCKS_PER_STEP
    assert rows % (rb * nblk) == 0 and rows >= NA_ROWS
    steps = rows // (rb * nblk)
    span = min(rows, NA_ROWS + rb)
    qcol = pl.BlockSpec((nblk * rb * GRID_W, PAIR), lambda p, b, r: (b * steps + r, p))
    seqcol = lambda off: pl.BlockSpec((seq, PAIR), lambda p, b, r: (b, off + p))
    ctxcol = pl.BlockSpec((1, 1, ctx, PAIR), lambda p, b, r: (b, 0, 0, p))
    bias_spec = pl.BlockSpec((1, 2 * NA_ROWS, PAIR, PAIR), lambda p, b, r: (p, 0, 0, 0))
    return pl.pallas_call(
        functools.partial(_na_kernel, rows=rows, span=span),
        out_shape=jax.ShapeDtypeStruct((zn.shape[0], NA_W), F32),
        grid=(npair, batch, steps),
        in_specs=[qcol, seqcol(npair), seqcol(2 * npair), ctxcol, ctxcol, bias_spec],
        out_specs=qcol,
        compiler_params=_cparams("arbitrary", "arbitrary", "arbitrary"),
        name="neighbourhood_attention",
    )(zn, zn, zn, kctx, vctx, bias)


def _outproj_kernel(x_ref, yg_ref, yc_ref, yn_ref, mod_ref, nw_ref, wg_ref, wc_ref, wn_ref, wq_ref,
                    xo_ref, h_ref, q_ref):
    y = (jnp.dot(yg_ref[...].astype(BF16), wg_ref[...], preferred_element_type=F32)
         + jnp.dot(yc_ref[...].astype(BF16), wc_ref[...], preferred_element_type=F32)
         + jnp.dot(yn_ref[...].astype(BF16), wn_ref[...], preferred_element_type=F32))
    x = x_ref[...] + mod_ref[0, 2:3, :] * y
    xo_ref[...] = x
    hb = _norm_mod(x, nw_ref[...], mod_ref[0, 4:5, :], mod_ref[0, 3:4, :]).astype(BF16)
    h_ref[...] = hb
    q_ref[...] = jnp.dot(hb, wq_ref[...], preferred_element_type=F32)


def _outproj(x, yg, yc, yn, mod, mod_row, nw, wts, wq):
    n = x.shape[0]
    wg, wc, wn = wts
    row = lambda w: pl.BlockSpec((TM, w), lambda i: (i, 0))
    return pl.pallas_call(
        _outproj_kernel,
        out_shape=(jax.ShapeDtypeStruct((n, D), F32), jax.ShapeDtypeStruct((n, D), BF16),
                   jax.ShapeDtypeStruct((n, wq.shape[1]), F32)),
        grid=(n // TM,),
        in_specs=[row(D), row(GLA_W), row(CONV_W), row(NA_W),
                  pl.BlockSpec((1, 6, D), lambda i: (mod_row(i), 0, 0)), _const_spec((1, D)),
                  _const_spec(wg.shape), _const_spec(wc.shape), _const_spec(wn.shape), _const_spec(wq.shape)],
        out_specs=(row(D), row(D), row(wq.shape[1])),
        compiler_params=_cparams("arbitrary"),
        name="out_projection",
    )(x, yg, yc, yn, mod, nw, wg, wc, wn, wq)


def _extract_top(s, dst_ref, want_rank=False):
    rank = jnp.full(s.shape, float(TOPK), F32) if want_rank else None
    for k in range(TOPK):
        m = jnp.max(s, axis=0, keepdims=True)
        dst_ref[k:k + 1, :] = m
        hit = s == m
        if want_rank:
            rank = jnp.where(hit, float(k), rank)
        s = jnp.where(hit, -jnp.inf, s)
    return rank


def _sort16_network():
    def merge(lo, hi, r):
        step = r * 2
        if step < hi - lo:
            yield from merge(lo, hi, step)
            yield from merge(lo + r, hi, step)
            yield from [(i, i + r) for i in range(lo + r, hi - r, step)]
        else:
            yield (lo, lo + r)

    def sort(lo, hi):
        if hi - lo >= 1:
            mid = lo + (hi - lo) // 2
            yield from sort(lo, mid)
            yield from sort(mid + 1, hi)
            yield from merge(lo, hi, 1)

    return tuple(sort(0, TOPK - 1))


_SORT16 = _sort16_network()
_BITONIC16 = tuple((i, i + d) for d in (8, 4, 2, 1) for i in range(TOPK) if not i & d)


def _exchange(v, pairs):
    v = list(v)
    for i, j in pairs:
        a, b = v[i], v[j]
        if b is None:
            continue
        if a is None:
            v[i], v[j] = b, None
        else:
            v[i], v[j] = jnp.maximum(a, b), jnp.minimum(a, b)
    return v


def _top16_sorted(tiles):
    v = _exchange(list(tiles) + [None] * (TOPK - len(tiles)), _SORT16)
    for shift in (4, 2, 1):
        other = [None if t is None else pltpu.roll(t, shift, 0) for t in v]
        merged = []
        for k in range(TOPK):
            a, b = v[k], other[TOPK - 1 - k]
            merged.append(b if a is None else a if b is None else jnp.maximum(a, b))
        v = _exchange(merged, _BITONIC16)
    return v


def _route_kernel(q_ref, kk_ref, r1_ref, e1_ref, c2_ref, e2_ref, v1_scr, v2_scr):
    sub = 8
    for h in range(ROUTE_HEADS_PER_STEP):
        qh = q_ref[:, h * LANES:(h + 1) * LANES]
        st = lax.dot_general(kk_ref[h], qh, NT, precision=HI, preferred_element_type=F32)
        s1 = st[:NKEYS]
        s2 = st[NKEYS:]
        r1_ref[h] = _extract_top(s1, v1_scr, want_rank=True)
        top2 = _top16_sorted([s2[g * sub:(g + 1) * sub] for g in range(NKEYS // sub)])
        for k in range(TOPK):
            v2_scr[k:k + 1, :] = top2[k][0:1]
        v1 = v1_scr[...]
        v2 = v2_scr[...]
        half = TOPK // 2
        cand = ([v1[0:1] + v2[:half], v1[0:1] + v2[half:]] + [v1[a:a + 1] + v2[:half] for a in range(1, half)]
                + [v1[half:] + v2[0:1]])
        top = _top16_sorted(cand)
        z = sum(jnp.exp(t - top[0]) for t in top)[0:1]
        tau = top[TOPK - 1][0:1]
        count = jnp.zeros(s2.shape, F32)
        for a in range(half):
            count = count + jnp.where(v1[a:a + 1] + s2 >= tau, 1.0, 0.0)
        tail = jnp.sum(jnp.where(v1[half:] + v2[0:1] >= tau, 1.0, 0.0), axis=0, keepdims=True)
        count = count + jnp.where(s2 == v2[0:1], tail, 0.0)
        e1_ref[h] = jnp.exp(s1 - v1[0:1])
        c2_ref[h] = count.astype(BF16)
        e2_ref[h] = (jnp.exp(s2 - v2[0:1]) * (1.0 / z)).astype(BF16)


def _route(q, kk):
    n = q.shape[0]
    hps = ROUTE_HEADS_PER_STEP
    tok = pl.BlockSpec((hps, NKEYS, ROUTE_TM), lambda i, g: (g, 0, i))
    rows = jax.ShapeDtypeStruct((PEER_HEADS, NKEYS, n), F32)
    cols = jax.ShapeDtypeStruct((PEER_HEADS, NKEYS, n), BF16)
    top = pltpu.VMEM((TOPK, ROUTE_TM), F32)
    return pl.pallas_call(
        _route_kernel,
        out_shape=(rows, rows, cols, cols),
        grid=(n // ROUTE_TM, PEER_HEADS // hps),
        in_specs=[pl.BlockSpec((ROUTE_TM, hps * LANES), lambda i, g: (i, g)),
                  pl.BlockSpec((hps, 2 * NKEYS, LANES), lambda i, g: (g, 0, 0))],
        out_specs=(tok, tok, tok, tok),
        scratch_shapes=[top, top],
        compiler_params=_cparams("arbitrary", "arbitrary"),
        name="peer_routing",
    )(q, kk)


def _expert_kernel(h_ref, u_ref, vt_ref, r1_ref, e1_ref, c2_ref, e2_ref, x_ref, mod_ref, fw_ref, o_ref, acc, w_scr,
                   *, final_norm):
    eb = pl.program_id(1)

    @pl.when(eb == 0)
    def _():
        acc[...] = jnp.zeros_like(acc)

    hb = h_ref[...]
    pack = 16
    keys_per_sub = EXPERT_SUB // NKEYS

    nsub = EXPERT_BLOCK // EXPERT_SUB

    def pre_act(sub):
        rows = slice(sub * EXPERT_SUB, (sub + 1) * EXPERT_SUB)
        return lax.dot_general(u_ref[0, rows, :], hb, NT, preferred_element_type=F32).astype(BF16)

    def gated(sub, pre):
        act = _gelu_tanh(pre)
        tile = (NKEYS // pack, pack, TT)
        key0 = eb * (EXPERT_BLOCK // NKEYS) + sub * keys_per_sub
        for ii in range(keys_per_sub):
            gate = jnp.zeros(tile, BF16)
            for h in range(PEER_HEADS):
                r1 = jnp.broadcast_to(r1_ref[h, pl.ds(key0 + ii, 1), :], (pack, TT)).astype(BF16)
                e1 = jnp.broadcast_to(e1_ref[h, pl.ds(key0 + ii, 1), :], (pack, TT)).astype(BF16)
                keep = c2_ref[h].reshape(tile) > r1[None]
                gate = gate + jnp.where(keep, e1[None], 0.0) * e2_ref[h].reshape(tile)
            rows = slice(ii * NKEYS, (ii + 1) * NKEYS)
            w_scr[sub * EXPERT_SUB + ii * NKEYS:sub * EXPERT_SUB + (ii + 1) * NKEYS, :] = (
                gate.reshape(NKEYS, TT) * act[rows])

    per_chunk = VALUE_CHUNK // EXPERT_SUB
    pre = pre_act(0)
    out = None
    for sub in range(nsub):
        nxt = pre_act(sub + 1) if sub + 1 < nsub else None
        gated(sub, pre)
        if (sub + 1) % per_chunk == 0:
            ch = sub // per_chunk
            part = jnp.dot(vt_ref[0, ch], w_scr[ch * VALUE_CHUNK:(ch + 1) * VALUE_CHUNK, :],
                           preferred_element_type=F32)
            out = part if out is None else out + part
        pre = nxt
    acc[...] += out

    @pl.when(eb == pl.num_programs(1) - 1)
    def _():
        x = x_ref[...] + mod_ref[0, 5:6, :] * acc[...].T
        if final_norm:
            x = x * lax.rsqrt(jnp.mean(x * x, axis=-1, keepdims=True) + EPS) * fw_ref[...]
        o_ref[...] = x


def _experts(h2, u, vt, layer, route, x, mod, mod_row, final_w, final_norm):
    n = h2.shape[0]
    tok = pl.BlockSpec((PEER_HEADS, NKEYS, TT), lambda i, e: (0, 0, i))
    return pl.pallas_call(
        functools.partial(_expert_kernel, final_norm=final_norm),
        out_shape=jax.ShapeDtypeStruct((n, D), F32),
        grid=(n // TT, u.shape[1] // EXPERT_BLOCK),
        in_specs=[pl.BlockSpec((TT, D), lambda i, e: (i, 0)),
                  pl.BlockSpec((1, EXPERT_BLOCK, D), lambda i, e: (layer, e, 0)),
                  pl.BlockSpec((1, EXPERT_BLOCK // VALUE_CHUNK, D, VALUE_CHUNK), lambda i, e: (layer, e, 0, 0)),
                  tok, tok, tok, tok,
                  pl.BlockSpec((TT, D), lambda i, e: (i, 0)),
                  pl.BlockSpec((1, 6, D), lambda i, e: (mod_row(i), 0, 0)),
                  pl.BlockSpec((1, D), lambda i, e: (0, 0))],
        out_specs=pl.BlockSpec((TT, D), lambda i, e: (i, 0)),
        scratch_shapes=[pltpu.VMEM((D, TT), F32),
                        pltpu.VMEM((EXPERT_BLOCK, TT), BF16)],
        compiler_params=_cparams("arbitrary", "arbitrary"),
        name="peer_experts",
    )(h2, u, vt, *route, x, mod, final_w)


def _vt_kernel(v_ref, o_ref):
    o_ref[0, 0] = v_ref[0].T.astype(BF16)


def _transposed_values(peer_v):
    depth, ne, _ = peer_v.shape
    return pl.pallas_call(
        _vt_kernel,
        out_shape=jax.ShapeDtypeStruct((depth, ne // VALUE_CHUNK, D, VALUE_CHUNK), BF16),
        grid=(depth, ne // VALUE_CHUNK),
        in_specs=[pl.BlockSpec((1, VALUE_CHUNK, D), lambda l, c: (l, c, 0))],
        out_specs=pl.BlockSpec((1, 1, D, VALUE_CHUNK), lambda l, c: (l, c, 0, 0)),
        compiler_params=_cparams("arbitrary", "arbitrary"),
        name="expert_value_layout",
    )(peer_v)


def _layer_weights(l, w_in, w_af, b_af, w_ab, b_ab, w_out, peer_wq, peer_k1, peer_k2):
    o_lr = 4 * GLA_W
    o_conv = o_lr + 2 * LOWRANK
    o_na = o_conv + 3 * CONV_W
    wi = w_in[l]
    wg = wi[:, :o_lr].astype(BF16)
    wl = jnp.tile(wi[:, o_lr:o_conv], (1, 8)).astype(BF16)
    wc = wi[:, o_conv:o_na].astype(BF16)
    wn = wi[:, o_na:].astype(BF16)
    wa = jnp.zeros((2 * LOWRANK, 2 * GLA_W), F32)
    wa = wa.at[:LOWRANK, :GLA_W].set(w_af[l]).at[LOWRANK:, GLA_W:].set(w_ab[l])
    a_hi = wa.astype(BF16)
    a_rest = wa - a_hi.astype(F32)
    a_mid = a_rest.astype(BF16)
    a_lo = (a_rest - a_mid.astype(F32)).astype(BF16)
    zero = jnp.zeros_like(a_hi)
    wab = jnp.concatenate([a_hi, a_mid, a_hi, a_lo, a_hi, a_mid, zero, zero], axis=0)
    bab = jnp.concatenate([b_af[l], b_ab[l]])[None, :]
    wo = w_out[l].astype(BF16)
    wo = (wo[:GLA_W], wo[GLA_W:GLA_W + CONV_W], wo[GLA_W + CONV_W:])
    half = peer_k1.shape[-1]
    kk = jnp.concatenate([jnp.pad(peer_k1[l], ((0, 0), (0, 0), (0, half))),
                          jnp.pad(peer_k2[l], ((0, 0), (0, 0), (half, 0)))], axis=1)
    return (wg, wc, wn, wl, wab, bab), wo, peer_wq[l].astype(BF16), kk


def kernel(x_prompt, x_sample, cache_na_k, cache_na_v, state_gla_fwd, state_gla_bwd, c, c_ctx, w_ada, b_ada, norm1_w, norm2_w, w_in, w_af, b_af, w_ab, b_ab, gla_norm_w, conv_w, conv_b, na_rpb, w_out, peer_wq, peer_k1, peer_k2, peer_u, peer_v, final_norm_w):
    bp, sp, _ = x_prompt.shape
    bs, ss, _ = x_sample.shape
    depth = w_ada.shape[0]
    xp = x_prompt.reshape(bp * sp, D)
    xs = x_sample.reshape(bs * ss, D)

    cvec = jnp.zeros((8, D), F32).at[0].set(c_ctx).at[1:1 + bs].set(c)
    mods = _modulation(cvec, w_ada, b_ada)
    kctx = jnp.swapaxes(cache_na_k, 2, 3).reshape(bs, depth, -1, NA_W)
    vctx = jnp.swapaxes(cache_na_v, 2, 3).reshape(bs, depth, -1, NA_W)

    prompt_row = lambda i: 0
    sample_row = lambda tile: (lambda i: 1 + i // (ss // tile))

    vt = _transposed_values(peer_v)
    u = peer_u.astype(BF16)
    fw = final_norm_w[None, :]
    new_k, new_v, new_sf, new_sb = [], [], [], []
    for l in range(depth):
        last = l == depth - 1
        inw, wo, wq, kk = _layer_weights(l, w_in, w_af, b_af, w_ab, b_ab, w_out, peer_wq, peer_k1, peer_k2)
        mod = mods[l]
        n1, n2 = norm1_w[l][None, :], norm2_w[l][None, :]
        gnw = gla_norm_w[l][None, :]
        cw, cbias = conv_w[l], conv_b[l][None, :]
        bias = _na_bias_table(na_rpb[l])
        s0 = (_states_to_blockdiag(state_gla_fwd[:, l]), _states_to_blockdiag(state_gla_bwd[:, l]))

        zg, la, zc, zn = _inproj(xp, mod, prompt_row, n1, inw)
        yg, sf, sb = _gla(zg, la, gnw, bp, sp, None)
        yc = _conv(zc, cw, cbias, bp, sp)
        yn = _dense_attn(zn, bp, sp)
        xp, h2, q = _outproj(xp, yg, yc, yn, mod, prompt_row, n2, wo, wq)
        xp = _experts(h2, u, vt, l, _route(q, kk), xp, mod, prompt_row, fw, last)
        heads = lambda a: a.reshape(bp, sp, NA_W // HD, HD).transpose(0, 2, 1, 3)
        new_k.append(heads(zn[:, NA_W:2 * NA_W]))
        new_v.append(heads(zn[:, 2 * NA_W:]))
        new_sf.append(_blockdiag_to_states(sf))
        new_sb.append(_blockdiag_to_states(sb))

        zg, la, zc, zn = _inproj(xs, mod, sample_row(TM), n1, inw)
        yg = _gla(zg, la, gnw, bs, ss, s0)
        yc = _conv(zc, cw, cbias, bs, ss)
        yn = _neighbourhood_attn(zn, kctx[:, l:l + 1], vctx[:, l:l + 1], bias, bs, ss)
        xs, h2, q = _outproj(xs, yg, yc, yn, mod, sample_row(TM), n2, wo, wq)
        xs = _experts(h2, u, vt, l, _route(q, kk), xs, mod, sample_row(TT), fw, last)

    y_prompt = xp.reshape(bp, sp, D)
    y_sample = xs.reshape(bs, ss, D)
    return (y_prompt, y_sample, jnp.stack(new_k, axis=1), jnp.stack(new_v, axis=1),
            jnp.stack(new_sf, axis=1), jnp.stack(new_sb, axis=1))
```

```python
import functools

import numpy as np
import jax
import jax.numpy as jnp
from jax import lax
from jax.experimental import pallas as pl
from jax.experimental.pallas import tpu as pltpu

F32 = jnp.float32
BF16 = jnp.bfloat16
HI = lax.Precision.HIGHEST
NT = (((1,), (1,)), ((), ()))
TN = (((0,), (0,)), ((), ()))

LANES = 128
SUBLANES = 8
BF16_ROWS = 16
EPS = 1e-6
D = 1024
HD = 64
PAIR = 2 * HD
GLA_W = 384
CONV_W = 256
NA_W = 384
LOWRANK = 16
GLA_CHUNK = 64
GLA_SUB = 16
GLA_OUT_ROWS = 256
GLA_GROUP = 8
GLA_TAU = 16.0
EXP_CLAMP = 80.0
GRID_W = 64
NA_ROWS = 8
NA_COLS = 16
NA_ROWS_PER_STEP = 4
NA_BLOCKS_PER_STEP = 4
PEER_HEADS = 8
NKEYS = 128
TOPK = 16
TM = 512
CUM_ROWS = 256
TT = 512
EXPERT_BLOCK = 2048
EXPERT_SUB = 512
VALUE_CHUNK = 2048
ROUTE_HEADS_PER_STEP = 8
ROUTE_TM = 128
VMEM_LIMIT = 56 * 1024 * 1024


def _cparams(*sem):
    return pltpu.CompilerParams(dimension_semantics=sem, vmem_limit_bytes=VMEM_LIMIT)


def _silu(x):
    return x * jax.nn.sigmoid(x)


def _gelu_tanh(x):
    c0 = float(np.sqrt(2.0 / np.pi))
    z = x * (c0 + (0.044715 * c0) * (x * x))
    hx = 0.5 * x
    return hx + hx * jnp.tanh(z)


def _norm_mod(x, w, scale, shift):
    ms = jnp.mean(x * x, axis=-1, keepdims=True)
    return x * lax.rsqrt(ms + EPS) * w * (1.0 + scale) + shift


def _head_masks():
    lane = lax.broadcasted_iota(jnp.int32, (1, PAIR), 1)
    m0 = (lane < HD).astype(F32)
    return m0, 1.0 - m0


def _mod_kernel(c_ref, w_ref, b_ref, o_ref):
    s = _silu(c_ref[...]).astype(BF16)
    o_ref[0] = jnp.dot(s, w_ref[0].astype(BF16), preferred_element_type=F32) + b_ref[0]


def _modulation(cvec, w_ada, b_ada):
    depth = w_ada.shape[0]
    nb = w_ada.shape[2] // D
    out = pl.pallas_call(
        _mod_kernel,
        out_shape=jax.ShapeDtypeStruct((depth, 8, nb * D), F32),
        grid=(depth, nb),
        in_specs=[pl.BlockSpec((8, D), lambda l, j: (0, 0)),
                  pl.BlockSpec((1, D, D), lambda l, j: (l, 0, j)),
                  pl.BlockSpec((1, 1, D), lambda l, j: (l, 0, j))],
        out_specs=pl.BlockSpec((1, 8, D), lambda l, j: (l, 0, j)),
        compiler_params=_cparams("arbitrary", "arbitrary"),
        name="adaln_modulation",
    )(cvec, w_ada, b_ada.reshape(depth, 1, nb * D))
    return out.reshape(depth, 8, nb, D)


def _inproj_kernel(x_ref, mod_ref, nw_ref, wg_ref, wc_ref, wn_ref, wl_ref, wab_ref, bab_ref,
                   zg_ref, la_ref, zc_ref, zn_ref):
    h = _norm_mod(x_ref[...], nw_ref[...], mod_ref[0, 1:2, :], mod_ref[0, 0:1, :])
    hb = h.astype(BF16)
    zg_ref[...] = jnp.dot(hb, wg_ref[...], preferred_element_type=F32)
    zc_ref[...] = jnp.dot(hb, wc_ref[...], preferred_element_type=F32)
    zn_ref[...] = jnp.dot(hb, wn_ref[...], preferred_element_type=F32)
    lr = jnp.dot(hb, wl_ref[...], preferred_element_type=F32)
    hi = lr.astype(BF16)
    rest = lr - hi.astype(F32)
    mid = rest.astype(BF16)
    lo = (rest - mid.astype(F32)).astype(BF16)
    grp = lax.broadcasted_iota(jnp.int32, (1, lr.shape[1]), 1) >> ((2 * LOWRANK).bit_length() - 1)
    pieces = jnp.where((grp == 2) | (grp == 5), mid, jnp.where(grp == 4, lo, hi))
    zz = jnp.dot(pieces, wab_ref[...], preferred_element_type=F32) + bab_ref[...]
    la = (jnp.minimum(zz, 0.0) - jnp.log(1.0 + jnp.exp(-jnp.abs(zz)))) * (1.0 / GLA_TAU)
    r = lax.broadcasted_iota(jnp.int32, (CUM_ROWS, CUM_ROWS), 0)
    s = lax.broadcasted_iota(jnp.int32, (CUM_ROWS, CUM_ROWS), 1)
    shift = GLA_CHUNK.bit_length() - 1
    same = (r >> shift) == (s >> shift)
    tri_f = (same & (s <= r)).astype(BF16)
    tri_b = (same & (s >= r)).astype(BF16)
    hi = la.astype(BF16)
    rest = la - hi.astype(F32)
    mid = rest.astype(BF16)
    lo = (rest - mid.astype(F32)).astype(BF16)
    for blk in range(TM // CUM_ROWS):
        rows = slice(blk * CUM_ROWS, (blk + 1) * CUM_ROWS)
        for tri, cols in ((tri_f, slice(0, GLA_W)), (tri_b, slice(GLA_W, 2 * GLA_W))):
            la_ref[rows, cols] = (jnp.dot(tri, hi[rows, cols], preferred_element_type=F32)
                                  + jnp.dot(tri, mid[rows, cols], preferred_element_type=F32)
                                  + jnp.dot(tri, lo[rows, cols], preferred_element_type=F32))


def _const_spec(shape):
    return pl.BlockSpec(shape, lambda *_: (0,) * len(shape))


def _inproj(x, mod, mod_row, nw, wts):
    n = x.shape[0]
    wg, wc, wn, wl, wab, bab = wts
    row = lambda w: pl.BlockSpec((TM, w), lambda i: (i, 0))
    return pl.pallas_call(
        _inproj_kernel,
        out_shape=(jax.ShapeDtypeStruct((n, 4 * GLA_W), F32), jax.ShapeDtypeStruct((n, 2 * GLA_W), F32),
                   jax.ShapeDtypeStruct((n, 3 * CONV_W), F32), jax.ShapeDtypeStruct((n, 3 * NA_W), F32)),
        grid=(n // TM,),
        in_specs=[row(D), pl.BlockSpec((1, 6, D), lambda i: (mod_row(i), 0, 0)), _const_spec((1, D)),
                  _const_spec(wg.shape), _const_spec(wc.shape), _const_spec(wn.shape), _const_spec(wl.shape),
                  _const_spec(wab.shape), _const_spec(bab.shape)],
        out_specs=(row(4 * GLA_W), row(2 * GLA_W), row(3 * CONV_W), row(3 * NA_W)),
        compiler_params=_cparams("arbitrary"),
        name="in_projection",
    )(x, mod, nw, wg, wc, wn, wl, wab, bab)


def _gla_consts(fwd):
    c, sb = GLA_CHUNK, GLA_SUB
    r = lax.broadcasted_iota(jnp.int32, (2 * c, c), 0) & (c - 1)
    s = lax.broadcasted_iota(jnp.int32, (2 * c, c), 1)
    caus = (s <= r) if fwd else (s >= r)
    rowid = lax.broadcasted_iota(jnp.int32, (c, PAIR), 0)
    seen = [(rowid < (i + 1) * sb) if fwd else (rowid >= i * sb) for i in range(c // sb)]
    m0, m1 = _head_masks()
    qsel = [[((rowid >= i * sb) & (rowid < (i + 1) * sb)).astype(F32) * m for i in range(c // sb)]
            for m in (m0, m1)]
    return caus, seen, qsel


def _gla_operands(qc, kc, vc, cum, fwd, consts):
    _, seen, qsel = consts
    c, sb = GLA_CHUNK, GLA_SUB
    nsb = c // sb
    zero_row = jnp.zeros((1, PAIR), F32)
    if fwd:
        last = cum[c - 1:c]
        starts = [zero_row] + [cum[i * sb - 1:i * sb] for i in range(1, nsb)]
    else:
        last = cum[0:1]
        starts = [cum[(i + 1) * sb:(i + 1) * sb + 1] for i in range(nsb - 1)] + [zero_row]
    bm = jnp.concatenate([jnp.broadcast_to(b, (sb, PAIR)) for b in starts], axis=0)
    qt = qc * jnp.exp(cum - bm)
    kbig = jnp.concatenate(
        [(kc * jnp.exp(jnp.where(seen[i], jnp.minimum(starts[i] - cum, EXP_CLAMP), 0.0))).astype(BF16)
         for i in range(nsb)], axis=1)
    qbig = jnp.concatenate(
        [jnp.concatenate([(qt * qsel[h][i]).astype(BF16) for i in range(nsb)], axis=1) for h in range(2)],
        axis=0)
    return dict(qbig=qbig, kbig=kbig, vb=vc.astype(BF16), qdec=(qc * jnp.exp(cum)).astype(BF16),
                khat=(kc * jnp.exp(last - cum)).astype(BF16), decay=jnp.exp(last))


def _gla_group(chunks, st, consts, masks, bd):
    caus = consts[0]
    m0, m1 = masks
    c = GLA_CHUNK
    atts = [lax.dot_general(ch['qbig'], ch['kbig'], NT, preferred_element_type=F32) for ch in chunks]
    upds = [lax.dot_general(ch['vb'], ch['khat'], TN, preferred_element_type=F32) for ch in chunks]
    atts = [jnp.where(caus, a, 0.0).astype(BF16) for a in atts]
    intras = [jnp.dot(a[:c], ch['vb'], preferred_element_type=F32) * m0
              + jnp.dot(a[c:], ch['vb'], preferred_element_type=F32) * m1 for a, ch in zip(atts, chunks)]
    outs = []
    for ch, upd, intra in zip(chunks, upds, intras):
        outs.append(intra + lax.dot_general(ch['qdec'], st.astype(BF16), NT, preferred_element_type=F32))
        st = st * ch['decay'] + upd * bd
    return outs, st


def _gla_kernel(*refs, seq, state_in):
    if state_in:
        q_ref, k_ref, v_ref, g_ref, laf_ref, lab_ref, nw_ref, s0f_ref, s0b_ref, y_ref, of_scr, ob_scr = refs
    else:
        q_ref, k_ref, v_ref, g_ref, laf_ref, lab_ref, nw_ref, y_ref, sf_ref, sb_ref, of_scr, ob_scr = refs
    c = GLA_CHUNK
    nc = seq // c
    masks = _head_masks()
    m0, m1 = masks
    rr = lax.broadcasted_iota(jnp.int32, (PAIR, PAIR), 0)
    cc = lax.broadcasted_iota(jnp.int32, (PAIR, PAIR), 1)
    bd = ((rr < HD) == (cc < HD)).astype(F32)
    cf = _gla_consts(True)
    cb = _gla_consts(False)
    scale = HD ** -0.5
    nw = nw_ref[...]

    def rows(i):
        return pl.ds(pl.multiple_of(i * c, c), c)

    grp = min(GLA_GROUP, nc)

    def scan_body(j, carry):
        stf, stb = carry
        slf = [rows(j * grp + k) for k in range(grp)]
        slb = [rows(nc - 1 - (j * grp + k)) for k in range(grp)]
        chf = [_gla_operands(q_ref[s, :] * scale, k_ref[s, :], v_ref[s, :], laf_ref[s, :], True, cf)
               for s in slf]
        chb = [_gla_operands(q_ref[s, :] * scale, k_ref[s, :], v_ref[s, :], lab_ref[s, :], False, cb)
               for s in slb]
        of, stf = _gla_group(chf, stf, cf, masks, bd)
        ob, stb = _gla_group(chb, stb, cb, masks, bd)
        for s, o in zip(slf, of):
            of_scr[s, :] = o
        for s, o in zip(slb, ob):
            ob_scr[s, :] = o
        return stf, stb

    def out_body(i, carry):
        sl = pl.ds(pl.multiple_of(i * GLA_OUT_ROWS, GLA_OUT_ROWS), GLA_OUT_ROWS)
        tot = of_scr[sl, :] + ob_scr[sl, :]
        sq = tot * tot
        ms = (jnp.sum(sq * m0, axis=-1, keepdims=True) * m0
              + jnp.sum(sq * m1, axis=-1, keepdims=True) * m1) * (1.0 / HD)
        y_ref[sl, :] = tot * lax.rsqrt(ms + EPS) * nw * _silu(g_ref[sl, :])
        return carry

    zero = jnp.zeros((PAIR, PAIR), F32)
    init = (s0f_ref[0, 0], s0b_ref[0, 0]) if state_in else (zero, zero)
    sf, sb = lax.fori_loop(0, nc // grp, scan_body, init)
    lax.fori_loop(0, seq // GLA_OUT_ROWS, out_body, 0)
    if not state_in:
        sf_ref[0, 0] = sf
        sb_ref[0, 0] = sb


def _gla(zg, la, nw, batch, seq, states):
    n = zg.shape[0]
    npair = GLA_W // PAIR
    col = lambda off: pl.BlockSpec((seq, PAIR), lambda b, p: (b, off + p))
    st_spec = pl.BlockSpec((1, 1, PAIR, PAIR), lambda b, p: (b, p, 0, 0))
    in_specs = [col(0), col(npair), col(2 * npair), col(3 * npair), col(0), col(npair),
                pl.BlockSpec((1, PAIR), lambda b, p: (0, p))]
    args = [zg, zg, zg, zg, la, la, nw]
    y_shape = jax.ShapeDtypeStruct((n, GLA_W), F32)
    y_spec = pl.BlockSpec((seq, PAIR), lambda b, p: (b, p))
    if states is None:
        st_shape = jax.ShapeDtypeStruct((batch, npair, PAIR, PAIR), F32)
        out_shape, out_specs = (y_shape, st_shape, st_shape), (y_spec, st_spec, st_spec)
    else:
        in_specs += [st_spec, st_spec]
        args += list(states)
        out_shape, out_specs = y_shape, y_spec
    return pl.pallas_call(
        functools.partial(_gla_kernel, seq=seq, state_in=states is not None),
        out_shape=out_shape,
        grid=(batch, npair),
        in_specs=in_specs,
        out_specs=out_specs,
        scratch_shapes=[pltpu.VMEM((seq, PAIR), F32), pltpu.VMEM((seq, PAIR), F32)],
        compiler_params=_cparams("arbitrary", "arbitrary"),
        name="gla_bidir",
    )(*args)


def _states_to_blockdiag(s):
    b, h = s.shape[:2]
    st = jnp.swapaxes(s, -1, -2).reshape(b, h // 2, 2, HD, HD)
    eye = jnp.eye(2, dtype=s.dtype)
    return jnp.einsum('bpivk,ij->bpivjk', st, eye).reshape(b, h // 2, PAIR, PAIR)


def _blockdiag_to_states(sbd):
    b, p = sbd.shape[:2]
    s6 = sbd.reshape(b, p, 2, HD, 2, HD)
    diag = jnp.stack([s6[:, :, 0, :, 0, :], s6[:, :, 1, :, 1, :]], axis=2)
    return jnp.swapaxes(diag, -1, -2).reshape(b, 2 * p, HD, HD)


def _conv_kernel(ch_ref, cb_ref, cc_ref, w_ref, b_ref, y_ref, *, seq):
    u = cc_ref[...] * ch_ref[...]
    row = lax.broadcasted_iota(jnp.int32, u.shape, 0)
    prev = jnp.where(row == 0, 0.0, pltpu.roll(u, 1, 0))
    nxt = jnp.where(row == seq - 1, 0.0, pltpu.roll(u, seq - 1, 0))
    y_ref[...] = cb_ref[...] * (w_ref[0:1, :] * prev + w_ref[1:2, :] * u + w_ref[2:3, :] * nxt + b_ref[...])


def _conv(zc, w, b, batch, seq):
    col = lambda j: pl.BlockSpec((seq, CONV_W), lambda i: (i, j))
    return pl.pallas_call(
        functools.partial(_conv_kernel, seq=seq),
        out_shape=jax.ShapeDtypeStruct((zc.shape[0], CONV_W), F32),
        grid=(batch,),
        in_specs=[col(0), col(1), col(2), _const_spec(w.shape), _const_spec(b.shape)],
        out_specs=col(0),
        compiler_params=_cparams("arbitrary"),
        name="gated_conv",
    )(zc, zc, zc, w, b)


def _dense_attn_kernel(q_ref, k_ref, v_ref, o_ref):
    q = q_ref[...] * (HD ** -0.5)
    kb = k_ref[...].astype(BF16)
    vb = v_ref[...].astype(BF16)
    out = jnp.zeros(q.shape, F32)
    for m in _head_masks():
        s = lax.dot_general((q * m).astype(BF16), kb, NT, preferred_element_type=F32)
        p = jnp.exp(s - jnp.max(s, axis=-1, keepdims=True))
        o = jnp.dot(p.astype(BF16), vb, preferred_element_type=F32)
        out = out + o * (m / jnp.sum(p, axis=-1, keepdims=True))
    o_ref[...] = out


def _dense_attn(zn, batch, seq):
    npair = NA_W // PAIR
    col = lambda off: pl.BlockSpec((seq, PAIR), lambda b, p: (b, off + p))
    return pl.pallas_call(
        _dense_attn_kernel,
        out_shape=jax.ShapeDtypeStruct((zn.shape[0], NA_W), F32),
        grid=(batch, npair),
        in_specs=[col(0), col(npair), col(2 * npair)],
        out_specs=col(0),
        compiler_params=_cparams("arbitrary", "arbitrary"),
        name="context_attention",
    )(zn, zn, zn)


def _na_kernel(q_ref, k_ref, v_ref, kc_ref, vc_ref, tp_ref, o_ref, *, rows, span):
    rb = NA_ROWS_PER_STEP
    kc = kc_ref[0, 0].astype(BF16)
    vc = vc_ref[0, 0].astype(BF16)
    m0, m1 = _head_masks()
    shift = GRID_W.bit_length() - 1
    keyrow = lax.broadcasted_iota(jnp.int32, (1, span * GRID_W), 1) >> shift

    def block(bi):
        g = pl.program_id(2) * NA_BLOCKS_PER_STEP + bi
        q0 = bi * rb * GRID_W
        us = jnp.clip(g * rb - NA_ROWS // 2, 0, rows - span)
        win = pl.ds(pl.multiple_of(us * GRID_W, GRID_W), span * GRID_W)
        kw = k_ref[win, :].astype(BF16)
        vw = v_ref[win, :].astype(BF16)
        qs = []
        for rr in range(rb):
            q = q_ref[q0 + rr * GRID_W:q0 + (rr + 1) * GRID_W, :] * (HD ** -0.5)
            qs += [q * m0, q * m1]
        qstack = jnp.concatenate(qs, axis=0).astype(BF16)
        sw = lax.dot_general(qstack, kw, NT, preferred_element_type=F32)
        sc = lax.dot_general(qstack, kc, NT, preferred_element_type=F32)
        parts = []
        for rr in range(rb):
            r = g * rb + rr
            lo = jnp.clip(r - NA_ROWS // 2, 0, rows - NA_ROWS) - us
            inside = (keyrow >= lo) & (keyrow < lo + NA_ROWS)
            blk = sw[rr * PAIR:(rr + 1) * PAIR]
            tiles = [blk[:, jp * PAIR:(jp + 1) * PAIR]
                     + tp_ref[0, jnp.clip(us + 2 * jp - r + NA_ROWS, 0, 2 * NA_ROWS - 1)]
                     for jp in range(span // 2)]
            parts.append(jnp.where(inside, jnp.concatenate(tiles, axis=1), -1e30))
        sw = jnp.concatenate(parts, axis=0)
        mx = jnp.maximum(jnp.max(sw, axis=-1, keepdims=True), jnp.max(sc, axis=-1, keepdims=True))
        pw = jnp.exp(sw - mx)
        pc = jnp.exp(sc - mx)
        den = jnp.sum(pw, axis=-1, keepdims=True) + jnp.sum(pc, axis=-1, keepdims=True)
        o = (jnp.dot(pw.astype(BF16), vw, preferred_element_type=F32)
             + jnp.dot(pc.astype(BF16), vc, preferred_element_type=F32)) / den
        for rr in range(rb):
            o_ref[q0 + rr * GRID_W:q0 + (rr + 1) * GRID_W, :] = (o[rr * PAIR:rr * PAIR + HD] * m0
                                                                 + o[rr * PAIR + HD:(rr + 1) * PAIR] * m1)

    for bi in range(NA_BLOCKS_PER_STEP):
        block(bi)


def _na_bias_table(rpb):
    nh = rpb.shape[0]
    cols = np.arange(GRID_W)
    cs = np.clip(cols - NA_COLS // 2, 0, GRID_W - NA_COLS)
    col_mask = (cols[None, :] >= cs[:, None]) & (cols[None, :] < cs[:, None] + NA_COLS)
    dc = np.clip(cols[None, :] - cols[:, None], -(NA_COLS - 1), NA_COLS - 1) + NA_COLS - 1
    onehot = (dc[:, :, None] == np.arange(2 * NA_COLS - 1)).astype(np.float32)
    toep = jnp.einsum('hab,qkb->haqk', rpb.astype(F32), onehot, precision=HI)
    toep = jnp.where(col_mask[None, None], toep, -1e30)
    ext = jnp.pad(toep, ((0, 0), (1, 1), (0, 0), (0, 0)), constant_values=-1e30)
    two = jnp.concatenate([ext[:, :-1], ext[:, 1:]], axis=-1)
    two = two.reshape(nh // 2, 2, 2 * NA_ROWS, GRID_W, 2 * GRID_W).transpose(0, 2, 1, 3, 4)
    return two.reshape(nh // 2, 2 * NA_ROWS, PAIR, PAIR)


def _neighbourhood_attn(zn, kctx, vctx, bias, batch, seq):
    npair = NA_W // PAIR
    rows = seq // GRID_W
    ctx = kctx.shape[2]
    rb = NA_ROWS_PER_STEP
    nblk = NA_BLOCKS_PER_STEP
    assert rows % (rb * nblk) == 0 and rows >= NA_ROWS
    steps = rows // (rb * nblk)
    span = min(rows, NA_ROWS + rb)
    qcol = pl.BlockSpec((nblk * rb * GRID_W, PAIR), lambda p, b, r: (b * steps + r, p))
    seqcol = lambda off: pl.BlockSpec((seq, PAIR), lambda p, b, r: (b, off + p))
    ctxcol = pl.BlockSpec((1, 1, ctx, PAIR), lambda p, b, r: (b, 0, 0, p))
    bias_spec = pl.BlockSpec((1, 2 * NA_ROWS, PAIR, PAIR), lambda p, b, r: (p, 0, 0, 0))
    return pl.pallas_call(
        functools.partial(_na_kernel, rows=rows, span=span),
        out_shape=jax.ShapeDtypeStruct((zn.shape[0], NA_W), F32),
        grid=(npair, batch, steps),
        in_specs=[qcol, seqcol(npair), seqcol(2 * npair), ctxcol, ctxcol, bias_spec],
        out_specs=qcol,
        compiler_params=_cparams("arbitrary", "arbitrary", "arbitrary"),
        name="neighbourhood_attention",
    )(zn, zn, zn, kctx, vctx, bias)


def _outproj_kernel(x_ref, yg_ref, yc_ref, yn_ref, mod_ref, nw_ref, wg_ref, wc_ref, wn_ref, wq_ref,
                    xo_ref, h_ref, q_ref):
    y = (jnp.dot(yg_ref[...].astype(BF16), wg_ref[...], preferred_element_type=F32)
         + jnp.dot(yc_ref[...].astype(BF16), wc_ref[...], preferred_element_type=F32)
         + jnp.dot(yn_ref[...].astype(BF16), wn_ref[...], preferred_element_type=F32))
    x = x_ref[...] + mod_ref[0, 2:3, :] * y
    xo_ref[...] = x
    hb = _norm_mod(x, nw_ref[...], mod_ref[0, 4:5, :], mod_ref[0, 3:4, :]).astype(BF16)
    h_ref[...] = hb
    q_ref[...] = jnp.dot(hb, wq_ref[...], preferred_element_type=F32)


def _outproj(x, yg, yc, yn, mod, mod_row, nw, wts, wq):
    n = x.shape[0]
    wg, wc, wn = wts
    row = lambda w: pl.BlockSpec((TM, w), lambda i: (i, 0))
    return pl.pallas_call(
        _outproj_kernel,
        out_shape=(jax.ShapeDtypeStruct((n, D), F32), jax.ShapeDtypeStruct((n, D), BF16),
                   jax.ShapeDtypeStruct((n, wq.shape[1]), F32)),
        grid=(n // TM,),
        in_specs=[row(D), row(GLA_W), row(CONV_W), row(NA_W),
                  pl.BlockSpec((1, 6, D), lambda i: (mod_row(i), 0, 0)), _const_spec((1, D)),
                  _const_spec(wg.shape), _const_spec(wc.shape), _const_spec(wn.shape), _const_spec(wq.shape)],
        out_specs=(row(D), row(D), row(wq.shape[1])),
        compiler_params=_cparams("arbitrary"),
        name="out_projection",
    )(x, yg, yc, yn, mod, nw, wg, wc, wn, wq)


def _rank_among(x, top):
    def pick(bits, lo, width):
        if not bits:
            return top[lo + width // 2 - 1]
        return jnp.where(bits[0], pick(bits[1:], lo + width // 2, width // 2),
                         pick(bits[1:], lo, width // 2))

    bits = []
    width = TOPK
    while width > 1:
        bits.append(pick(bits, 0, TOPK) > x)
        width //= 2
    rank = sum(jnp.where(b, float(TOPK >> (i + 1)), 0.0) for i, b in enumerate(bits))
    return jnp.where(top[TOPK - 1] > x, float(TOPK), rank)


def _sort16_network():
    def merge(lo, hi, r):
        step = r * 2
        if step < hi - lo:
            yield from merge(lo, hi, step)
            yield from merge(lo + r, hi, step)
            yield from [(i, i + r) for i in range(lo + r, hi - r, step)]
        else:
            yield (lo, lo + r)

    def sort(lo, hi):
        if hi - lo >= 1:
            mid = lo + (hi - lo) // 2
            yield from sort(lo, mid)
            yield from sort(mid + 1, hi)
            yield from merge(lo, hi, 1)

    return tuple(sort(0, TOPK - 1))


_SORT16 = _sort16_network()
_BITONIC16 = tuple((i, i + d) for d in (8, 4, 2, 1) for i in range(TOPK) if not i & d)


def _exchange(v, pairs):
    v = list(v)
    for i, j in pairs:
        a, b = v[i], v[j]
        if b is None:
            continue
        if a is None:
            v[i], v[j] = b, None
        else:
            v[i], v[j] = jnp.maximum(a, b), jnp.minimum(a, b)
    return v


def _top16_sorted(tiles):
    v = _exchange(list(tiles) + [None] * (TOPK - len(tiles)), _SORT16)
    for shift in (4, 2, 1):
        other = [None if t is None else pltpu.roll(t, shift, 0) for t in v]
        merged = []
        for k in range(TOPK):
            a, b = v[k], other[TOPK - 1 - k]
            merged.append(b if a is None else a if b is None else jnp.maximum(a, b))
        v = _exchange(merged, _BITONIC16)
    return v


def _route_kernel(q_ref, kk_ref, r1_ref, e1_ref, c2_ref, e2_ref, v1_scr, v2_scr):
    sub = SUBLANES
    for h in range(ROUTE_HEADS_PER_STEP):
        qh = q_ref[:, h * LANES:(h + 1) * LANES]
        st = lax.dot_general(kk_ref[h], qh, NT, precision=HI, preferred_element_type=F32)
        s1 = st[:NKEYS]
        s2 = st[NKEYS:]
        tiles1 = [s1[g * sub:(g + 1) * sub] for g in range(NKEYS // sub)]
        top1 = _top16_sorted(tiles1)
        top2 = _top16_sorted([s2[g * sub:(g + 1) * sub] for g in range(NKEYS // sub)])
        for k in range(TOPK):
            v1_scr[k:k + 1, :] = top1[k][0:1]
            v2_scr[k:k + 1, :] = top2[k][0:1]
        r1_ref[h] = jnp.concatenate([_rank_among(t, top1) for t in tiles1], axis=0)
        v1 = v1_scr[...]
        v2 = v2_scr[...]
        half = TOPK // 2
        cand = ([v1[0:1] + v2[:half], v1[0:1] + v2[half:]] + [v1[a:a + 1] + v2[:half] for a in range(1, half)]
                + [v1[half:] + v2[0:1]])
        top = _top16_sorted(cand)
        z = sum(jnp.exp(t - top[0]) for t in top)[0:1]
        tau = top[TOPK - 1][0:1]
        count = jnp.zeros(s2.shape, F32)
        for a in range(half):
            count = count + jnp.where(v1[a:a + 1] + s2 >= tau, 1.0, 0.0)
        tail = jnp.sum(jnp.where(v1[half:] + v2[0:1] >= tau, 1.0, 0.0), axis=0, keepdims=True)
        count = count + jnp.where(s2 == v2[0:1], tail, 0.0)
        e1_ref[h] = jnp.exp(s1 - v1[0:1])
        c2_ref[h] = count.astype(BF16)
        e2_ref[h] = (jnp.exp(s2 - v2[0:1]) * (1.0 / z)).astype(BF16)


def _route(q, kk):
    n = q.shape[0]
    hps = ROUTE_HEADS_PER_STEP
    tok = pl.BlockSpec((hps, NKEYS, ROUTE_TM), lambda i, g: (g, 0, i))
    rows = jax.ShapeDtypeStruct((PEER_HEADS, NKEYS, n), F32)
    cols = jax.ShapeDtypeStruct((PEER_HEADS, NKEYS, n), BF16)
    top = pltpu.VMEM((TOPK, ROUTE_TM), F32)
    return pl.pallas_call(
        _route_kernel,
        out_shape=(rows, rows, cols, cols),
        grid=(n // ROUTE_TM, PEER_HEADS // hps),
        in_specs=[pl.BlockSpec((ROUTE_TM, hps * LANES), lambda i, g: (i, g)),
                  pl.BlockSpec((hps, 2 * NKEYS, LANES), lambda i, g: (g, 0, 0))],
        out_specs=(tok, tok, tok, tok),
        scratch_shapes=[top, top],
        compiler_params=_cparams("arbitrary", "arbitrary"),
        name="peer_routing",
    )(q, kk)


def _expert_kernel(h_ref, u_ref, vt_ref, r1_ref, e1_ref, c2_ref, e2_ref, x_ref, mod_ref, fw_ref, o_ref, acc, w_scr,
                   *, final_norm):
    eb = pl.program_id(1)

    @pl.when(eb == 0)
    def _():
        acc[...] = jnp.zeros_like(acc)

    hb = h_ref[...]
    pack = BF16_ROWS
    keys_per_sub = EXPERT_SUB // NKEYS
    nsub = EXPERT_BLOCK // EXPERT_SUB

    def pre_act(sub):
        rows = slice(sub * EXPERT_SUB, (sub + 1) * EXPERT_SUB)
        return lax.dot_general(u_ref[0, rows, :], hb, NT, preferred_element_type=F32).astype(BF16)

    def gated(sub, pre):
        act = _gelu_tanh(pre)
        tile = (NKEYS // pack, pack, TT)
        key0 = eb * (EXPERT_BLOCK // NKEYS) + sub * keys_per_sub
        for ii in range(keys_per_sub):
            gate = jnp.zeros(tile, BF16)
            for h in range(PEER_HEADS):
                r1 = jnp.broadcast_to(r1_ref[h, pl.ds(key0 + ii, 1), :], (pack, TT)).astype(BF16)
                e1 = jnp.broadcast_to(e1_ref[h, pl.ds(key0 + ii, 1), :], (pack, TT)).astype(BF16)
                keep = c2_ref[h].reshape(tile) > r1[None]
                gate = gate + jnp.where(keep, e1[None], 0.0) * e2_ref[h].reshape(tile)
            rows = slice(ii * NKEYS, (ii + 1) * NKEYS)
            w_scr[sub * EXPERT_SUB + ii * NKEYS:sub * EXPERT_SUB + (ii + 1) * NKEYS, :] = (
                gate.reshape(NKEYS, TT) * act[rows])

    per_chunk = VALUE_CHUNK // EXPERT_SUB
    pre = pre_act(0)
    out = None
    for sub in range(nsub):
        nxt = pre_act(sub + 1) if sub + 1 < nsub else None
        gated(sub, pre)
        if (sub + 1) % per_chunk == 0:
            ch = sub // per_chunk
            part = jnp.dot(vt_ref[0, ch], w_scr[ch * VALUE_CHUNK:(ch + 1) * VALUE_CHUNK, :],
                           preferred_element_type=F32)
            out = part if out is None else out + part
        pre = nxt
    acc[...] += out

    @pl.when(eb == pl.num_programs(1) - 1)
    def _():
        x = x_ref[...] + mod_ref[0, 5:6, :] * acc[...].T
        if final_norm:
            x = x * lax.rsqrt(jnp.mean(x * x, axis=-1, keepdims=True) + EPS) * fw_ref[...]
        o_ref[...] = x


def _experts(h2, u, vt, layer, route, x, mod, mod_row, final_w, final_norm):
    n = h2.shape[0]
    tok = pl.BlockSpec((PEER_HEADS, NKEYS, TT), lambda i, e: (0, 0, i))
    return pl.pallas_call(
        functools.partial(_expert_kernel, final_norm=final_norm),
        out_shape=jax.ShapeDtypeStruct((n, D), F32),
        grid=(n // TT, u.shape[1] // EXPERT_BLOCK),
        in_specs=[pl.BlockSpec((TT, D), lambda i, e: (i, 0)),
                  pl.BlockSpec((1, EXPERT_BLOCK, D), lambda i, e: (layer, e, 0)),
                  pl.BlockSpec((1, EXPERT_BLOCK // VALUE_CHUNK, D, VALUE_CHUNK), lambda i, e: (layer, e, 0, 0)),
                  tok, tok, tok, tok,
                  pl.BlockSpec((TT, D), lambda i, e: (i, 0)),
                  pl.BlockSpec((1, 6, D), lambda i, e: (mod_row(i), 0, 0)),
                  pl.BlockSpec((1, D), lambda i, e: (0, 0))],
        out_specs=pl.BlockSpec((TT, D), lambda i, e: (i, 0)),
        scratch_shapes=[pltpu.VMEM((D, TT), F32),
                        pltpu.VMEM((EXPERT_BLOCK, TT), BF16)],
        compiler_params=_cparams("arbitrary", "arbitrary"),
        name="peer_experts",
    )(h2, u, vt, *route, x, mod, final_w)


def _vt_kernel(v_ref, o_ref):
    o_ref[0, 0] = v_ref[0].T.astype(BF16)


def _transposed_values(peer_v):
    depth, ne, _ = peer_v.shape
    return pl.pallas_call(
        _vt_kernel,
        out_shape=jax.ShapeDtypeStruct((depth, ne // VALUE_CHUNK, D, VALUE_CHUNK), BF16),
        grid=(depth, ne // VALUE_CHUNK),
        in_specs=[pl.BlockSpec((1, VALUE_CHUNK, D), lambda l, c: (l, c, 0))],
        out_specs=pl.BlockSpec((1, 1, D, VALUE_CHUNK), lambda l, c: (l, c, 0, 0)),
        compiler_params=_cparams("arbitrary", "arbitrary"),
        name="expert_value_layout",
    )(peer_v)


def _layer_weights(l, w_in, w_af, b_af, w_ab, b_ab, w_out, peer_wq, peer_k1, peer_k2):
    o_lr = 4 * GLA_W
    o_conv = o_lr + 2 * LOWRANK
    o_na = o_conv + 3 * CONV_W
    wi = w_in[l]
    wg = wi[:, :o_lr].astype(BF16)
    wl = jnp.tile(wi[:, o_lr:o_conv], (1, 8)).astype(BF16)
    wc = wi[:, o_conv:o_na].astype(BF16)
    wn = wi[:, o_na:].astype(BF16)
    wa = jnp.zeros((2 * LOWRANK, 2 * GLA_W), F32)
    wa = wa.at[:LOWRANK, :GLA_W].set(w_af[l]).at[LOWRANK:, GLA_W:].set(w_ab[l])
    a_hi = wa.astype(BF16)
    a_rest = wa - a_hi.astype(F32)
    a_mid = a_rest.astype(BF16)
    a_lo = (a_rest - a_mid.astype(F32)).astype(BF16)
    zero = jnp.zeros_like(a_hi)
    wab = jnp.concatenate([a_hi, a_mid, a_hi, a_lo, a_hi, a_mid, zero, zero], axis=0)
    bab = jnp.concatenate([b_af[l], b_ab[l]])[None, :]
    wo = w_out[l].astype(BF16)
    wo = (wo[:GLA_W], wo[GLA_W:GLA_W + CONV_W], wo[GLA_W + CONV_W:])
    half = peer_k1.shape[-1]
    kk = jnp.concatenate([jnp.pad(peer_k1[l], ((0, 0), (0, 0), (0, half))),
                          jnp.pad(peer_k2[l], ((0, 0), (0, 0), (half, 0)))], axis=1)
    return (wg, wc, wn, wl, wab, bab), wo, peer_wq[l].astype(BF16), kk


def kernel(x_prompt, x_sample, cache_na_k, cache_na_v, state_gla_fwd, state_gla_bwd, c, c_ctx, w_ada, b_ada, norm1_w, norm2_w, w_in, w_af, b_af, w_ab, b_ab, gla_norm_w, conv_w, conv_b, na_rpb, w_out, peer_wq, peer_k1, peer_k2, peer_u, peer_v, final_norm_w):
    bp, sp, _ = x_prompt.shape
    bs, ss, _ = x_sample.shape
    depth = w_ada.shape[0]
    xp = x_prompt.reshape(bp * sp, D)
    xs = x_sample.reshape(bs * ss, D)

    cvec = jnp.zeros((8, D), F32).at[0].set(c_ctx).at[1:1 + bs].set(c)
    mods = _modulation(cvec, w_ada, b_ada)
    kctx = jnp.swapaxes(cache_na_k, 2, 3).reshape(bs, depth, -1, NA_W)
    vctx = jnp.swapaxes(cache_na_v, 2, 3).reshape(bs, depth, -1, NA_W)

    prompt_row = lambda i: 0
    sample_row = lambda tile: (lambda i: 1 + i // (ss // tile))

    vt = _transposed_values(peer_v)
    u = peer_u.astype(BF16)
    fw = final_norm_w[None, :]
    new_k, new_v, new_sf, new_sb = [], [], [], []
    for l in range(depth):
        last = l == depth - 1
        inw, wo, wq, kk = _layer_weights(l, w_in, w_af, b_af, w_ab, b_ab, w_out, peer_wq, peer_k1, peer_k2)
        mod = mods[l]
        n1, n2 = norm1_w[l][None, :], norm2_w[l][None, :]
        gnw = gla_norm_w[l][None, :]
        cw, cbias = conv_w[l], conv_b[l][None, :]
        bias = _na_bias_table(na_rpb[l])
        s0 = (_states_to_blockdiag(state_gla_fwd[:, l]), _states_to_blockdiag(state_gla_bwd[:, l]))

        zg, la, zc, zn = _inproj(xp, mod, prompt_row, n1, inw)
        yg, sf, sb = _gla(zg, la, gnw, bp, sp, None)
        yc = _conv(zc, cw, cbias, bp, sp)
        yn = _dense_attn(zn, bp, sp)
        xp, h2, q = _outproj(xp, yg, yc, yn, mod, prompt_row, n2, wo, wq)
        xp = _experts(h2, u, vt, l, _route(q, kk), xp, mod, prompt_row, fw, last)
        heads = lambda a: a.reshape(bp, sp, NA_W // HD, HD).transpose(0, 2, 1, 3)
        new_k.append(heads(zn[:, NA_W:2 * NA_W]))
        new_v.append(heads(zn[:, 2 * NA_W:]))
        new_sf.append(_blockdiag_to_states(sf))
        new_sb.append(_blockdiag_to_states(sb))

        zg, la, zc, zn = _inproj(xs, mod, sample_row(TM), n1, inw)
        yg = _gla(zg, la, gnw, bs, ss, s0)
        yc = _conv(zc, cw, cbias, bs, ss)
        yn = _neighbourhood_attn(zn, kctx[:, l:l + 1], vctx[:, l:l + 1], bias, bs, ss)
        xs, h2, q = _outproj(xs, yg, yc, yn, mod, sample_row(TM), n2, wo, wq)
        xs = _experts(h2, u, vt, l, _route(q, kk), xs, mod, sample_row(TT), fw, last)

    y_prompt = xp.reshape(bp, sp, D)
    y_sample = xs.reshape(bs, ss, D)
    return (y_prompt, y_sample, jnp.stack(new_k, axis=1), jnp.stack(new_v, axis=1),
            jnp.stack(new_sf, axis=1), jnp.stack(new_sb, axis=1))
```

```python
import functools

import numpy as np
import jax
import jax.numpy as jnp
from jax import lax
from jax.experimental import pallas as pl
from jax.experimental.pallas import tpu as pltpu

F32 = jnp.float32
BF16 = jnp.bfloat16
HI = lax.Precision.HIGHEST
NT = (((1,), (1,)), ((), ()))
TN = (((0,), (0,)), ((), ()))

LANES = 128
SUBLANES = 8
BF16_ROWS = 16
EPS = 1e-6
D = 1024
HD = 64
PAIR = 2 * HD
GLA_W = 384
CONV_W = 256
NA_W = 384
LOWRANK = 16
GLA_CHUNK = 64
GLA_SUB = 16
GLA_OUT_ROWS = 256
GLA_GROUP = 8
GLA_ROWS_PER_STEP = 1024
GLA_TAU = 16.0
EXP_CLAMP = 80.0
GRID_W = 64
NA_ROWS = 8
NA_COLS = 16
CTX_SEQS_PER_STEP = 4
NA_ROWS_PER_STEP = 4
NA_BLOCKS_PER_STEP = 4
PEER_HEADS = 8
NKEYS = 128
TOPK = 16
TM = 512
CUM_ROWS = 256
TT = 512
EXPERT_BLOCK = 2048
EXPERT_SUB = 512
VALUE_CHUNK = 2048
ROUTE_HEADS_PER_STEP = 8
ROUTE_TM = 128
VMEM_LIMIT = 56 * 1024 * 1024


def _cparams(*sem):
    return pltpu.CompilerParams(dimension_semantics=sem, vmem_limit_bytes=VMEM_LIMIT)


def _silu(x):
    return x * jax.nn.sigmoid(x)


def _gelu_tanh(x):
    c0 = float(np.sqrt(2.0 / np.pi))
    z = x * (c0 + (0.044715 * c0) * (x * x))
    hx = 0.5 * x
    return hx + hx * jnp.tanh(z)


def _norm_mod(x, w, scale, shift):
    ms = jnp.mean(x * x, axis=-1, keepdims=True)
    return x * lax.rsqrt(ms + EPS) * w * (1.0 + scale) + shift


def _head_masks():
    lane = lax.broadcasted_iota(jnp.int32, (1, PAIR), 1)
    m0 = (lane < HD).astype(F32)
    return m0, 1.0 - m0


def _mod_kernel(c_ref, w_ref, b_ref, o_ref):
    s = _silu(c_ref[...]).astype(BF16)
    o_ref[0] = jnp.dot(s, w_ref[0].astype(BF16), preferred_element_type=F32) + b_ref[0]


def _modulation(cvec, w_ada, b_ada):
    depth = w_ada.shape[0]
    nb = w_ada.shape[2] // D
    out = pl.pallas_call(
        _mod_kernel,
        out_shape=jax.ShapeDtypeStruct((depth, 8, nb * D), F32),
        grid=(depth, nb),
        in_specs=[pl.BlockSpec((8, D), lambda l, j: (0, 0)),
                  pl.BlockSpec((1, D, D), lambda l, j: (l, 0, j)),
                  pl.BlockSpec((1, 1, D), lambda l, j: (l, 0, j))],
        out_specs=pl.BlockSpec((1, 8, D), lambda l, j: (l, 0, j)),
        compiler_params=_cparams("arbitrary", "arbitrary"),
        name="adaln_modulation",
    )(cvec, w_ada, b_ada.reshape(depth, 1, nb * D))
    return out.reshape(depth, 8, nb, D)


def _inproj_kernel(x_ref, mod_ref, nw_ref, wg_ref, wc_ref, wn_ref, wl_ref, wab_ref, bab_ref,
                   zg_ref, la_ref, zc_ref, zn_ref):
    h = _norm_mod(x_ref[...], nw_ref[...], mod_ref[0, 1:2, :], mod_ref[0, 0:1, :])
    hb = h.astype(BF16)
    zg_ref[...] = jnp.dot(hb, wg_ref[...], preferred_element_type=F32)
    zc_ref[...] = jnp.dot(hb, wc_ref[...], preferred_element_type=F32)
    zn_ref[...] = jnp.dot(hb, wn_ref[...], preferred_element_type=F32)
    lr = jnp.dot(hb, wl_ref[...], preferred_element_type=F32)
    hi = lr.astype(BF16)
    rest = lr - hi.astype(F32)
    mid = rest.astype(BF16)
    lo = (rest - mid.astype(F32)).astype(BF16)
    grp = lax.broadcasted_iota(jnp.int32, (1, lr.shape[1]), 1) >> ((2 * LOWRANK).bit_length() - 1)
    pieces = jnp.where((grp == 2) | (grp == 5), mid, jnp.where(grp == 4, lo, hi))
    zz = jnp.dot(pieces, wab_ref[...], preferred_element_type=F32) + bab_ref[...]
    la = (jnp.minimum(zz, 0.0) - jnp.log(1.0 + jnp.exp(-jnp.abs(zz)))) * (1.0 / GLA_TAU)
    r = lax.broadcasted_iota(jnp.int32, (CUM_ROWS, CUM_ROWS), 0)
    s = lax.broadcasted_iota(jnp.int32, (CUM_ROWS, CUM_ROWS), 1)
    shift = GLA_CHUNK.bit_length() - 1
    same = (r >> shift) == (s >> shift)
    tri_f = (same & (s <= r)).astype(BF16)
    tri_b = (same & (s >= r)).astype(BF16)
    hi = la.astype(BF16)
    rest = la - hi.astype(F32)
    mid = rest.astype(BF16)
    lo = (rest - mid.astype(F32)).astype(BF16)
    for blk in range(TM // CUM_ROWS):
        rows = slice(blk * CUM_ROWS, (blk + 1) * CUM_ROWS)
        for tri, cols in ((tri_f, slice(0, GLA_W)), (tri_b, slice(GLA_W, 2 * GLA_W))):
            la_ref[rows, cols] = (jnp.dot(tri, hi[rows, cols], preferred_element_type=F32)
                                  + jnp.dot(tri, mid[rows, cols], preferred_element_type=F32)
                                  + jnp.dot(tri, lo[rows, cols], preferred_element_type=F32))


def _const_spec(shape):
    return pl.BlockSpec(shape, lambda *_: (0,) * len(shape))


def _inproj(x, mod, mod_row, nw, wts):
    n = x.shape[0]
    wg, wc, wn, wl, wab, bab = wts
    row = lambda w: pl.BlockSpec((TM, w), lambda i: (i, 0))
    return pl.pallas_call(
        _inproj_kernel,
        out_shape=(jax.ShapeDtypeStruct((n, 4 * GLA_W), F32), jax.ShapeDtypeStruct((n, 2 * GLA_W), F32),
                   jax.ShapeDtypeStruct((n, 3 * CONV_W), F32), jax.ShapeDtypeStruct((n, 3 * NA_W), F32)),
        grid=(n // TM,),
        in_specs=[row(D), pl.BlockSpec((1, 6, D), lambda i: (mod_row(i), 0, 0)), _const_spec((1, D)),
                  _const_spec(wg.shape), _const_spec(wc.shape), _const_spec(wn.shape), _const_spec(wl.shape),
                  _const_spec(wab.shape), _const_spec(bab.shape)],
        out_specs=(row(4 * GLA_W), row(2 * GLA_W), row(3 * CONV_W), row(3 * NA_W)),
        compiler_params=_cparams("arbitrary"),
        name="in_projection",
    )(x, mod, nw, wg, wc, wn, wl, wab, bab)


def _gla_consts(fwd):
    c, sb = GLA_CHUNK, GLA_SUB
    r = lax.broadcasted_iota(jnp.int32, (2 * c, c), 0) & (c - 1)
    s = lax.broadcasted_iota(jnp.int32, (2 * c, c), 1)
    caus = (s <= r) if fwd else (s >= r)
    rowid = lax.broadcasted_iota(jnp.int32, (c, PAIR), 0)
    seen = [(rowid < (i + 1) * sb) if fwd else (rowid >= i * sb) for i in range(c // sb)]
    m0, m1 = _head_masks()
    qsel = [[((rowid >= i * sb) & (rowid < (i + 1) * sb)).astype(F32) * m for i in range(c // sb)]
            for m in (m0, m1)]
    return caus, seen, qsel


def _gla_operands(qc, kc, vc, cum, fwd, consts):
    _, seen, qsel = consts
    c, sb = GLA_CHUNK, GLA_SUB
    nsb = c // sb
    zero_row = jnp.zeros((1, PAIR), F32)
    if fwd:
        last = cum[c - 1:c]
        starts = [zero_row] + [cum[i * sb - 1:i * sb] for i in range(1, nsb)]
    else:
        last = cum[0:1]
        starts = [cum[(i + 1) * sb:(i + 1) * sb + 1] for i in range(nsb - 1)] + [zero_row]
    bm = jnp.concatenate([jnp.broadcast_to(b, (sb, PAIR)) for b in starts], axis=0)
    qt = qc * jnp.exp(cum - bm)
    kbig = jnp.concatenate(
        [(kc * jnp.exp(jnp.where(seen[i], jnp.minimum(starts[i] - cum, EXP_CLAMP), 0.0))).astype(BF16)
         for i in range(nsb)], axis=1)
    qbig = jnp.concatenate(
        [jnp.concatenate([(qt * qsel[h][i]).astype(BF16) for i in range(nsb)], axis=1) for h in range(2)],
        axis=0)
    return dict(qbig=qbig, kbig=kbig, vb=vc.astype(BF16), qdec=(qc * jnp.exp(cum)).astype(BF16),
                khat=(kc * jnp.exp(last - cum)).astype(BF16), decay=jnp.exp(last))


def _gla_group(chunks, st, consts, masks, bd):
    caus = consts[0]
    m0, m1 = masks
    c = GLA_CHUNK
    atts = [lax.dot_general(ch['qbig'], ch['kbig'], NT, preferred_element_type=F32) for ch in chunks]
    upds = [lax.dot_general(ch['vb'], ch['khat'], TN, preferred_element_type=F32) for ch in chunks]
    atts = [jnp.where(caus, a, 0.0).astype(BF16) for a in atts]
    intras = [jnp.dot(a[:c], ch['vb'], preferred_element_type=F32) * m0
              + jnp.dot(a[c:], ch['vb'], preferred_element_type=F32) * m1 for a, ch in zip(atts, chunks)]
    outs = []
    for ch, upd, intra in zip(chunks, upds, intras):
        outs.append(intra + lax.dot_general(ch['qdec'], st.astype(BF16), NT, preferred_element_type=F32))
        st = st * ch['decay'] + upd * bd
    return outs, st


def _gla_kernel(*refs, seq, nseq, state_in):
    if state_in:
        q_ref, k_ref, v_ref, g_ref, laf_ref, lab_ref, nw_ref, s0f_ref, s0b_ref, y_ref, of_scr, ob_scr = refs
    else:
        q_ref, k_ref, v_ref, g_ref, laf_ref, lab_ref, nw_ref, y_ref, sf_ref, sb_ref, of_scr, ob_scr = refs
    c = GLA_CHUNK
    nc = seq // c
    masks = _head_masks()
    m0, m1 = masks
    rr = lax.broadcasted_iota(jnp.int32, (PAIR, PAIR), 0)
    cc = lax.broadcasted_iota(jnp.int32, (PAIR, PAIR), 1)
    bd = ((rr < HD) == (cc < HD)).astype(F32)
    cf = _gla_consts(True)
    cb = _gla_consts(False)
    scale = HD ** -0.5
    nw = nw_ref[...]

    grp = min(GLA_GROUP, nc)
    zero = jnp.zeros((PAIR, PAIR), F32)

    def scan_sequence(sq):
        def rows(i):
            start = sq * seq + i * c
            return pl.ds(start if isinstance(start, int) else pl.multiple_of(start, c), c)

        def scan_body(j, carry):
            stf, stb = carry
            slf = [rows(j * grp + k) for k in range(grp)]
            slb = [rows(nc - 1 - (j * grp + k)) for k in range(grp)]
            chf = [_gla_operands(q_ref[s, :] * scale, k_ref[s, :], v_ref[s, :], laf_ref[s, :], True, cf)
                   for s in slf]
            chb = [_gla_operands(q_ref[s, :] * scale, k_ref[s, :], v_ref[s, :], lab_ref[s, :], False, cb)
                   for s in slb]
            of, stf = _gla_group(chf, stf, cf, masks, bd)
            ob, stb = _gla_group(chb, stb, cb, masks, bd)
            for s, o in zip(slf, of):
                of_scr[s, :] = o
            for s, o in zip(slb, ob):
                ob_scr[s, :] = o
            return stf, stb

        init = (s0f_ref[sq, 0], s0b_ref[sq, 0]) if state_in else (zero, zero)
        if nc == grp:
            sf, sb = scan_body(0, init)
        else:
            sf, sb = lax.fori_loop(0, nc // grp, scan_body, init)
        if not state_in:
            sf_ref[sq, 0] = sf
            sb_ref[sq, 0] = sb

    for sq in range(nseq):
        scan_sequence(sq)

    def out_body(i, carry):
        sl = pl.ds(pl.multiple_of(i * GLA_OUT_ROWS, GLA_OUT_ROWS), GLA_OUT_ROWS)
        tot = of_scr[sl, :] + ob_scr[sl, :]
        sq = tot * tot
        ms = (jnp.sum(sq * m0, axis=-1, keepdims=True) * m0
              + jnp.sum(sq * m1, axis=-1, keepdims=True) * m1) * (1.0 / HD)
        y_ref[sl, :] = tot * lax.rsqrt(ms + EPS) * nw * _silu(g_ref[sl, :])
        return carry

    lax.fori_loop(0, nseq * seq // GLA_OUT_ROWS, out_body, 0)


def _gla(zg, la, nw, batch, seq, states):
    n = zg.shape[0]
    npair = GLA_W // PAIR
    nseq = max(1, min(batch, GLA_ROWS_PER_STEP // seq))
    assert batch % nseq == 0
    col = lambda off: pl.BlockSpec((nseq * seq, PAIR), lambda b, p: (b, off + p))
    st_spec = pl.BlockSpec((nseq, 1, PAIR, PAIR), lambda b, p: (b, p, 0, 0))
    in_specs = [col(0), col(npair), col(2 * npair), col(3 * npair), col(0), col(npair),
                pl.BlockSpec((1, PAIR), lambda b, p: (0, p))]
    args = [zg, zg, zg, zg, la, la, nw]
    y_shape = jax.ShapeDtypeStruct((n, GLA_W), F32)
    y_spec = pl.BlockSpec((nseq * seq, PAIR), lambda b, p: (b, p))
    if states is None:
        st_shape = jax.ShapeDtypeStruct((batch, npair, PAIR, PAIR), F32)
        out_shape, out_specs = (y_shape, st_shape, st_shape), (y_spec, st_spec, st_spec)
    else:
        in_specs += [st_spec, st_spec]
        args += list(states)
        out_shape, out_specs = y_shape, y_spec
    return pl.pallas_call(
        functools.partial(_gla_kernel, seq=seq, nseq=nseq, state_in=states is not None),
        out_shape=out_shape,
        grid=(batch // nseq, npair),
        in_specs=in_specs,
        out_specs=out_specs,
        scratch_shapes=[pltpu.VMEM((nseq * seq, PAIR), F32), pltpu.VMEM((nseq * seq, PAIR), F32)],
        compiler_params=_cparams("arbitrary", "arbitrary"),
        name="gla_bidir",
    )(*args)


def _states_to_blockdiag(s):
    b, h = s.shape[:2]
    st = jnp.swapaxes(s, -1, -2).reshape(b, h // 2, 2, HD, HD)
    eye = jnp.eye(2, dtype=s.dtype)
    return jnp.einsum('bpivk,ij->bpivjk', st, eye).reshape(b, h // 2, PAIR, PAIR)


def _blockdiag_to_states(sbd):
    b, p = sbd.shape[:2]
    s6 = sbd.reshape(b, p, 2, HD, 2, HD)
    diag = jnp.stack([s6[:, :, 0, :, 0, :], s6[:, :, 1, :, 1, :]], axis=2)
    return jnp.swapaxes(diag, -1, -2).reshape(b, 2 * p, HD, HD)


def _conv_kernel(ch_ref, cb_ref, cc_ref, w_ref, b_ref, y_ref, *, seq):
    u = cc_ref[...] * ch_ref[...]
    row = lax.broadcasted_iota(jnp.int32, u.shape, 0)
    prev = jnp.where(row == 0, 0.0, pltpu.roll(u, 1, 0))
    nxt = jnp.where(row == seq - 1, 0.0, pltpu.roll(u, seq - 1, 0))
    y_ref[...] = cb_ref[...] * (w_ref[0:1, :] * prev + w_ref[1:2, :] * u + w_ref[2:3, :] * nxt + b_ref[...])


def _conv(zc, w, b, batch, seq):
    col = lambda j: pl.BlockSpec((seq, CONV_W), lambda i: (i, j))
    return pl.pallas_call(
        functools.partial(_conv_kernel, seq=seq),
        out_shape=jax.ShapeDtypeStruct((zc.shape[0], CONV_W), F32),
        grid=(batch,),
        in_specs=[col(0), col(1), col(2), _const_spec(w.shape), _const_spec(b.shape)],
        out_specs=col(0),
        compiler_params=_cparams("arbitrary"),
        name="gated_conv",
    )(zc, zc, zc, w, b)


def _dense_attn_kernel(q_ref, k_ref, v_ref, o_ref, *, seq, nseq):
    m0, m1 = _head_masks()
    for b in range(nseq):
        rows = slice(b * seq, (b + 1) * seq)
        q = q_ref[rows, :] * (HD ** -0.5)
        qstack = jnp.concatenate([q * m0, q * m1], axis=0).astype(BF16)
        s = lax.dot_general(qstack, k_ref[rows, :].astype(BF16), NT, preferred_element_type=F32)
        p = jnp.exp(s - jnp.max(s, axis=-1, keepdims=True))
        o = (jnp.dot(p.astype(BF16), v_ref[rows, :].astype(BF16), preferred_element_type=F32)
             / jnp.sum(p, axis=-1, keepdims=True))
        o_ref[rows, :] = o[:seq] * m0 + o[seq:] * m1


def _dense_attn(zn, batch, seq):
    npair = NA_W // PAIR
    nseq = min(CTX_SEQS_PER_STEP, batch)
    assert batch % nseq == 0
    col = lambda off: pl.BlockSpec((nseq * seq, PAIR), lambda b, p: (b, off + p))
    return pl.pallas_call(
        functools.partial(_dense_attn_kernel, seq=seq, nseq=nseq),
        out_shape=jax.ShapeDtypeStruct((zn.shape[0], NA_W), F32),
        grid=(batch // nseq, npair),
        in_specs=[col(0), col(npair), col(2 * npair)],
        out_specs=col(0),
        compiler_params=_cparams("arbitrary", "arbitrary"),
        name="context_attention",
    )(zn, zn, zn)


def _na_kernel(q_ref, k_ref, v_ref, kc_ref, vc_ref, tp_ref, o_ref, *, rows, span):
    rb = NA_ROWS_PER_STEP
    kc = kc_ref[0, 0].astype(BF16)
    vc = vc_ref[0, 0].astype(BF16)
    m0, m1 = _head_masks()
    shift = GRID_W.bit_length() - 1
    keyrow = lax.broadcasted_iota(jnp.int32, (1, span * GRID_W), 1) >> shift

    def block(bi):
        g = pl.program_id(2) * NA_BLOCKS_PER_STEP + bi
        q0 = bi * rb * GRID_W
        us = jnp.clip(g * rb - NA_ROWS // 2, 0, rows - span)
        win = pl.ds(pl.multiple_of(us * GRID_W, GRID_W), span * GRID_W)
        kw = k_ref[win, :].astype(BF16)
        vw = v_ref[win, :].astype(BF16)
        qs = []
        for rr in range(rb):
            q = q_ref[q0 + rr * GRID_W:q0 + (rr + 1) * GRID_W, :] * (HD ** -0.5)
            qs += [q * m0, q * m1]
        qstack = jnp.concatenate(qs, axis=0).astype(BF16)
        sw = lax.dot_general(qstack, kw, NT, preferred_element_type=F32)
        sc = lax.dot_general(qstack, kc, NT, preferred_element_type=F32)
        parts = []
        for rr in range(rb):
            r = g * rb + rr
            lo = jnp.clip(r - NA_ROWS // 2, 0, rows - NA_ROWS) - us
            inside = (keyrow >= lo) & (keyrow < lo + NA_ROWS)
            blk = sw[rr * PAIR:(rr + 1) * PAIR]
            tiles = [blk[:, jp * PAIR:(jp + 1) * PAIR]
                     + tp_ref[0, jnp.clip(us + 2 * jp - r + NA_ROWS, 0, 2 * NA_ROWS - 1)]
                     for jp in range(span // 2)]
            parts.append(jnp.where(inside, jnp.concatenate(tiles, axis=1), -1e30))
        sw = jnp.concatenate(parts, axis=0)
        mx = jnp.maximum(jnp.max(sw, axis=-1, keepdims=True), jnp.max(sc, axis=-1, keepdims=True))
        pw = jnp.exp(sw - mx)
        pc = jnp.exp(sc - mx)
        den = jnp.sum(pw, axis=-1, keepdims=True) + jnp.sum(pc, axis=-1, keepdims=True)
        o = (jnp.dot(pw.astype(BF16), vw, preferred_element_type=F32)
             + jnp.dot(pc.astype(BF16), vc, preferred_element_type=F32)) / den
        for rr in range(rb):
            o_ref[q0 + rr * GRID_W:q0 + (rr + 1) * GRID_W, :] = (o[rr * PAIR:rr * PAIR + HD] * m0
                                                                 + o[rr * PAIR + HD:(rr + 1) * PAIR] * m1)

    for bi in range(NA_BLOCKS_PER_STEP):
        block(bi)


def _na_bias_table(rpb):
    nh = rpb.shape[0]
    cols = np.arange(GRID_W)
    cs = np.clip(cols - NA_COLS // 2, 0, GRID_W - NA_COLS)
    col_mask = (cols[None, :] >= cs[:, None]) & (cols[None, :] < cs[:, None] + NA_COLS)
    dc = np.clip(cols[None, :] - cols[:, None], -(NA_COLS - 1), NA_COLS - 1) + NA_COLS - 1
    onehot = (dc[:, :, None] == np.arange(2 * NA_COLS - 1)).astype(np.float32)
    toep = jnp.einsum('hab,qkb->haqk', rpb.astype(F32), onehot, precision=HI)
    toep = jnp.where(col_mask[None, None], toep, -1e30)
    ext = jnp.pad(toep, ((0, 0), (1, 1), (0, 0), (0, 0)), constant_values=-1e30)
    two = jnp.concatenate([ext[:, :-1], ext[:, 1:]], axis=-1)
    two = two.reshape(nh // 2, 2, 2 * NA_ROWS, GRID_W, 2 * GRID_W).transpose(0, 2, 1, 3, 4)
    return two.reshape(nh // 2, 2 * NA_ROWS, PAIR, PAIR)


def _neighbourhood_attn(zn, kctx, vctx, bias, batch, seq):
    npair = NA_W // PAIR
    rows = seq // GRID_W
    ctx = kctx.shape[2]
    rb = NA_ROWS_PER_STEP
    nblk = NA_BLOCKS_PER_STEP
    assert rows % (rb * nblk) == 0 and rows >= NA_ROWS
    steps = rows // (rb * nblk)
    span = min(rows, NA_ROWS + rb)
    qcol = pl.BlockSpec((nblk * rb * GRID_W, PAIR), lambda p, b, r: (b * steps + r, p))
    seqcol = lambda off: pl.BlockSpec((seq, PAIR), lambda p, b, r: (b, off + p))
    ctxcol = pl.BlockSpec((1, 1, ctx, PAIR), lambda p, b, r: (b, 0, 0, p))
    bias_spec = pl.BlockSpec((1, 2 * NA_ROWS, PAIR, PAIR), lambda p, b, r: (p, 0, 0, 0))
    return pl.pallas_call(
        functools.partial(_na_kernel, rows=rows, span=span),
        out_shape=jax.ShapeDtypeStruct((zn.shape[0], NA_W), F32),
        grid=(npair, batch, steps),
        in_specs=[qcol, seqcol(npair), seqcol(2 * npair), ctxcol, ctxcol, bias_spec],
        out_specs=qcol,
        compiler_params=_cparams("arbitrary", "arbitrary", "arbitrary"),
        name="neighbourhood_attention",
    )(zn, zn, zn, kctx, vctx, bias)


def _outproj_kernel(x_ref, yg_ref, yc_ref, yn_ref, mod_ref, nw_ref, wg_ref, wc_ref, wn_ref, wq_ref,
                    xo_ref, h_ref, q_ref):
    y = (jnp.dot(yg_ref[...].astype(BF16), wg_ref[...], preferred_element_type=F32)
         + jnp.dot(yc_ref[...].astype(BF16), wc_ref[...], preferred_element_type=F32)
         + jnp.dot(yn_ref[...].astype(BF16), wn_ref[...], preferred_element_type=F32))
    x = x_ref[...] + mod_ref[0, 2:3, :] * y
    xo_ref[...] = x
    hb = _norm_mod(x, nw_ref[...], mod_ref[0, 4:5, :], mod_ref[0, 3:4, :]).astype(BF16)
    h_ref[...] = hb
    q_ref[...] = jnp.dot(hb, wq_ref[...], preferred_element_type=F32)


def _outproj(x, yg, yc, yn, mod, mod_row, nw, wts, wq):
    n = x.shape[0]
    wg, wc, wn = wts
    row = lambda w: pl.BlockSpec((TM, w), lambda i: (i, 0))
    return pl.pallas_call(
        _outproj_kernel,
        out_shape=(jax.ShapeDtypeStruct((n, D), F32), jax.ShapeDtypeStruct((n, D), BF16),
                   jax.ShapeDtypeStruct((n, wq.shape[1]), F32)),
        grid=(n // TM,),
        in_specs=[row(D), row(GLA_W), row(CONV_W), row(NA_W),
                  pl.BlockSpec((1, 6, D), lambda i: (mod_row(i), 0, 0)), _const_spec((1, D)),
                  _const_spec(wg.shape), _const_spec(wc.shape), _const_spec(wn.shape), _const_spec(wq.shape)],
        out_specs=(row(D), row(D), row(wq.shape[1])),
        compiler_params=_cparams("arbitrary"),
        name="out_projection",
    )(x, yg, yc, yn, mod, nw, wg, wc, wn, wq)


def _count_prefix(pred, top, n):
    def pick(bits, lo, width):
        if not bits:
            return top[lo + width // 2 - 1]
        return jnp.where(bits[0], pick(bits[1:], lo + width // 2, width // 2),
                         pick(bits[1:], lo, width // 2))

    bits = []
    width = n
    while width > 1:
        bits.append(pred(pick(bits, 0, n)))
        width //= 2
    count = sum(jnp.where(b, float(n >> (i + 1)), 0.0) for i, b in enumerate(bits))
    return jnp.where(pred(top[n - 1]), float(n), count)


def _sort16_network():
    def merge(lo, hi, r):
        step = r * 2
        if step < hi - lo:
            yield from merge(lo, hi, step)
            yield from merge(lo + r, hi, step)
            yield from [(i, i + r) for i in range(lo + r, hi - r, step)]
        else:
            yield (lo, lo + r)

    def sort(lo, hi):
        if hi - lo >= 1:
            mid = lo + (hi - lo) // 2
            yield from sort(lo, mid)
            yield from sort(mid + 1, hi)
            yield from merge(lo, hi, 1)

    return tuple(sort(0, TOPK - 1))


_SORT16 = _sort16_network()
_BITONIC16 = tuple((i, i + d) for d in (8, 4, 2, 1) for i in range(TOPK) if not i & d)


def _exchange(v, pairs):
    v = list(v)
    for i, j in pairs:
        a, b = v[i], v[j]
        if b is None:
            continue
        if a is None:
            v[i], v[j] = b, None
        else:
            v[i], v[j] = jnp.maximum(a, b), jnp.minimum(a, b)
    return v


def _top16_sorted(tiles):
    v = _exchange(list(tiles) + [None] * (TOPK - len(tiles)), _SORT16)
    for shift in (4, 2, 1):
        other = [None if t is None else pltpu.roll(t, shift, 0) for t in v]
        merged = []
        for k in range(TOPK):
            a, b = v[k], other[TOPK - 1 - k]
            merged.append(b if a is None else a if b is None else jnp.maximum(a, b))
        v = _exchange(merged, _BITONIC16)
    return v


def _route_kernel(q_ref, kk_ref, r1_ref, e1_ref, c2_ref, e2_ref, v1_scr, v2_scr):
    sub = SUBLANES
    for h in range(ROUTE_HEADS_PER_STEP):
        qh = q_ref[:, h * LANES:(h + 1) * LANES]
        st = lax.dot_general(kk_ref[h], qh, NT, precision=HI, preferred_element_type=F32)
        s1 = st[:NKEYS]
        s2 = st[NKEYS:]
        tiles1 = [s1[g * sub:(g + 1) * sub] for g in range(NKEYS // sub)]
        top1 = _top16_sorted(tiles1)
        tiles2 = [s2[g * sub:(g + 1) * sub] for g in range(NKEYS // sub)]
        top2 = _top16_sorted(tiles2)
        for k in range(TOPK):
            v1_scr[k:k + 1, :] = top1[k][0:1]
            v2_scr[k:k + 1, :] = top2[k][0:1]
        r1_ref[h] = jnp.concatenate([_count_prefix(lambda p, t=t: p > t, top1, TOPK) for t in tiles1], axis=0)
        v1 = v1_scr[...]
        v2 = v2_scr[...]
        half = TOPK // 2
        cand = ([v1[0:1] + v2[:half], v1[0:1] + v2[half:]] + [v1[a:a + 1] + v2[:half] for a in range(1, half)]
                + [v1[half:] + v2[0:1]])
        top = _top16_sorted(cand)
        z = sum(jnp.exp(t - top[0]) for t in top)[0:1]
        tau = top[TOPK - 1]
        tail = sum(jnp.where(top1[a] + top2[0] >= tau, 1.0, 0.0) for a in range(half, TOPK))
        count = jnp.concatenate(
            [_count_prefix(lambda p, t=t: p + t >= tau, top1, half) + jnp.where(t == top2[0], tail, 0.0)
             for t in tiles2], axis=0)
        e1_ref[h] = jnp.exp(s1 - v1[0:1])
        c2_ref[h] = count.astype(BF16)
        e2_ref[h] = (jnp.exp(s2 - v2[0:1]) * (1.0 / z)).astype(BF16)


def _route(q, kk):
    n = q.shape[0]
    hps = ROUTE_HEADS_PER_STEP
    tok = pl.BlockSpec((hps, NKEYS, ROUTE_TM), lambda i, g: (g, 0, i))
    rows = jax.ShapeDtypeStruct((PEER_HEADS, NKEYS, n), F32)
    cols = jax.ShapeDtypeStruct((PEER_HEADS, NKEYS, n), BF16)
    top = pltpu.VMEM((TOPK, ROUTE_TM), F32)
    return pl.pallas_call(
        _route_kernel,
        out_shape=(rows, rows, cols, cols),
        grid=(n // ROUTE_TM, PEER_HEADS // hps),
        in_specs=[pl.BlockSpec((ROUTE_TM, hps * LANES), lambda i, g: (i, g)),
                  pl.BlockSpec((hps, 2 * NKEYS, LANES), lambda i, g: (g, 0, 0))],
        out_specs=(tok, tok, tok, tok),
        scratch_shapes=[top, top],
        compiler_params=_cparams("arbitrary", "arbitrary"),
        name="peer_routing",
    )(q, kk)


def _expert_kernel(h_ref, u_ref, vt_ref, r1_ref, e1_ref, c2_ref, e2_ref, x_ref, mod_ref, fw_ref, o_ref, acc, w_scr,
                   *, final_norm):
    eb = pl.program_id(1)

    @pl.when(eb == 0)
    def _():
        acc[...] = jnp.zeros_like(acc)

    hb = h_ref[...]
    pack = BF16_ROWS
    keys_per_sub = EXPERT_SUB // NKEYS
    nsub = EXPERT_BLOCK // EXPERT_SUB

    def pre_act(sub):
        rows = slice(sub * EXPERT_SUB, (sub + 1) * EXPERT_SUB)
        return lax.dot_general(u_ref[0, rows, :], hb, NT, preferred_element_type=F32).astype(BF16)

    def gated(sub, pre):
        act = _gelu_tanh(pre)
        tile = (NKEYS // pack, pack, TT)
        key0 = eb * (EXPERT_BLOCK // NKEYS) + sub * keys_per_sub
        for ii in range(keys_per_sub):
            gate = jnp.zeros(tile, BF16)
            for h in range(PEER_HEADS):
                r1 = jnp.broadcast_to(r1_ref[h, pl.ds(key0 + ii, 1), :], (pack, TT)).astype(BF16)
                e1 = jnp.broadcast_to(e1_ref[h, pl.ds(key0 + ii, 1), :], (pack, TT)).astype(BF16)
                keep = c2_ref[h].reshape(tile) > r1[None]
                gate = gate + jnp.where(keep, e1[None], 0.0) * e2_ref[h].reshape(tile)
            rows = slice(ii * NKEYS, (ii + 1) * NKEYS)
            w_scr[sub * EXPERT_SUB + ii * NKEYS:sub * EXPERT_SUB + (ii + 1) * NKEYS, :] = (
                gate.reshape(NKEYS, TT) * act[rows])

    per_chunk = VALUE_CHUNK // EXPERT_SUB
    pre = pre_act(0)
    out = None
    for sub in range(nsub):
        nxt = pre_act(sub + 1) if sub + 1 < nsub else None
        gated(sub, pre)
        if (sub + 1) % per_chunk == 0:
            ch = sub // per_chunk
            part = jnp.dot(vt_ref[0, ch], w_scr[ch * VALUE_CHUNK:(ch + 1) * VALUE_CHUNK, :],
                           preferred_element_type=F32)
            out = part if out is None else out + part
        pre = nxt
    acc[...] += out

    @pl.when(eb == pl.num_programs(1) - 1)
    def _():
        x = x_ref[...] + mod_ref[0, 5:6, :] * acc[...].T
        if final_norm:
            x = x * lax.rsqrt(jnp.mean(x * x, axis=-1, keepdims=True) + EPS) * fw_ref[...]
        o_ref[...] = x


def _experts(h2, u, vt, layer, route, x, mod, mod_row, final_w, final_norm):
    n = h2.shape[0]
    tok = pl.BlockSpec((PEER_HEADS, NKEYS, TT), lambda i, e: (0, 0, i))
    return pl.pallas_call(
        functools.partial(_expert_kernel, final_norm=final_norm),
        out_shape=jax.ShapeDtypeStruct((n, D), F32),
        grid=(n // TT, u.shape[1] // EXPERT_BLOCK),
        in_specs=[pl.BlockSpec((TT, D), lambda i, e: (i, 0)),
                  pl.BlockSpec((1, EXPERT_BLOCK, D), lambda i, e: (layer, e, 0)),
                  pl.BlockSpec((1, EXPERT_BLOCK // VALUE_CHUNK, D, VALUE_CHUNK), lambda i, e: (layer, e, 0, 0)),
                  tok, tok, tok, tok,
                  pl.BlockSpec((TT, D), lambda i, e: (i, 0)),
                  pl.BlockSpec((1, 6, D), lambda i, e: (mod_row(i), 0, 0)),
                  pl.BlockSpec((1, D), lambda i, e: (0, 0))],
        out_specs=pl.BlockSpec((TT, D), lambda i, e: (i, 0)),
        scratch_shapes=[pltpu.VMEM((D, TT), F32),
                        pltpu.VMEM((EXPERT_BLOCK, TT), BF16)],
        compiler_params=_cparams("arbitrary", "arbitrary"),
        name="peer_experts",
    )(h2, u, vt, *route, x, mod, final_w)


def _vt_kernel(v_ref, o_ref):
    o_ref[0, 0] = v_ref[0].T.astype(BF16)


def _transposed_values(peer_v):
    depth, ne, _ = peer_v.shape
    return pl.pallas_call(
        _vt_kernel,
        out_shape=jax.ShapeDtypeStruct((depth, ne // VALUE_CHUNK, D, VALUE_CHUNK), BF16),
        grid=(depth, ne // VALUE_CHUNK),
        in_specs=[pl.BlockSpec((1, VALUE_CHUNK, D), lambda l, c: (l, c, 0))],
        out_specs=pl.BlockSpec((1, 1, D, VALUE_CHUNK), lambda l, c: (l, c, 0, 0)),
        compiler_params=_cparams("arbitrary", "arbitrary"),
        name="expert_value_layout",
    )(peer_v)


def _layer_weights(l, w_in, w_af, b_af, w_ab, b_ab, w_out, peer_wq, peer_k1, peer_k2):
    o_lr = 4 * GLA_W
    o_conv = o_lr + 2 * LOWRANK
    o_na = o_conv + 3 * CONV_W
    wi = w_in[l]
    wg = wi[:, :o_lr].astype(BF16)
    wl = jnp.tile(wi[:, o_lr:o_conv], (1, 8)).astype(BF16)
    wc = wi[:, o_conv:o_na].astype(BF16)
    wn = wi[:, o_na:].astype(BF16)
    wa = jnp.zeros((2 * LOWRANK, 2 * GLA_W), F32)
    wa = wa.at[:LOWRANK, :GLA_W].set(w_af[l]).at[LOWRANK:, GLA_W:].set(w_ab[l])
    a_hi = wa.astype(BF16)
    a_rest = wa - a_hi.astype(F32)
    a_mid = a_rest.astype(BF16)
    a_lo = (a_rest - a_mid.astype(F32)).astype(BF16)
    zero = jnp.zeros_like(a_hi)
    wab = jnp.concatenate([a_hi, a_mid, a_hi, a_lo, a_hi, a_mid, zero, zero], axis=0)
    bab = jnp.concatenate([b_af[l], b_ab[l]])[None, :]
    wo = w_out[l].astype(BF16)
    wo = (wo[:GLA_W], wo[GLA_W:GLA_W + CONV_W], wo[GLA_W + CONV_W:])
    half = peer_k1.shape[-1]
    kk = jnp.concatenate([jnp.pad(peer_k1[l], ((0, 0), (0, 0), (0, half))),
                          jnp.pad(peer_k2[l], ((0, 0), (0, 0), (half, 0)))], axis=1)
    return (wg, wc, wn, wl, wab, bab), wo, peer_wq[l].astype(BF16), kk


def kernel(x_prompt, x_sample, cache_na_k, cache_na_v, state_gla_fwd, state_gla_bwd, c, c_ctx, w_ada, b_ada, norm1_w, norm2_w, w_in, w_af, b_af, w_ab, b_ab, gla_norm_w, conv_w, conv_b, na_rpb, w_out, peer_wq, peer_k1, peer_k2, peer_u, peer_v, final_norm_w):
    bp, sp, _ = x_prompt.shape
    bs, ss, _ = x_sample.shape
    depth = w_ada.shape[0]
    xp = x_prompt.reshape(bp * sp, D)
    xs = x_sample.reshape(bs * ss, D)

    cvec = jnp.zeros((8, D), F32).at[0].set(c_ctx).at[1:1 + bs].set(c)
    mods = _modulation(cvec, w_ada, b_ada)
    kctx = jnp.swapaxes(cache_na_k, 2, 3).reshape(bs, depth, -1, NA_W)
    vctx = jnp.swapaxes(cache_na_v, 2, 3).reshape(bs, depth, -1, NA_W)

    prompt_row = lambda i: 0
    sample_row = lambda tile: (lambda i: 1 + i // (ss // tile))

    vt = _transposed_values(peer_v)
    u = peer_u.astype(BF16)
    fw = final_norm_w[None, :]
    new_k, new_v, new_sf, new_sb = [], [], [], []
    for l in range(depth):
        last = l == depth - 1
        inw, wo, wq, kk = _layer_weights(l, w_in, w_af, b_af, w_ab, b_ab, w_out, peer_wq, peer_k1, peer_k2)
        mod = mods[l]
        n1, n2 = norm1_w[l][None, :], norm2_w[l][None, :]
        gnw = gla_norm_w[l][None, :]
        cw, cbias = conv_w[l], conv_b[l][None, :]
        bias = _na_bias_table(na_rpb[l])
        s0 = (_states_to_blockdiag(state_gla_fwd[:, l]), _states_to_blockdiag(state_gla_bwd[:, l]))

        zg, la, zc, zn = _inproj(xp, mod, prompt_row, n1, inw)
        yg, sf, sb = _gla(zg, la, gnw, bp, sp, None)
        yc = _conv(zc, cw, cbias, bp, sp)
        yn = _dense_attn(zn, bp, sp)
        xp, h2, q = _outproj(xp, yg, yc, yn, mod, prompt_row, n2, wo, wq)
        xp = _experts(h2, u, vt, l, _route(q, kk), xp, mod, prompt_row, fw, last)
        heads = lambda a: a.reshape(bp, sp, NA_W // HD, HD).transpose(0, 2, 1, 3)
        new_k.append(heads(zn[:, NA_W:2 * NA_W]))
        new_v.append(heads(zn[:, 2 * NA_W:]))
        new_sf.append(_blockdiag_to_states(sf))
        new_sb.append(_blockdiag_to_states(sb))

        zg, la, zc, zn = _inproj(xs, mod, sample_row(TM), n1, inw)
        yg = _gla(zg, la, gnw, bs, ss, s0)
        yc = _conv(zc, cw, cbias, bs, ss)
        yn = _neighbourhood_attn(zn, kctx[:, l:l + 1], vctx[:, l:l + 1], bias, bs, ss)
        xs, h2, q = _outproj(xs, yg, yc, yn, mod, sample_row(TM), n2, wo, wq)
        xs = _experts(h2, u, vt, l, _route(q, kk), xs, mod, sample_row(TT), fw, last)

    y_prompt = xp.reshape(bp, sp, D)
    y_sample = xs.reshape(bs, ss, D)
    return (y_prompt, y_sample, jnp.stack(new_k, axis=1), jnp.stack(new_v, axis=1),
            jnp.stack(new_sf, axis=1), jnp.stack(new_sb, axis=1))
```

```python
import functools

import numpy as np
import jax
import jax.numpy as jnp
from jax import lax
from jax.experimental import pallas as pl
from jax.experimental.pallas import tpu as pltpu

F32 = jnp.float32
BF16 = jnp.bfloat16
HI = lax.Precision.HIGHEST
NT = (((1,), (1,)), ((), ()))
TN = (((0,), (0,)), ((), ()))

LANES = 128
SUBLANES = 8
BF16_ROWS = 16
EPS = 1e-6
D = 1024
HD = 64
PAIR = 2 * HD
GLA_W = 384
CONV_W = 256
NA_W = 384
LOWRANK = 16
GLA_CHUNK = 64
GLA_SUB = 16
GLA_OUT_ROWS = 256
GLA_GROUP = 8
GLA_ROWS_PER_STEP = 1024
GLA_TAU = 16.0
EXP_CLAMP = 80.0
GRID_W = 64
NA_ROWS = 8
NA_COLS = 16
CTX_SEQS_PER_STEP = 4
NA_ROWS_PER_STEP = 4
NA_BLOCKS_PER_STEP = 4
PEER_HEADS = 8
NKEYS = 128
TOPK = 16
TM = 512
CUM_ROWS = 256
TT = 512
EXPERT_BLOCK = 2048
EXPERT_SUB = 512
VALUE_CHUNK = 2048
ROUTE_HEADS_PER_STEP = 8
ROUTE_TM = 128
VMEM_LIMIT = 56 * 1024 * 1024


def _cparams(*sem):
    return pltpu.CompilerParams(dimension_semantics=sem, vmem_limit_bytes=VMEM_LIMIT)


def _silu(x):
    return x * jax.nn.sigmoid(x)


def _twice_gelu_tanh(x):
    c0 = float(np.sqrt(2.0 / np.pi))
    z = x * (c0 + (0.044715 * c0) * (x * x))
    return x + x * jnp.tanh(z)


def _norm_mod(x, w, scale, shift):
    ms = jnp.mean(x * x, axis=-1, keepdims=True)
    return x * lax.rsqrt(ms + EPS) * w * (1.0 + scale) + shift


def _head_masks():
    lane = lax.broadcasted_iota(jnp.int32, (1, PAIR), 1)
    m0 = (lane < HD).astype(F32)
    return m0, 1.0 - m0


def _mod_kernel(c_ref, w_ref, b_ref, o_ref):
    s = _silu(c_ref[...]).astype(BF16)
    o_ref[0] = jnp.dot(s, w_ref[0].astype(BF16), preferred_element_type=F32) + b_ref[0]


def _modulation(cvec, w_ada, b_ada):
    depth = w_ada.shape[0]
    nb = w_ada.shape[2] // D
    out = pl.pallas_call(
        _mod_kernel,
        out_shape=jax.ShapeDtypeStruct((depth, 8, nb * D), F32),
        grid=(depth, nb),
        in_specs=[pl.BlockSpec((8, D), lambda l, j: (0, 0)),
                  pl.BlockSpec((1, D, D), lambda l, j: (l, 0, j)),
                  pl.BlockSpec((1, 1, D), lambda l, j: (l, 0, j))],
        out_specs=pl.BlockSpec((1, 8, D), lambda l, j: (l, 0, j)),
        compiler_params=_cparams("arbitrary", "arbitrary"),
        name="adaln_modulation",
    )(cvec, w_ada, b_ada.reshape(depth, 1, nb * D))
    return out.reshape(depth, 8, nb, D)


def _inproj_kernel(x_ref, mod_ref, nw_ref, wg_ref, wc_ref, wn_ref, wl_ref, wab_ref, bab_ref,
                   zg_ref, la_ref, zc_ref, zn_ref):
    h = _norm_mod(x_ref[...], nw_ref[...], mod_ref[0, 1:2, :], mod_ref[0, 0:1, :])
    hb = h.astype(BF16)
    zg_ref[...] = jnp.dot(hb, wg_ref[...], preferred_element_type=F32)
    zc_ref[...] = jnp.dot(hb, wc_ref[...], preferred_element_type=F32)
    zn_ref[...] = jnp.dot(hb, wn_ref[...], preferred_element_type=F32)
    lr = jnp.dot(hb, wl_ref[...], preferred_element_type=F32)
    hi = lr.astype(BF16)
    rest = lr - hi.astype(F32)
    mid = rest.astype(BF16)
    lo = (rest - mid.astype(F32)).astype(BF16)
    grp = lax.broadcasted_iota(jnp.int32, (1, lr.shape[1]), 1) >> ((2 * LOWRANK).bit_length() - 1)
    pieces = jnp.where((grp == 2) | (grp == 5), mid, jnp.where(grp == 4, lo, hi))
    zz = jnp.dot(pieces, wab_ref[...], preferred_element_type=F32) + bab_ref[...]
    la = (jnp.minimum(zz, 0.0) - jnp.log(1.0 + jnp.exp(-jnp.abs(zz)))) * (1.0 / GLA_TAU)
    r = lax.broadcasted_iota(jnp.int32, (CUM_ROWS, CUM_ROWS), 0)
    s = lax.broadcasted_iota(jnp.int32, (CUM_ROWS, CUM_ROWS), 1)
    shift = GLA_CHUNK.bit_length() - 1
    same = (r >> shift) == (s >> shift)
    tri_f = (same & (s <= r)).astype(BF16)
    tri_b = (same & (s >= r)).astype(BF16)
    hi = la.astype(BF16)
    rest = la - hi.astype(F32)
    mid = rest.astype(BF16)
    lo = (rest - mid.astype(F32)).astype(BF16)
    for blk in range(TM // CUM_ROWS):
        rows = slice(blk * CUM_ROWS, (blk + 1) * CUM_ROWS)
        for tri, cols in ((tri_f, slice(0, GLA_W)), (tri_b, slice(GLA_W, 2 * GLA_W))):
            la_ref[rows, cols] = (jnp.dot(tri, hi[rows, cols], preferred_element_type=F32)
                                  + jnp.dot(tri, mid[rows, cols], preferred_element_type=F32)
                                  + jnp.dot(tri, lo[rows, cols], preferred_element_type=F32))


def _const_spec(shape):
    return pl.BlockSpec(shape, lambda *_: (0,) * len(shape))


def _inproj(x, mod, mod_row, nw, wts):
    n = x.shape[0]
    wg, wc, wn, wl, wab, bab = wts
    row = lambda w: pl.BlockSpec((TM, w), lambda i: (i, 0))
    return pl.pallas_call(
        _inproj_kernel,
        out_shape=(jax.ShapeDtypeStruct((n, 4 * GLA_W), F32), jax.ShapeDtypeStruct((n, 2 * GLA_W), F32),
                   jax.ShapeDtypeStruct((n, 3 * CONV_W), F32), jax.ShapeDtypeStruct((n, 3 * NA_W), F32)),
        grid=(n // TM,),
        in_specs=[row(D), pl.BlockSpec((1, 6, D), lambda i: (mod_row(i), 0, 0)), _const_spec((1, D)),
                  _const_spec(wg.shape), _const_spec(wc.shape), _const_spec(wn.shape), _const_spec(wl.shape),
                  _const_spec(wab.shape), _const_spec(bab.shape)],
        out_specs=(row(4 * GLA_W), row(2 * GLA_W), row(3 * CONV_W), row(3 * NA_W)),
        compiler_params=_cparams("arbitrary"),
        name="in_projection",
    )(x, mod, nw, wg, wc, wn, wl, wab, bab)


def _gla_consts(fwd):
    c, sb = GLA_CHUNK, GLA_SUB
    r = lax.broadcasted_iota(jnp.int32, (2 * c, c), 0) & (c - 1)
    s = lax.broadcasted_iota(jnp.int32, (2 * c, c), 1)
    caus = (s <= r) if fwd else (s >= r)
    rowid = lax.broadcasted_iota(jnp.int32, (c, PAIR), 0)
    seen = [(rowid < (i + 1) * sb) if fwd else (rowid >= i * sb) for i in range(c // sb)]
    m0, m1 = _head_masks()
    qsel = [[((rowid >= i * sb) & (rowid < (i + 1) * sb)).astype(F32) * m for i in range(c // sb)]
            for m in (m0, m1)]
    return caus, seen, qsel


def _gla_operands(qc, kc, vc, cum, fwd, consts):
    _, seen, qsel = consts
    c, sb = GLA_CHUNK, GLA_SUB
    nsb = c // sb
    zero_row = jnp.zeros((1, PAIR), F32)
    if fwd:
        last = cum[c - 1:c]
        starts = [zero_row] + [cum[i * sb - 1:i * sb] for i in range(1, nsb)]
    else:
        last = cum[0:1]
        starts = [cum[(i + 1) * sb:(i + 1) * sb + 1] for i in range(nsb - 1)] + [zero_row]
    bm = jnp.concatenate([jnp.broadcast_to(b, (sb, PAIR)) for b in starts], axis=0)
    qt = qc * jnp.exp(cum - bm)
    kbig = jnp.concatenate(
        [(kc * jnp.exp(jnp.where(seen[i], jnp.minimum(starts[i] - cum, EXP_CLAMP), 0.0))).astype(BF16)
         for i in range(nsb)], axis=1)
    qbig = jnp.concatenate(
        [jnp.concatenate([(qt * qsel[h][i]).astype(BF16) for i in range(nsb)], axis=1) for h in range(2)],
        axis=0)
    return dict(qbig=qbig, kbig=kbig, vb=vc.astype(BF16), qdec=(qc * jnp.exp(cum)).astype(BF16),
                khat=(kc * jnp.exp(last - cum)).astype(BF16), decay=jnp.exp(last))


def _gla_group(chunks, st, consts, masks, bd):
    caus = consts[0]
    m0, m1 = masks
    c = GLA_CHUNK
    atts = [lax.dot_general(ch['qbig'], ch['kbig'], NT, preferred_element_type=F32) for ch in chunks]
    upds = [lax.dot_general(ch['vb'], ch['khat'], TN, preferred_element_type=F32) for ch in chunks]
    atts = [jnp.where(caus, a, 0.0).astype(BF16) for a in atts]
    intras = [jnp.dot(a[:c], ch['vb'], preferred_element_type=F32) * m0
              + jnp.dot(a[c:], ch['vb'], preferred_element_type=F32) * m1 for a, ch in zip(atts, chunks)]
    outs = []
    for ch, upd, intra in zip(chunks, upds, intras):
        outs.append(intra + lax.dot_general(ch['qdec'], st.astype(BF16), NT, preferred_element_type=F32))
        st = st * ch['decay'] + upd * bd
    return outs, st


def _gla_kernel(*refs, seq, nseq, state_in):
    if state_in:
        q_ref, k_ref, v_ref, g_ref, laf_ref, lab_ref, nw_ref, s0f_ref, s0b_ref, y_ref, of_scr, ob_scr = refs
    else:
        q_ref, k_ref, v_ref, g_ref, laf_ref, lab_ref, nw_ref, y_ref, sf_ref, sb_ref, of_scr, ob_scr = refs
    c = GLA_CHUNK
    nc = seq // c
    masks = _head_masks()
    m0, m1 = masks
    rr = lax.broadcasted_iota(jnp.int32, (PAIR, PAIR), 0)
    cc = lax.broadcasted_iota(jnp.int32, (PAIR, PAIR), 1)
    bd = ((rr < HD) == (cc < HD)).astype(F32)
    cf = _gla_consts(True)
    cb = _gla_consts(False)
    scale = HD ** -0.5
    nw = nw_ref[...]

    grp = min(GLA_GROUP, nc)
    zero = jnp.zeros((PAIR, PAIR), F32)

    def scan_sequence(sq):
        def rows(i):
            start = sq * seq + i * c
            return pl.ds(start if isinstance(start, int) else pl.multiple_of(start, c), c)

        def scan_body(j, carry):
            stf, stb = carry
            slf = [rows(j * grp + k) for k in range(grp)]
            slb = [rows(nc - 1 - (j * grp + k)) for k in range(grp)]
            chf = [_gla_operands(q_ref[s, :] * scale, k_ref[s, :], v_ref[s, :], laf_ref[s, :], True, cf)
                   for s in slf]
            chb = [_gla_operands(q_ref[s, :] * scale, k_ref[s, :], v_ref[s, :], lab_ref[s, :], False, cb)
                   for s in slb]
            of, stf = _gla_group(chf, stf, cf, masks, bd)
            ob, stb = _gla_group(chb, stb, cb, masks, bd)
            for s, o in zip(slf, of):
                of_scr[s, :] = o
            for s, o in zip(slb, ob):
                ob_scr[s, :] = o
            return stf, stb

        init = (s0f_ref[sq, 0], s0b_ref[sq, 0]) if state_in else (zero, zero)
        if nc == grp:
            sf, sb = scan_body(0, init)
        else:
            sf, sb = lax.fori_loop(0, nc // grp, scan_body, init)
        if not state_in:
            sf_ref[sq, 0] = sf
            sb_ref[sq, 0] = sb

    for sq in range(nseq):
        scan_sequence(sq)

    def out_body(i, carry):
        sl = pl.ds(pl.multiple_of(i * GLA_OUT_ROWS, GLA_OUT_ROWS), GLA_OUT_ROWS)
        tot = of_scr[sl, :] + ob_scr[sl, :]
        sq = tot * tot
        ms = (jnp.sum(sq * m0, axis=-1, keepdims=True) * m0
              + jnp.sum(sq * m1, axis=-1, keepdims=True) * m1) * (1.0 / HD)
        y_ref[sl, :] = tot * lax.rsqrt(ms + EPS) * nw * _silu(g_ref[sl, :])
        return carry

    lax.fori_loop(0, nseq * seq // GLA_OUT_ROWS, out_body, 0)


def _gla(zg, la, nw, batch, seq, states):
    n = zg.shape[0]
    npair = GLA_W // PAIR
    nseq = max(1, min(batch, GLA_ROWS_PER_STEP // seq))
    assert batch % nseq == 0
    col = lambda off: pl.BlockSpec((nseq * seq, PAIR), lambda b, p: (b, off + p))
    st_spec = pl.BlockSpec((nseq, 1, PAIR, PAIR), lambda b, p: (b, p, 0, 0))
    in_specs = [col(0), col(npair), col(2 * npair), col(3 * npair), col(0), col(npair),
                pl.BlockSpec((1, PAIR), lambda b, p: (0, p))]
    args = [zg, zg, zg, zg, la, la, nw]
    y_shape = jax.ShapeDtypeStruct((n, GLA_W), F32)
    y_spec = pl.BlockSpec((nseq * seq, PAIR), lambda b, p: (b, p))
    if states is None:
        st_shape = jax.ShapeDtypeStruct((batch, npair, PAIR, PAIR), F32)
        out_shape, out_specs = (y_shape, st_shape, st_shape), (y_spec, st_spec, st_spec)
    else:
        in_specs += [st_spec, st_spec]
        args += list(states)
        out_shape, out_specs = y_shape, y_spec
    return pl.pallas_call(
        functools.partial(_gla_kernel, seq=seq, nseq=nseq, state_in=states is not None),
        out_shape=out_shape,
        grid=(batch // nseq, npair),
        in_specs=in_specs,
        out_specs=out_specs,
        scratch_shapes=[pltpu.VMEM((nseq * seq, PAIR), F32), pltpu.VMEM((nseq * seq, PAIR), F32)],
        compiler_params=_cparams("arbitrary", "arbitrary"),
        name="gla_bidir",
    )(*args)


def _states_to_blockdiag(s):
    b, h = s.shape[:2]
    st = jnp.swapaxes(s, -1, -2).reshape(b, h // 2, 2, HD, HD)
    eye = jnp.eye(2, dtype=s.dtype)
    return jnp.einsum('bpivk,ij->bpivjk', st, eye).reshape(b, h // 2, PAIR, PAIR)


def _blockdiag_to_states(sbd):
    b, p = sbd.shape[:2]
    s6 = sbd.reshape(b, p, 2, HD, 2, HD)
    diag = jnp.stack([s6[:, :, 0, :, 0, :], s6[:, :, 1, :, 1, :]], axis=2)
    return jnp.swapaxes(diag, -1, -2).reshape(b, 2 * p, HD, HD)


def _conv_kernel(ch_ref, cb_ref, cc_ref, w_ref, b_ref, y_ref, *, seq):
    u = cc_ref[...] * ch_ref[...]
    row = lax.broadcasted_iota(jnp.int32, u.shape, 0)
    prev = jnp.where(row == 0, 0.0, pltpu.roll(u, 1, 0))
    nxt = jnp.where(row == seq - 1, 0.0, pltpu.roll(u, seq - 1, 0))
    y_ref[...] = cb_ref[...] * (w_ref[0:1, :] * prev + w_ref[1:2, :] * u + w_ref[2:3, :] * nxt + b_ref[...])


def _conv(zc, w, b, batch, seq):
    col = lambda j: pl.BlockSpec((seq, CONV_W), lambda i: (i, j))
    return pl.pallas_call(
        functools.partial(_conv_kernel, seq=seq),
        out_shape=jax.ShapeDtypeStruct((zc.shape[0], CONV_W), F32),
        grid=(batch,),
        in_specs=[col(0), col(1), col(2), _const_spec(w.shape), _const_spec(b.shape)],
        out_specs=col(0),
        compiler_params=_cparams("arbitrary"),
        name="gated_conv",
    )(zc, zc, zc, w, b)


def _dense_attn_kernel(q_ref, k_ref, v_ref, o_ref, *, seq, nseq):
    m0, m1 = _head_masks()
    for b in range(nseq):
        rows = slice(b * seq, (b + 1) * seq)
        q = q_ref[rows, :] * (HD ** -0.5)
        qstack = jnp.concatenate([q * m0, q * m1], axis=0).astype(BF16)
        s = lax.dot_general(qstack, k_ref[rows, :].astype(BF16), NT, preferred_element_type=F32)
        p = jnp.exp(s - jnp.max(s, axis=-1, keepdims=True))
        o = (jnp.dot(p.astype(BF16), v_ref[rows, :].astype(BF16), preferred_element_type=F32)
             / jnp.sum(p, axis=-1, keepdims=True))
        o_ref[rows, :] = o[:seq] * m0 + o[seq:] * m1


def _dense_attn(zn, batch, seq):
    npair = NA_W // PAIR
    nseq = min(CTX_SEQS_PER_STEP, batch)
    assert batch % nseq == 0
    col = lambda off: pl.BlockSpec((nseq * seq, PAIR), lambda b, p: (b, off + p))
    return pl.pallas_call(
        functools.partial(_dense_attn_kernel, seq=seq, nseq=nseq),
        out_shape=jax.ShapeDtypeStruct((zn.shape[0], NA_W), F32),
        grid=(batch // nseq, npair),
        in_specs=[col(0), col(npair), col(2 * npair)],
        out_specs=col(0),
        compiler_params=_cparams("arbitrary", "arbitrary"),
        name="context_attention",
    )(zn, zn, zn)


def _na_kernel(q_ref, k_ref, v_ref, kc_ref, vc_ref, tp_ref, o_ref, *, rows, span):
    rb = NA_ROWS_PER_STEP
    kc = kc_ref[0, 0].astype(BF16)
    vc = vc_ref[0, 0].astype(BF16)
    m0, m1 = _head_masks()
    shift = GRID_W.bit_length() - 1
    keyrow = lax.broadcasted_iota(jnp.int32, (1, span * GRID_W), 1) >> shift

    def block(bi):
        g = pl.program_id(2) * NA_BLOCKS_PER_STEP + bi
        q0 = bi * rb * GRID_W
        us = jnp.clip(g * rb - NA_ROWS // 2, 0, rows - span)
        win = pl.ds(pl.multiple_of(us * GRID_W, GRID_W), span * GRID_W)
        kw = k_ref[win, :].astype(BF16)
        vw = v_ref[win, :].astype(BF16)
        qs = []
        for rr in range(rb):
            q = q_ref[q0 + rr * GRID_W:q0 + (rr + 1) * GRID_W, :] * (HD ** -0.5)
            qs += [q * m0, q * m1]
        qstack = jnp.concatenate(qs, axis=0).astype(BF16)
        sw = lax.dot_general(qstack, kw, NT, preferred_element_type=F32)
        sc = lax.dot_general(qstack, kc, NT, preferred_element_type=F32)
        parts = []
        for rr in range(rb):
            r = g * rb + rr
            lo = jnp.clip(r - NA_ROWS // 2, 0, rows - NA_ROWS) - us
            inside = (keyrow >= lo) & (keyrow < lo + NA_ROWS)
            blk = sw[rr * PAIR:(rr + 1) * PAIR]
            tiles = [blk[:, jp * PAIR:(jp + 1) * PAIR]
                     + tp_ref[0, jnp.clip(us + 2 * jp - r + NA_ROWS, 0, 2 * NA_ROWS - 1)]
                     for jp in range(span // 2)]
            parts.append(jnp.where(inside, jnp.concatenate(tiles, axis=1), -1e30))
        sw = jnp.concatenate(parts, axis=0)
        mx = jnp.maximum(jnp.max(sw, axis=-1, keepdims=True), jnp.max(sc, axis=-1, keepdims=True))
        pw = jnp.exp(sw - mx)
        pc = jnp.exp(sc - mx)
        den = jnp.sum(pw, axis=-1, keepdims=True) + jnp.sum(pc, axis=-1, keepdims=True)
        o = (jnp.dot(pw.astype(BF16), vw, preferred_element_type=F32)
             + jnp.dot(pc.astype(BF16), vc, preferred_element_type=F32)) / den
        for rr in range(rb):
            o_ref[q0 + rr * GRID_W:q0 + (rr + 1) * GRID_W, :] = (o[rr * PAIR:rr * PAIR + HD] * m0
                                                                 + o[rr * PAIR + HD:(rr + 1) * PAIR] * m1)

    for bi in range(NA_BLOCKS_PER_STEP):
        block(bi)


def _na_bias_table(rpb):
    nh = rpb.shape[0]
    cols = np.arange(GRID_W)
    cs = np.clip(cols - NA_COLS // 2, 0, GRID_W - NA_COLS)
    col_mask = (cols[None, :] >= cs[:, None]) & (cols[None, :] < cs[:, None] + NA_COLS)
    dc = np.clip(cols[None, :] - cols[:, None], -(NA_COLS - 1), NA_COLS - 1) + NA_COLS - 1
    onehot = (dc[:, :, None] == np.arange(2 * NA_COLS - 1)).astype(np.float32)
    toep = jnp.einsum('hab,qkb->haqk', rpb.astype(F32), onehot, precision=HI)
    toep = jnp.where(col_mask[None, None], toep, -1e30)
    ext = jnp.pad(toep, ((0, 0), (1, 1), (0, 0), (0, 0)), constant_values=-1e30)
    two = jnp.concatenate([ext[:, :-1], ext[:, 1:]], axis=-1)
    two = two.reshape(nh // 2, 2, 2 * NA_ROWS, GRID_W, 2 * GRID_W).transpose(0, 2, 1, 3, 4)
    return two.reshape(nh // 2, 2 * NA_ROWS, PAIR, PAIR)


def _neighbourhood_attn(zn, kctx, vctx, bias, batch, seq):
    npair = NA_W // PAIR
    rows = seq // GRID_W
    ctx = kctx.shape[2]
    rb = NA_ROWS_PER_STEP
    nblk = NA_BLOCKS_PER_STEP
    assert rows % (rb * nblk) == 0 and rows >= NA_ROWS
    steps = rows // (rb * nblk)
    span = min(rows, NA_ROWS + rb)
    qcol = pl.BlockSpec((nblk * rb * GRID_W, PAIR), lambda p, b, r: (b * steps + r, p))
    seqcol = lambda off: pl.BlockSpec((seq, PAIR), lambda p, b, r: (b, off + p))
    ctxcol = pl.BlockSpec((1, 1, ctx, PAIR), lambda p, b, r: (b, 0, 0, p))
    bias_spec = pl.BlockSpec((1, 2 * NA_ROWS, PAIR, PAIR), lambda p, b, r: (p, 0, 0, 0))
    return pl.pallas_call(
        functools.partial(_na_kernel, rows=rows, span=span),
        out_shape=jax.ShapeDtypeStruct((zn.shape[0], NA_W), F32),
        grid=(npair, batch, steps),
        in_specs=[qcol, seqcol(npair), seqcol(2 * npair), ctxcol, ctxcol, bias_spec],
        out_specs=qcol,
        compiler_params=_cparams("arbitrary", "arbitrary", "arbitrary"),
        name="neighbourhood_attention",
    )(zn, zn, zn, kctx, vctx, bias)


def _outproj_kernel(x_ref, yg_ref, yc_ref, yn_ref, mod_ref, nw_ref, wg_ref, wc_ref, wn_ref, wq_ref,
                    xo_ref, h_ref, q_ref):
    y = (jnp.dot(yg_ref[...].astype(BF16), wg_ref[...], preferred_element_type=F32)
         + jnp.dot(yc_ref[...].astype(BF16), wc_ref[...], preferred_element_type=F32)
         + jnp.dot(yn_ref[...].astype(BF16), wn_ref[...], preferred_element_type=F32))
    x = x_ref[...] + mod_ref[0, 2:3, :] * y
    xo_ref[...] = x
    hb = _norm_mod(x, nw_ref[...], mod_ref[0, 4:5, :], mod_ref[0, 3:4, :]).astype(BF16)
    h_ref[...] = hb
    q_ref[...] = jnp.dot(hb, wq_ref[...], preferred_element_type=F32)


def _outproj(x, yg, yc, yn, mod, mod_row, nw, wts, wq):
    n = x.shape[0]
    wg, wc, wn = wts
    row = lambda w: pl.BlockSpec((TM, w), lambda i: (i, 0))
    return pl.pallas_call(
        _outproj_kernel,
        out_shape=(jax.ShapeDtypeStruct((n, D), F32), jax.ShapeDtypeStruct((n, D), BF16),
                   jax.ShapeDtypeStruct((n, wq.shape[1]), F32)),
        grid=(n // TM,),
        in_specs=[row(D), row(GLA_W), row(CONV_W), row(NA_W),
                  pl.BlockSpec((1, 6, D), lambda i: (mod_row(i), 0, 0)), _const_spec((1, D)),
                  _const_spec(wg.shape), _const_spec(wc.shape), _const_spec(wn.shape), _const_spec(wq.shape)],
        out_specs=(row(D), row(D), row(wq.shape[1])),
        compiler_params=_cparams("arbitrary"),
        name="out_projection",
    )(x, yg, yc, yn, mod, nw, wg, wc, wn, wq)


def _count_prefix(pred, top, n):
    def pick(bits, lo, width):
        if not bits:
            return top[lo + width // 2 - 1]
        return jnp.where(bits[0], pick(bits[1:], lo + width // 2, width // 2),
                         pick(bits[1:], lo, width // 2))

    bits = []
    width = n
    while width > 1:
        bits.append(pred(pick(bits, 0, n)))
        width //= 2
    count = sum(jnp.where(b, float(n >> (i + 1)), 0.0) for i, b in enumerate(bits))
    return jnp.where(pred(top[n - 1]), float(n), count)


def _sort16_network():
    def merge(lo, hi, r):
        step = r * 2
        if step < hi - lo:
            yield from merge(lo, hi, step)
            yield from merge(lo + r, hi, step)
            yield from [(i, i + r) for i in range(lo + r, hi - r, step)]
        else:
            yield (lo, lo + r)

    def sort(lo, hi):
        if hi - lo >= 1:
            mid = lo + (hi - lo) // 2
            yield from sort(lo, mid)
            yield from sort(mid + 1, hi)
            yield from merge(lo, hi, 1)

    return tuple(sort(0, TOPK - 1))


_SORT16 = _sort16_network()
_BITONIC16 = tuple((i, i + d) for d in (8, 4, 2, 1) for i in range(TOPK) if not i & d)


def _exchange(v, pairs):
    v = list(v)
    for i, j in pairs:
        a, b = v[i], v[j]
        if b is None:
            continue
        if a is None:
            v[i], v[j] = b, None
        else:
            v[i], v[j] = jnp.maximum(a, b), jnp.minimum(a, b)
    return v


def _top16_sorted(tiles):
    v = _exchange(list(tiles) + [None] * (TOPK - len(tiles)), _SORT16)
    for shift in (4, 2, 1):
        other = [None if t is None else pltpu.roll(t, shift, 0) for t in v]
        merged = []
        for k in range(TOPK):
            a, b = v[k], other[TOPK - 1 - k]
            merged.append(b if a is None else a if b is None else jnp.maximum(a, b))
        v = _exchange(merged, _BITONIC16)
    return v


def _route_kernel(q_ref, kk_ref, r1_ref, e1_ref, c2_ref, e2_ref, v1_scr, v2_scr):
    sub = SUBLANES
    for h in range(ROUTE_HEADS_PER_STEP):
        qh = q_ref[:, h * LANES:(h + 1) * LANES]
        st = lax.dot_general(kk_ref[h], qh, NT, precision=HI, preferred_element_type=F32)
        s1 = st[:NKEYS]
        s2 = st[NKEYS:]
        tiles1 = [s1[g * sub:(g + 1) * sub] for g in range(NKEYS // sub)]
        top1 = _top16_sorted(tiles1)
        tiles2 = [s2[g * sub:(g + 1) * sub] for g in range(NKEYS // sub)]
        top2 = _top16_sorted(tiles2)
        for k in range(TOPK):
            v1_scr[k:k + 1, :] = top1[k][0:1]
            v2_scr[k:k + 1, :] = top2[k][0:1]
        r1_ref[h] = jnp.concatenate([_count_prefix(lambda p, t=t: p > t, top1, TOPK) for t in tiles1], axis=0)
        v1 = v1_scr[...]
        v2 = v2_scr[...]
        half = TOPK // 2
        cand = ([v1[0:1] + v2[:half], v1[0:1] + v2[half:]] + [v1[a:a + 1] + v2[:half] for a in range(1, half)]
                + [v1[half:] + v2[0:1]])
        top = _top16_sorted(cand)
        z = sum(jnp.exp(t - top[0]) for t in top)[0:1]
        tau = top[TOPK - 1]
        tail = sum(jnp.where(top1[a] + top2[0] >= tau, 1.0, 0.0) for a in range(half, TOPK))
        count = jnp.concatenate(
            [_count_prefix(lambda p, t=t: p + t >= tau, top1, half) + jnp.where(t == top2[0], tail, 0.0)
             for t in tiles2], axis=0)
        e1_ref[h] = jnp.exp(s1 - v1[0:1])
        c2_ref[h] = count.astype(BF16)
        e2_ref[h] = (jnp.exp(s2 - v2[0:1]) * (0.5 / z)).astype(BF16)


def _route(q, kk):
    n = q.shape[0]
    hps = ROUTE_HEADS_PER_STEP
    tok = pl.BlockSpec((hps, NKEYS, ROUTE_TM), lambda i, g: (g, 0, i))
    rows = jax.ShapeDtypeStruct((PEER_HEADS, NKEYS, n), F32)
    cols = jax.ShapeDtypeStruct((PEER_HEADS, NKEYS, n), BF16)
    top = pltpu.VMEM((TOPK, ROUTE_TM), F32)
    return pl.pallas_call(
        _route_kernel,
        out_shape=(rows, rows, cols, cols),
        grid=(n // ROUTE_TM, PEER_HEADS // hps),
        in_specs=[pl.BlockSpec((ROUTE_TM, hps * LANES), lambda i, g: (i, g)),
                  pl.BlockSpec((hps, 2 * NKEYS, LANES), lambda i, g: (g, 0, 0))],
        out_specs=(tok, tok, tok, tok),
        scratch_shapes=[top, top],
        compiler_params=_cparams("arbitrary", "arbitrary"),
        name="peer_routing",
    )(q, kk)


def _expert_kernel(h_ref, u_ref, vt_ref, r1_ref, e1_ref, c2_ref, e2_ref, x_ref, mod_ref, fw_ref, o_ref, acc, w_scr,
                   *, final_norm):
    eb = pl.program_id(1)

    @pl.when(eb == 0)
    def _():
        acc[...] = jnp.zeros_like(acc)

    hb = h_ref[...]
    pack = BF16_ROWS
    keys_per_sub = EXPERT_SUB // NKEYS
    nsub = EXPERT_BLOCK // EXPERT_SUB

    def pre_act(sub):
        rows = slice(sub * EXPERT_SUB, (sub + 1) * EXPERT_SUB)
        return lax.dot_general(u_ref[0, rows, :], hb, NT, preferred_element_type=F32).astype(BF16)

    def gated(sub, pre):
        act = _twice_gelu_tanh(pre)
        tile = (NKEYS // pack, pack, TT)
        key0 = eb * (EXPERT_BLOCK // NKEYS) + sub * keys_per_sub
        for ii in range(keys_per_sub):
            gate = jnp.zeros(tile, BF16)
            for h in range(PEER_HEADS):
                r1 = jnp.broadcast_to(r1_ref[h, pl.ds(key0 + ii, 1), :], (pack, TT)).astype(BF16)
                e1 = jnp.broadcast_to(e1_ref[h, pl.ds(key0 + ii, 1), :], (pack, TT)).astype(BF16)
                keep = c2_ref[h].reshape(tile) > r1[None]
                gate = gate + jnp.where(keep, e1[None], 0.0) * e2_ref[h].reshape(tile)
            rows = slice(ii * NKEYS, (ii + 1) * NKEYS)
            w_scr[sub * EXPERT_SUB + ii * NKEYS:sub * EXPERT_SUB + (ii + 1) * NKEYS, :] = (
                gate.reshape(NKEYS, TT) * act[rows])

    per_chunk = VALUE_CHUNK // EXPERT_SUB
    pre = pre_act(0)
    out = None
    for sub in range(nsub):
        nxt = pre_act(sub + 1) if sub + 1 < nsub else None
        gated(sub, pre)
        if (sub + 1) % per_chunk == 0:
            ch = sub // per_chunk
            part = jnp.dot(vt_ref[0, ch], w_scr[ch * VALUE_CHUNK:(ch + 1) * VALUE_CHUNK, :],
                           preferred_element_type=F32)
            out = part if out is None else out + part
        pre = nxt
    acc[...] += out

    @pl.when(eb == pl.num_programs(1) - 1)
    def _():
        x = x_ref[...] + mod_ref[0, 5:6, :] * acc[...].T
        if final_norm:
            x = x * lax.rsqrt(jnp.mean(x * x, axis=-1, keepdims=True) + EPS) * fw_ref[...]
        o_ref[...] = x


def _experts(h2, u, vt, layer, route, x, mod, mod_row, final_w, final_norm):
    n = h2.shape[0]
    tok = pl.BlockSpec((PEER_HEADS, NKEYS, TT), lambda i, e: (0, 0, i))
    return pl.pallas_call(
        functools.partial(_expert_kernel, final_norm=final_norm),
        out_shape=jax.ShapeDtypeStruct((n, D), F32),
        grid=(n // TT, u.shape[1] // EXPERT_BLOCK),
        in_specs=[pl.BlockSpec((TT, D), lambda i, e: (i, 0)),
                  pl.BlockSpec((1, EXPERT_BLOCK, D), lambda i, e: (layer, e, 0)),
                  pl.BlockSpec((1, EXPERT_BLOCK // VALUE_CHUNK, D, VALUE_CHUNK), lambda i, e: (layer, e, 0, 0)),
                  tok, tok, tok, tok,
                  pl.BlockSpec((TT, D), lambda i, e: (i, 0)),
                  pl.BlockSpec((1, 6, D), lambda i, e: (mod_row(i), 0, 0)),
                  pl.BlockSpec((1, D), lambda i, e: (0, 0))],
        out_specs=pl.BlockSpec((TT, D), lambda i, e: (i, 0)),
        scratch_shapes=[pltpu.VMEM((D, TT), F32),
                        pltpu.VMEM((EXPERT_BLOCK, TT), BF16)],
        compiler_params=_cparams("arbitrary", "arbitrary"),
        name="peer_experts",
    )(h2, u, vt, *route, x, mod, final_w)


def _vt_kernel(v_ref, o_ref):
    o_ref[0, 0] = v_ref[0].T.astype(BF16)


def _transposed_values(peer_v):
    depth, ne, _ = peer_v.shape
    return pl.pallas_call(
        _vt_kernel,
        out_shape=jax.ShapeDtypeStruct((depth, ne // VALUE_CHUNK, D, VALUE_CHUNK), BF16),
        grid=(depth, ne // VALUE_CHUNK),
        in_specs=[pl.BlockSpec((1, VALUE_CHUNK, D), lambda l, c: (l, c, 0))],
        out_specs=pl.BlockSpec((1, 1, D, VALUE_CHUNK), lambda l, c: (l, c, 0, 0)),
        compiler_params=_cparams("arbitrary", "arbitrary"),
        name="expert_value_layout",
    )(peer_v)


def _layer_weights(l, w_in, w_af, b_af, w_ab, b_ab, w_out, peer_wq, peer_k1, peer_k2):
    o_lr = 4 * GLA_W
    o_conv = o_lr + 2 * LOWRANK
    o_na = o_conv + 3 * CONV_W
    wi = w_in[l]
    wg = wi[:, :o_lr].astype(BF16)
    wl = jnp.tile(wi[:, o_lr:o_conv], (1, 8)).astype(BF16)
    wc = wi[:, o_conv:o_na].astype(BF16)
    wn = wi[:, o_na:].astype(BF16)
    wa = jnp.zeros((2 * LOWRANK, 2 * GLA_W), F32)
    wa = wa.at[:LOWRANK, :GLA_W].set(w_af[l]).at[LOWRANK:, GLA_W:].set(w_ab[l])
    a_hi = wa.astype(BF16)
    a_rest = wa - a_hi.astype(F32)
    a_mid = a_rest.astype(BF16)
    a_lo = (a_rest - a_mid.astype(F32)).astype(BF16)
    zero = jnp.zeros_like(a_hi)
    wab = jnp.concatenate([a_hi, a_mid, a_hi, a_lo, a_hi, a_mid, zero, zero], axis=0)
    bab = jnp.concatenate([b_af[l], b_ab[l]])[None, :]
    wo = w_out[l].astype(BF16)
    wo = (wo[:GLA_W], wo[GLA_W:GLA_W + CONV_W], wo[GLA_W + CONV_W:])
    half = peer_k1.shape[-1]
    kk = jnp.concatenate([jnp.pad(peer_k1[l], ((0, 0), (0, 0), (0, half))),
                          jnp.pad(peer_k2[l], ((0, 0), (0, 0), (half, 0)))], axis=1)
    return (wg, wc, wn, wl, wab, bab), wo, peer_wq[l].astype(BF16), kk


def kernel(x_prompt, x_sample, cache_na_k, cache_na_v, state_gla_fwd, state_gla_bwd, c, c_ctx, w_ada, b_ada, norm1_w, norm2_w, w_in, w_af, b_af, w_ab, b_ab, gla_norm_w, conv_w, conv_b, na_rpb, w_out, peer_wq, peer_k1, peer_k2, peer_u, peer_v, final_norm_w):
    bp, sp, _ = x_prompt.shape
    bs, ss, _ = x_sample.shape
    depth = w_ada.shape[0]
    xp = x_prompt.reshape(bp * sp, D)
    xs = x_sample.reshape(bs * ss, D)

    cvec = jnp.zeros((8, D), F32).at[0].set(c_ctx).at[1:1 + bs].set(c)
    mods = _modulation(cvec, w_ada, b_ada)
    kctx = jnp.swapaxes(cache_na_k, 2, 3).reshape(bs, depth, -1, NA_W)
    vctx = jnp.swapaxes(cache_na_v, 2, 3).reshape(bs, depth, -1, NA_W)

    prompt_row = lambda i: 0
    sample_row = lambda tile: (lambda i: 1 + i // (ss // tile))

    vt = _transposed_values(peer_v)
    u = peer_u.astype(BF16)
    fw = final_norm_w[None, :]
    new_k, new_v, new_sf, new_sb = [], [], [], []
    for l in range(depth):
        last = l == depth - 1
        inw, wo, wq, kk = _layer_weights(l, w_in, w_af, b_af, w_ab, b_ab, w_out, peer_wq, peer_k1, peer_k2)
        mod = mods[l]
        n1, n2 = norm1_w[l][None, :], norm2_w[l][None, :]
        gnw = gla_norm_w[l][None, :]
        cw, cbias = conv_w[l], conv_b[l][None, :]
        bias = _na_bias_table(na_rpb[l])
        s0 = (_states_to_blockdiag(state_gla_fwd[:, l]), _states_to_blockdiag(state_gla_bwd[:, l]))

        zg, la, zc, zn = _inproj(xp, mod, prompt_row, n1, inw)
        yg, sf, sb = _gla(zg, la, gnw, bp, sp, None)
        yc = _conv(zc, cw, cbias, bp, sp)
        yn = _dense_attn(zn, bp, sp)
        xp, h2, q = _outproj(xp, yg, yc, yn, mod, prompt_row, n2, wo, wq)
        xp = _experts(h2, u, vt, l, _route(q, kk), xp, mod, prompt_row, fw, last)
        heads = lambda a: a.reshape(bp, sp, NA_W // HD, HD).transpose(0, 2, 1, 3)
        new_k.append(heads(zn[:, NA_W:2 * NA_W]))
        new_v.append(heads(zn[:, 2 * NA_W:]))
        new_sf.append(_blockdiag_to_states(sf))
        new_sb.append(_blockdiag_to_states(sb))

        zg, la, zc, zn = _inproj(xs, mod, sample_row(TM), n1, inw)
        yg = _gla(zg, la, gnw, bs, ss, s0)
        yc = _conv(zc, cw, cbias, bs, ss)
        yn = _neighbourhood_attn(zn, kctx[:, l:l + 1], vctx[:, l:l + 1], bias, bs, ss)
        xs, h2, q = _outproj(xs, yg, yc, yn, mod, sample_row(TM), n2, wo, wq)
        xs = _experts(h2, u, vt, l, _route(q, kk), xs, mod, sample_row(TT), fw, last)

    y_prompt = xp.reshape(bp, sp, D)
    y_sample = xs.reshape(bs, ss, D)
    return (y_prompt, y_sample, jnp.stack(new_k, axis=1), jnp.stack(new_v, axis=1),
            jnp.stack(new_sf, axis=1), jnp.stack(new_sb, axis=1))
```

```python
import functools

import numpy as np
import jax
import jax.numpy as jnp
from jax import lax
from jax.experimental import pallas as pl
from jax.experimental.pallas import tpu as pltpu

F32 = jnp.float32
BF16 = jnp.bfloat16
HI = lax.Precision.HIGHEST
NT = (((1,), (1,)), ((), ()))
TN = (((0,), (0,)), ((), ()))

LANES = 128
SUBLANES = 8
BF16_ROWS = 16
EPS = 1e-6
D = 1024
HD = 64
PAIR = 2 * HD
GLA_W = 384
CONV_W = 256
NA_W = 384
LOWRANK = 16
GLA_CHUNK = 64
GLA_SUB = 16
GLA_OUT_ROWS = 256
GLA_GROUP = 16
GLA_ROWS_PER_STEP = 1024
GLA_TAU = 16.0
EXP_CLAMP = 80.0
GRID_W = 64
NA_ROWS = 8
NA_COLS = 16
CTX_SEQS_PER_STEP = 4
NA_ROWS_PER_STEP = 4
NA_BLOCKS_PER_STEP = 8
PEER_HEADS = 8
NKEYS = 128
TOPK = 16
TM = 512
CUM_ROWS = 256
TT = 512
EXPERT_BLOCK = 2048
EXPERT_SUB = 512
VALUE_CHUNK = 2048
ROUTE_HEADS_PER_STEP = 8
ROUTE_TM = 128
VMEM_LIMIT = 56 * 1024 * 1024


def _cparams(*sem):
    return pltpu.CompilerParams(dimension_semantics=sem, vmem_limit_bytes=VMEM_LIMIT)


def _silu(x):
    return x * jax.nn.sigmoid(x)


def _twice_gelu_tanh(x):
    c0 = float(np.sqrt(2.0 / np.pi))
    z = x * (c0 + (0.044715 * c0) * (x * x))
    return x + x * jnp.tanh(z)


def _norm_mod(x, w, scale, shift):
    ms = jnp.mean(x * x, axis=-1, keepdims=True)
    return x * lax.rsqrt(ms + EPS) * w * (1.0 + scale) + shift


def _head_masks():
    lane = lax.broadcasted_iota(jnp.int32, (1, PAIR), 1)
    m0 = (lane < HD).astype(F32)
    return m0, 1.0 - m0


def _mod_kernel(c_ref, w_ref, b_ref, o_ref):
    s = _silu(c_ref[...]).astype(BF16)
    o_ref[0] = jnp.dot(s, w_ref[0].astype(BF16), preferred_element_type=F32) + b_ref[0]


def _modulation(cvec, w_ada, b_ada):
    depth = w_ada.shape[0]
    nb = w_ada.shape[2] // D
    out = pl.pallas_call(
        _mod_kernel,
        out_shape=jax.ShapeDtypeStruct((depth, 8, nb * D), F32),
        grid=(depth, nb),
        in_specs=[pl.BlockSpec((8, D), lambda l, j: (0, 0)),
                  pl.BlockSpec((1, D, D), lambda l, j: (l, 0, j)),
                  pl.BlockSpec((1, 1, D), lambda l, j: (l, 0, j))],
        out_specs=pl.BlockSpec((1, 8, D), lambda l, j: (l, 0, j)),
        compiler_params=_cparams("arbitrary", "arbitrary"),
        name="adaln_modulation",
    )(cvec, w_ada, b_ada.reshape(depth, 1, nb * D))
    return out.reshape(depth, 8, nb, D)


def _inproj_kernel(x_ref, mod_ref, nw_ref, wg_ref, wc_ref, wn_ref, wl_ref, wab_ref, bab_ref,
                   zg_ref, la_ref, zc_ref, zn_ref):
    h = _norm_mod(x_ref[...], nw_ref[...], mod_ref[0, 1:2, :], mod_ref[0, 0:1, :])
    hb = h.astype(BF16)
    zg_ref[...] = jnp.dot(hb, wg_ref[...], preferred_element_type=F32)
    zc_ref[...] = jnp.dot(hb, wc_ref[...], preferred_element_type=F32)
    zn_ref[...] = jnp.dot(hb, wn_ref[...], preferred_element_type=F32)
    lr = jnp.dot(hb, wl_ref[...], preferred_element_type=F32)
    hi = lr.astype(BF16)
    rest = lr - hi.astype(F32)
    mid = rest.astype(BF16)
    lo = (rest - mid.astype(F32)).astype(BF16)
    grp = lax.broadcasted_iota(jnp.int32, (1, lr.shape[1]), 1) >> ((2 * LOWRANK).bit_length() - 1)
    pieces = jnp.where((grp == 2) | (grp == 5), mid, jnp.where(grp == 4, lo, hi))
    zz = jnp.dot(pieces, wab_ref[...], preferred_element_type=F32) + bab_ref[...]
    la = (jnp.minimum(zz, 0.0) - jnp.log(1.0 + jnp.exp(-jnp.abs(zz)))) * (1.0 / GLA_TAU)
    r = lax.broadcasted_iota(jnp.int32, (CUM_ROWS, CUM_ROWS), 0)
    s = lax.broadcasted_iota(jnp.int32, (CUM_ROWS, CUM_ROWS), 1)
    shift = GLA_CHUNK.bit_length() - 1
    same = (r >> shift) == (s >> shift)
    tri_f = (same & (s <= r)).astype(BF16)
    tri_b = (same & (s >= r)).astype(BF16)
    hi = la.astype(BF16)
    rest = la - hi.astype(F32)
    mid = rest.astype(BF16)
    lo = (rest - mid.astype(F32)).astype(BF16)
    for blk in range(TM // CUM_ROWS):
        rows = slice(blk * CUM_ROWS, (blk + 1) * CUM_ROWS)
        for tri, cols in ((tri_f, slice(0, GLA_W)), (tri_b, slice(GLA_W, 2 * GLA_W))):
            la_ref[rows, cols] = (jnp.dot(tri, hi[rows, cols], preferred_element_type=F32)
                                  + jnp.dot(tri, mid[rows, cols], preferred_element_type=F32)
                                  + jnp.dot(tri, lo[rows, cols], preferred_element_type=F32))


def _const_spec(shape):
    return pl.BlockSpec(shape, lambda *_: (0,) * len(shape))


def _inproj(x, mod, mod_row, nw, wts):
    n = x.shape[0]
    wg, wc, wn, wl, wab, bab = wts
    row = lambda w: pl.BlockSpec((TM, w), lambda i: (i, 0))
    return pl.pallas_call(
        _inproj_kernel,
        out_shape=(jax.ShapeDtypeStruct((n, 4 * GLA_W), F32), jax.ShapeDtypeStruct((n, 2 * GLA_W), F32),
                   jax.ShapeDtypeStruct((n, 3 * CONV_W), F32), jax.ShapeDtypeStruct((n, 3 * NA_W), F32)),
        grid=(n // TM,),
        in_specs=[row(D), pl.BlockSpec((1, 6, D), lambda i: (mod_row(i), 0, 0)), _const_spec((1, D)),
                  _const_spec(wg.shape), _const_spec(wc.shape), _const_spec(wn.shape), _const_spec(wl.shape),
                  _const_spec(wab.shape), _const_spec(bab.shape)],
        out_specs=(row(4 * GLA_W), row(2 * GLA_W), row(3 * CONV_W), row(3 * NA_W)),
        compiler_params=_cparams("arbitrary"),
        name="in_projection",
    )(x, mod, nw, wg, wc, wn, wl, wab, bab)


def _gla_consts(fwd):
    c, sb = GLA_CHUNK, GLA_SUB
    r = lax.broadcasted_iota(jnp.int32, (2 * c, c), 0) & (c - 1)
    s = lax.broadcasted_iota(jnp.int32, (2 * c, c), 1)
    caus = (s <= r) if fwd else (s >= r)
    rowid = lax.broadcasted_iota(jnp.int32, (c, PAIR), 0)
    seen = [(rowid < (i + 1) * sb) if fwd else (rowid >= i * sb) for i in range(c // sb)]
    m0, m1 = _head_masks()
    qsel = [[((rowid >= i * sb) & (rowid < (i + 1) * sb)).astype(F32) * m for i in range(c // sb)]
            for m in (m0, m1)]
    return caus, seen, qsel


def _gla_operands(qc, kc, vc, cum, fwd, consts):
    _, seen, qsel = consts
    c, sb = GLA_CHUNK, GLA_SUB
    nsb = c // sb
    zero_row = jnp.zeros((1, PAIR), F32)
    if fwd:
        last = cum[c - 1:c]
        starts = [zero_row] + [cum[i * sb - 1:i * sb] for i in range(1, nsb)]
    else:
        last = cum[0:1]
        starts = [cum[(i + 1) * sb:(i + 1) * sb + 1] for i in range(nsb - 1)] + [zero_row]
    bm = jnp.concatenate([jnp.broadcast_to(b, (sb, PAIR)) for b in starts], axis=0)
    qt = qc * jnp.exp(cum - bm)
    kbig = jnp.concatenate(
        [(kc * jnp.exp(jnp.where(seen[i], jnp.minimum(starts[i] - cum, EXP_CLAMP), 0.0))).astype(BF16)
         for i in range(nsb)], axis=1)
    qbig = jnp.concatenate(
        [jnp.concatenate([(qt * qsel[h][i]).astype(BF16) for i in range(nsb)], axis=1) for h in range(2)],
        axis=0)
    return dict(qbig=qbig, kbig=kbig, vb=vc.astype(BF16), qdec=(qc * jnp.exp(cum)).astype(BF16),
                khat=(kc * jnp.exp(last - cum)).astype(BF16), decay=jnp.exp(last))


def _gla_group(chunks, st, consts, masks, bd):
    caus = consts[0]
    m0, m1 = masks
    c = GLA_CHUNK
    atts = [lax.dot_general(ch['qbig'], ch['kbig'], NT, preferred_element_type=F32) for ch in chunks]
    upds = [lax.dot_general(ch['vb'], ch['khat'], TN, preferred_element_type=F32) for ch in chunks]
    atts = [jnp.where(caus, a, 0.0).astype(BF16) for a in atts]
    intras = [jnp.dot(a[:c], ch['vb'], preferred_element_type=F32) * m0
              + jnp.dot(a[c:], ch['vb'], preferred_element_type=F32) * m1 for a, ch in zip(atts, chunks)]
    outs = []
    for ch, upd, intra in zip(chunks, upds, intras):
        outs.append(intra + lax.dot_general(ch['qdec'], st.astype(BF16), NT, preferred_element_type=F32))
        st = st * ch['decay'] + upd * bd
    return outs, st


def _gla_kernel(*refs, seq, nseq, state_in):
    if state_in:
        q_ref, k_ref, v_ref, g_ref, laf_ref, lab_ref, nw_ref, s0f_ref, s0b_ref, y_ref, of_scr, ob_scr = refs
    else:
        q_ref, k_ref, v_ref, g_ref, laf_ref, lab_ref, nw_ref, y_ref, sf_ref, sb_ref, of_scr, ob_scr = refs
    c = GLA_CHUNK
    nc = seq // c
    masks = _head_masks()
    m0, m1 = masks
    rr = lax.broadcasted_iota(jnp.int32, (PAIR, PAIR), 0)
    cc = lax.broadcasted_iota(jnp.int32, (PAIR, PAIR), 1)
    bd = ((rr < HD) == (cc < HD)).astype(F32)
    cf = _gla_consts(True)
    cb = _gla_consts(False)
    scale = HD ** -0.5
    nw = nw_ref[...]

    grp = min(GLA_GROUP, nc)
    zero = jnp.zeros((PAIR, PAIR), F32)

    def scan_sequence(sq):
        def rows(i):
            start = sq * seq + i * c
            return pl.ds(start if isinstance(start, int) else pl.multiple_of(start, c), c)

        def scan_body(j, carry):
            stf, stb = carry
            slf = [rows(j * grp + k) for k in range(grp)]
            slb = [rows(nc - 1 - (j * grp + k)) for k in range(grp)]
            chf = [_gla_operands(q_ref[s, :] * scale, k_ref[s, :], v_ref[s, :], laf_ref[s, :], True, cf)
                   for s in slf]
            chb = [_gla_operands(q_ref[s, :] * scale, k_ref[s, :], v_ref[s, :], lab_ref[s, :], False, cb)
                   for s in slb]
            of, stf = _gla_group(chf, stf, cf, masks, bd)
            ob, stb = _gla_group(chb, stb, cb, masks, bd)
            for s, o in zip(slf, of):
                of_scr[s, :] = o
            for s, o in zip(slb, ob):
                ob_scr[s, :] = o
            return stf, stb

        init = (s0f_ref[sq, 0], s0b_ref[sq, 0]) if state_in else (zero, zero)
        if nc == grp:
            sf, sb = scan_body(0, init)
        else:
            sf, sb = lax.fori_loop(0, nc // grp, scan_body, init)
        if not state_in:
            sf_ref[sq, 0] = sf
            sb_ref[sq, 0] = sb

    for sq in range(nseq):
        scan_sequence(sq)

    def out_body(i, carry):
        sl = pl.ds(pl.multiple_of(i * GLA_OUT_ROWS, GLA_OUT_ROWS), GLA_OUT_ROWS)
        tot = of_scr[sl, :] + ob_scr[sl, :]
        sq = tot * tot
        ms = (jnp.sum(sq * m0, axis=-1, keepdims=True) * m0
              + jnp.sum(sq * m1, axis=-1, keepdims=True) * m1) * (1.0 / HD)
        y_ref[sl, :] = tot * lax.rsqrt(ms + EPS) * nw * _silu(g_ref[sl, :])
        return carry

    lax.fori_loop(0, nseq * seq // GLA_OUT_ROWS, out_body, 0)


def _gla(zg, la, nw, batch, seq, states):
    n = zg.shape[0]
    npair = GLA_W // PAIR
    nseq = max(1, min(batch, GLA_ROWS_PER_STEP // seq))
    assert batch % nseq == 0
    col = lambda off: pl.BlockSpec((nseq * seq, PAIR), lambda b, p: (b, off + p))
    st_spec = pl.BlockSpec((nseq, 1, PAIR, PAIR), lambda b, p: (b, p, 0, 0))
    in_specs = [col(0), col(npair), col(2 * npair), col(3 * npair), col(0), col(npair),
                pl.BlockSpec((1, PAIR), lambda b, p: (0, p))]
    args = [zg, zg, zg, zg, la, la, nw]
    y_shape = jax.ShapeDtypeStruct((n, GLA_W), F32)
    y_spec = pl.BlockSpec((nseq * seq, PAIR), lambda b, p: (b, p))
    if states is None:
        st_shape = jax.ShapeDtypeStruct((batch, npair, PAIR, PAIR), F32)
        out_shape, out_specs = (y_shape, st_shape, st_shape), (y_spec, st_spec, st_spec)
    else:
        in_specs += [st_spec, st_spec]
        args += list(states)
        out_shape, out_specs = y_shape, y_spec
    return pl.pallas_call(
        functools.partial(_gla_kernel, seq=seq, nseq=nseq, state_in=states is not None),
        out_shape=out_shape,
        grid=(batch // nseq, npair),
        in_specs=in_specs,
        out_specs=out_specs,
        scratch_shapes=[pltpu.VMEM((nseq * seq, PAIR), F32), pltpu.VMEM((nseq * seq, PAIR), F32)],
        compiler_params=_cparams("arbitrary", "arbitrary"),
        name="gla_bidir",
    )(*args)


def _states_to_blockdiag(s):
    b, h = s.shape[:2]
    st = jnp.swapaxes(s, -1, -2).reshape(b, h // 2, 2, HD, HD)
    eye = jnp.eye(2, dtype=s.dtype)
    return jnp.einsum('bpivk,ij->bpivjk', st, eye).reshape(b, h // 2, PAIR, PAIR)


def _blockdiag_to_states(sbd):
    b, p = sbd.shape[:2]
    s6 = sbd.reshape(b, p, 2, HD, 2, HD)
    diag = jnp.stack([s6[:, :, 0, :, 0, :], s6[:, :, 1, :, 1, :]], axis=2)
    return jnp.swapaxes(diag, -1, -2).reshape(b, 2 * p, HD, HD)


def _conv_kernel(ch_ref, cb_ref, cc_ref, w_ref, b_ref, y_ref, *, seq):
    u = cc_ref[...] * ch_ref[...]
    row = lax.broadcasted_iota(jnp.int32, u.shape, 0)
    prev = jnp.where(row == 0, 0.0, pltpu.roll(u, 1, 0))
    nxt = jnp.where(row == seq - 1, 0.0, pltpu.roll(u, seq - 1, 0))
    y_ref[...] = cb_ref[...] * (w_ref[0:1, :] * prev + w_ref[1:2, :] * u + w_ref[2:3, :] * nxt + b_ref[...])


def _conv(zc, w, b, batch, seq):
    col = lambda j: pl.BlockSpec((seq, CONV_W), lambda i: (i, j))
    return pl.pallas_call(
        functools.partial(_conv_kernel, seq=seq),
        out_shape=jax.ShapeDtypeStruct((zc.shape[0], CONV_W), F32),
        grid=(batch,),
        in_specs=[col(0), col(1), col(2), _const_spec(w.shape), _const_spec(b.shape)],
        out_specs=col(0),
        compiler_params=_cparams("arbitrary"),
        name="gated_conv",
    )(zc, zc, zc, w, b)


def _dense_attn_kernel(q_ref, k_ref, v_ref, o_ref, *, seq, nseq):
    m0, m1 = _head_masks()
    for b in range(nseq):
        rows = slice(b * seq, (b + 1) * seq)
        q = q_ref[rows, :] * (HD ** -0.5)
        qstack = jnp.concatenate([q * m0, q * m1], axis=0).astype(BF16)
        s = lax.dot_general(qstack, k_ref[rows, :].astype(BF16), NT, preferred_element_type=F32)
        p = jnp.exp(s - jnp.max(s, axis=-1, keepdims=True))
        o = (jnp.dot(p.astype(BF16), v_ref[rows, :].astype(BF16), preferred_element_type=F32)
             / jnp.sum(p, axis=-1, keepdims=True))
        o_ref[rows, :] = o[:seq] * m0 + o[seq:] * m1


def _dense_attn(zn, batch, seq):
    npair = NA_W // PAIR
    nseq = min(CTX_SEQS_PER_STEP, batch)
    assert batch % nseq == 0
    col = lambda off: pl.BlockSpec((nseq * seq, PAIR), lambda b, p: (b, off + p))
    return pl.pallas_call(
        functools.partial(_dense_attn_kernel, seq=seq, nseq=nseq),
        out_shape=jax.ShapeDtypeStruct((zn.shape[0], NA_W), F32),
        grid=(batch // nseq, npair),
        in_specs=[col(0), col(npair), col(2 * npair)],
        out_specs=col(0),
        compiler_params=_cparams("arbitrary", "arbitrary"),
        name="context_attention",
    )(zn, zn, zn)


def _na_kernel(q_ref, k_ref, v_ref, kc_ref, vc_ref, tp_ref, o_ref, *, rows, span):
    rb = NA_ROWS_PER_STEP
    kc = kc_ref[0, 0].astype(BF16)
    vc = vc_ref[0, 0].astype(BF16)
    m0, m1 = _head_masks()
    shift = GRID_W.bit_length() - 1
    keyrow = lax.broadcasted_iota(jnp.int32, (1, span * GRID_W), 1) >> shift

    def block(bi):
        g = pl.program_id(2) * NA_BLOCKS_PER_STEP + bi
        q0 = bi * rb * GRID_W
        us = jnp.clip(g * rb - NA_ROWS // 2, 0, rows - span)
        win = pl.ds(pl.multiple_of(us * GRID_W, GRID_W), span * GRID_W)
        kw = k_ref[win, :].astype(BF16)
        vw = v_ref[win, :].astype(BF16)
        qs = []
        for rr in range(rb):
            q = q_ref[q0 + rr * GRID_W:q0 + (rr + 1) * GRID_W, :] * (HD ** -0.5)
            qs += [q * m0, q * m1]
        qstack = jnp.concatenate(qs, axis=0).astype(BF16)
        sw = lax.dot_general(qstack, kw, NT, preferred_element_type=F32)
        sc = lax.dot_general(qstack, kc, NT, preferred_element_type=F32)
        parts = []
        for rr in range(rb):
            r = g * rb + rr
            lo = jnp.clip(r - NA_ROWS // 2, 0, rows - NA_ROWS) - us
            inside = (keyrow >= lo) & (keyrow < lo + NA_ROWS)
            blk = sw[rr * PAIR:(rr + 1) * PAIR]
            tiles = [blk[:, jp * PAIR:(jp + 1) * PAIR]
                     + tp_ref[0, jnp.clip(us + 2 * jp - r + NA_ROWS, 0, 2 * NA_ROWS - 1)]
                     for jp in range(span // 2)]
            parts.append(jnp.where(inside, jnp.concatenate(tiles, axis=1), -1e30))
        sw = jnp.concatenate(parts, axis=0)
        mx = jnp.maximum(jnp.max(sw, axis=-1, keepdims=True), jnp.max(sc, axis=-1, keepdims=True))
        pw = jnp.exp(sw - mx)
        pc = jnp.exp(sc - mx)
        den = jnp.sum(pw, axis=-1, keepdims=True) + jnp.sum(pc, axis=-1, keepdims=True)
        o = (jnp.dot(pw.astype(BF16), vw, preferred_element_type=F32)
             + jnp.dot(pc.astype(BF16), vc, preferred_element_type=F32)) / den
        for rr in range(rb):
            o_ref[q0 + rr * GRID_W:q0 + (rr + 1) * GRID_W, :] = (o[rr * PAIR:rr * PAIR + HD] * m0
                                                                 + o[rr * PAIR + HD:(rr + 1) * PAIR] * m1)

    for bi in range(NA_BLOCKS_PER_STEP):
        block(bi)


def _na_bias_table(rpb):
    nh = rpb.shape[0]
    cols = np.arange(GRID_W)
    cs = np.clip(cols - NA_COLS // 2, 0, GRID_W - NA_COLS)
    col_mask = (cols[None, :] >= cs[:, None]) & (cols[None, :] < cs[:, None] + NA_COLS)
    dc = np.clip(cols[None, :] - cols[:, None], -(NA_COLS - 1), NA_COLS - 1) + NA_COLS - 1
    onehot = (dc[:, :, None] == np.arange(2 * NA_COLS - 1)).astype(np.float32)
    toep = jnp.einsum('hab,qkb->haqk', rpb.astype(F32), onehot, precision=HI)
    toep = jnp.where(col_mask[None, None], toep, -1e30)
    ext = jnp.pad(toep, ((0, 0), (1, 1), (0, 0), (0, 0)), constant_values=-1e30)
    two = jnp.concatenate([ext[:, :-1], ext[:, 1:]], axis=-1)
    two = two.reshape(nh // 2, 2, 2 * NA_ROWS, GRID_W, 2 * GRID_W).transpose(0, 2, 1, 3, 4)
    return two.reshape(nh // 2, 2 * NA_ROWS, PAIR, PAIR)


def _neighbourhood_attn(zn, kctx, vctx, bias, batch, seq):
    npair = NA_W // PAIR
    rows = seq // GRID_W
    ctx = kctx.shape[2]
    rb = NA_ROWS_PER_STEP
    nblk = NA_BLOCKS_PER_STEP
    assert rows % (rb * nblk) == 0 and rows >= NA_ROWS
    steps = rows // (rb * nblk)
    span = min(rows, NA_ROWS + rb)
    qcol = pl.BlockSpec((nblk * rb * GRID_W, PAIR), lambda p, b, r: (b * steps + r, p))
    seqcol = lambda off: pl.BlockSpec((seq, PAIR), lambda p, b, r: (b, off + p))
    ctxcol = pl.BlockSpec((1, 1, ctx, PAIR), lambda p, b, r: (b, 0, 0, p))
    bias_spec = pl.BlockSpec((1, 2 * NA_ROWS, PAIR, PAIR), lambda p, b, r: (p, 0, 0, 0))
    return pl.pallas_call(
        functools.partial(_na_kernel, rows=rows, span=span),
        out_shape=jax.ShapeDtypeStruct((zn.shape[0], NA_W), F32),
        grid=(npair, batch, steps),
        in_specs=[qcol, seqcol(npair), seqcol(2 * npair), ctxcol, ctxcol, bias_spec],
        out_specs=qcol,
        compiler_params=_cparams("arbitrary", "arbitrary", "arbitrary"),
        name="neighbourhood_attention",
    )(zn, zn, zn, kctx, vctx, bias)


def _outproj_kernel(x_ref, yg_ref, yc_ref, yn_ref, mod_ref, nw_ref, wg_ref, wc_ref, wn_ref, wq_ref,
                    xo_ref, h_ref, q_ref):
    y = (jnp.dot(yg_ref[...].astype(BF16), wg_ref[...], preferred_element_type=F32)
         + jnp.dot(yc_ref[...].astype(BF16), wc_ref[...], preferred_element_type=F32)
         + jnp.dot(yn_ref[...].astype(BF16), wn_ref[...], preferred_element_type=F32))
    x = x_ref[...] + mod_ref[0, 2:3, :] * y
    xo_ref[...] = x
    hb = _norm_mod(x, nw_ref[...], mod_ref[0, 4:5, :], mod_ref[0, 3:4, :]).astype(BF16)
    h_ref[...] = hb
    q_ref[...] = jnp.dot(hb, wq_ref[...], preferred_element_type=F32)


def _outproj(x, yg, yc, yn, mod, mod_row, nw, wts, wq):
    n = x.shape[0]
    wg, wc, wn = wts
    row = lambda w: pl.BlockSpec((TM, w), lambda i: (i, 0))
    return pl.pallas_call(
        _outproj_kernel,
        out_shape=(jax.ShapeDtypeStruct((n, D), F32), jax.ShapeDtypeStruct((n, D), BF16),
                   jax.ShapeDtypeStruct((n, wq.shape[1]), F32)),
        grid=(n // TM,),
        in_specs=[row(D), row(GLA_W), row(CONV_W), row(NA_W),
                  pl.BlockSpec((1, 6, D), lambda i: (mod_row(i), 0, 0)), _const_spec((1, D)),
                  _const_spec(wg.shape), _const_spec(wc.shape), _const_spec(wn.shape), _const_spec(wq.shape)],
        out_specs=(row(D), row(D), row(wq.shape[1])),
        compiler_params=_cparams("arbitrary"),
        name="out_projection",
    )(x, yg, yc, yn, mod, nw, wg, wc, wn, wq)


def _count_prefix(pred, top, n):
    def pick(bits, lo, width):
        if not bits:
            return top[lo + width // 2 - 1]
        return jnp.where(bits[0], pick(bits[1:], lo + width // 2, width // 2),
                         pick(bits[1:], lo, width // 2))

    bits = []
    width = n
    while width > 1:
        bits.append(pred(pick(bits, 0, n)))
        width //= 2
    count = sum(jnp.where(b, float(n >> (i + 1)), 0.0) for i, b in enumerate(bits))
    return jnp.where(pred(top[n - 1]), float(n), count)


def _sort16_network():
    def merge(lo, hi, r):
        step = r * 2
        if step < hi - lo:
            yield from merge(lo, hi, step)
            yield from merge(lo + r, hi, step)
            yield from [(i, i + r) for i in range(lo + r, hi - r, step)]
        else:
            yield (lo, lo + r)

    def sort(lo, hi):
        if hi - lo >= 1:
            mid = lo + (hi - lo) // 2
            yield from sort(lo, mid)
            yield from sort(mid + 1, hi)
            yield from merge(lo, hi, 1)

    return tuple(sort(0, TOPK - 1))


_SORT16 = _sort16_network()
_BITONIC16 = tuple((i, i + d) for d in (8, 4, 2, 1) for i in range(TOPK) if not i & d)


def _exchange(v, pairs):
    v = list(v)
    for i, j in pairs:
        a, b = v[i], v[j]
        if b is None:
            continue
        if a is None:
            v[i], v[j] = b, None
        else:
            v[i], v[j] = jnp.maximum(a, b), jnp.minimum(a, b)
    return v


def _top16_sorted(tiles):
    v = _exchange(list(tiles) + [None] * (TOPK - len(tiles)), _SORT16)
    for shift in (4, 2, 1):
        other = [None if t is None else pltpu.roll(t, shift, 0) for t in v]
        merged = []
        for k in range(TOPK):
            a, b = v[k], other[TOPK - 1 - k]
            merged.append(b if a is None else a if b is None else jnp.maximum(a, b))
        v = _exchange(merged, _BITONIC16)
    return v


def _route_kernel(q_ref, kk_ref, r1_ref, e1_ref, c2_ref, e2_ref, v1_scr, v2_scr):
    sub = SUBLANES
    for h in range(ROUTE_HEADS_PER_STEP):
        qh = q_ref[:, h * LANES:(h + 1) * LANES]
        st = lax.dot_general(kk_ref[h], qh, NT, precision=HI, preferred_element_type=F32)
        s1 = st[:NKEYS]
        s2 = st[NKEYS:]
        tiles1 = [s1[g * sub:(g + 1) * sub] for g in range(NKEYS // sub)]
        top1 = _top16_sorted(tiles1)
        tiles2 = [s2[g * sub:(g + 1) * sub] for g in range(NKEYS // sub)]
        top2 = _top16_sorted(tiles2)
        for k in range(TOPK):
            v1_scr[k:k + 1, :] = top1[k][0:1]
            v2_scr[k:k + 1, :] = top2[k][0:1]
        r1_ref[h] = jnp.concatenate([_count_prefix(lambda p, t=t: p > t, top1, TOPK) for t in tiles1], axis=0)
        v1 = v1_scr[...]
        v2 = v2_scr[...]
        half = TOPK // 2
        cand = ([v1[0:1] + v2[:half], v1[0:1] + v2[half:]] + [v1[a:a + 1] + v2[:half] for a in range(1, half)]
                + [v1[half:] + v2[0:1]])
        top = _top16_sorted(cand)
        z = sum(jnp.exp(t - top[0]) for t in top)[0:1]
        tau = top[TOPK - 1]
        tail = sum(jnp.where(top1[a] + top2[0] >= tau, 1.0, 0.0) for a in range(half, TOPK))
        count = jnp.concatenate(
            [_count_prefix(lambda p, t=t: p + t >= tau, top1, half) + jnp.where(t == top2[0], tail, 0.0)
             for t in tiles2], axis=0)
        e1_ref[h] = jnp.exp(s1 - v1[0:1])
        c2_ref[h] = count.astype(BF16)
        e2_ref[h] = (jnp.exp(s2 - v2[0:1]) * (0.5 / z)).astype(BF16)


def _route(q, kk):
    n = q.shape[0]
    hps = ROUTE_HEADS_PER_STEP
    tok = pl.BlockSpec((hps, NKEYS, ROUTE_TM), lambda i, g: (g, 0, i))
    rows = jax.ShapeDtypeStruct((PEER_HEADS, NKEYS, n), F32)
    cols = jax.ShapeDtypeStruct((PEER_HEADS, NKEYS, n), BF16)
    top = pltpu.VMEM((TOPK, ROUTE_TM), F32)
    return pl.pallas_call(
        _route_kernel,
        out_shape=(rows, rows, cols, cols),
        grid=(n // ROUTE_TM, PEER_HEADS // hps),
        in_specs=[pl.BlockSpec((ROUTE_TM, hps * LANES), lambda i, g: (i, g)),
                  pl.BlockSpec((hps, 2 * NKEYS, LANES), lambda i, g: (g, 0, 0))],
        out_specs=(tok, tok, tok, tok),
        scratch_shapes=[top, top],
        compiler_params=_cparams("arbitrary", "arbitrary"),
        name="peer_routing",
    )(q, kk)


def _expert_kernel(h_ref, u_ref, vt_ref, r1_ref, e1_ref, c2_ref, e2_ref, x_ref, mod_ref, fw_ref, o_ref, acc, w_scr,
                   *, final_norm):
    eb = pl.program_id(1)

    @pl.when(eb == 0)
    def _():
        acc[...] = jnp.zeros_like(acc)

    hb = h_ref[...]
    pack = BF16_ROWS
    keys_per_sub = EXPERT_SUB // NKEYS
    nsub = EXPERT_BLOCK // EXPERT_SUB

    def pre_act(sub):
        rows = slice(sub * EXPERT_SUB, (sub + 1) * EXPERT_SUB)
        return lax.dot_general(u_ref[0, rows, :], hb, NT, preferred_element_type=F32).astype(BF16)

    def gated(sub, pre):
        act = _twice_gelu_tanh(pre)
        tile = (NKEYS // pack, pack, TT)
        key0 = eb * (EXPERT_BLOCK // NKEYS) + sub * keys_per_sub
        for ii in range(keys_per_sub):
            gate = jnp.zeros(tile, BF16)
            for h in range(PEER_HEADS):
                r1 = jnp.broadcast_to(r1_ref[h, pl.ds(key0 + ii, 1), :], (pack, TT)).astype(BF16)
                e1 = jnp.broadcast_to(e1_ref[h, pl.ds(key0 + ii, 1), :], (pack, TT)).astype(BF16)
                keep = c2_ref[h].reshape(tile) > r1[None]
                gate = gate + jnp.where(keep, e1[None], 0.0) * e2_ref[h].reshape(tile)
            rows = slice(ii * NKEYS, (ii + 1) * NKEYS)
            w_scr[sub * EXPERT_SUB + ii * NKEYS:sub * EXPERT_SUB + (ii + 1) * NKEYS, :] = (
                gate.reshape(NKEYS, TT) * act[rows])

    per_chunk = VALUE_CHUNK // EXPERT_SUB
    pre = pre_act(0)
    out = None
    for sub in range(nsub):
        nxt = pre_act(sub + 1) if sub + 1 < nsub else None
        gated(sub, pre)
        if (sub + 1) % per_chunk == 0:
            ch = sub // per_chunk
            part = jnp.dot(vt_ref[0, ch], w_scr[ch * VALUE_CHUNK:(ch + 1) * VALUE_CHUNK, :],
                           preferred_element_type=F32)
            out = part if out is None else out + part
        pre = nxt
    acc[...] += out

    @pl.when(eb == pl.num_programs(1) - 1)
    def _():
        x = x_ref[...] + mod_ref[0, 5:6, :] * acc[...].T
        if final_norm:
            x = x * lax.rsqrt(jnp.mean(x * x, axis=-1, keepdims=True) + EPS) * fw_ref[...]
        o_ref[...] = x


def _experts(h2, u, vt, layer, route, x, mod, mod_row, final_w, final_norm):
    n = h2.shape[0]
    tok = pl.BlockSpec((PEER_HEADS, NKEYS, TT), lambda i, e: (0, 0, i))
    return pl.pallas_call(
        functools.partial(_expert_kernel, final_norm=final_norm),
        out_shape=jax.ShapeDtypeStruct((n, D), F32),
        grid=(n // TT, u.shape[1] // EXPERT_BLOCK),
        in_specs=[pl.BlockSpec((TT, D), lambda i, e: (i, 0)),
                  pl.BlockSpec((1, EXPERT_BLOCK, D), lambda i, e: (layer, e, 0)),
                  pl.BlockSpec((1, EXPERT_BLOCK // VALUE_CHUNK, D, VALUE_CHUNK), lambda i, e: (layer, e, 0, 0)),
                  tok, tok, tok, tok,
                  pl.BlockSpec((TT, D), lambda i, e: (i, 0)),
                  pl.BlockSpec((1, 6, D), lambda i, e: (mod_row(i), 0, 0)),
                  pl.BlockSpec((1, D), lambda i, e: (0, 0))],
        out_specs=pl.BlockSpec((TT, D), lambda i, e: (i, 0)),
        scratch_shapes=[pltpu.VMEM((D, TT), F32),
                        pltpu.VMEM((EXPERT_BLOCK, TT), BF16)],
        compiler_params=_cparams("arbitrary", "arbitrary"),
        name="peer_experts",
    )(h2, u, vt, *route, x, mod, final_w)


def _vt_kernel(v_ref, o_ref):
    o_ref[0, 0] = v_ref[0].T.astype(BF16)


def _transposed_values(peer_v):
    depth, ne, _ = peer_v.shape
    return pl.pallas_call(
        _vt_kernel,
        out_shape=jax.ShapeDtypeStruct((depth, ne // VALUE_CHUNK, D, VALUE_CHUNK), BF16),
        grid=(depth, ne // VALUE_CHUNK),
        in_specs=[pl.BlockSpec((1, VALUE_CHUNK, D), lambda l, c: (l, c, 0))],
        out_specs=pl.BlockSpec((1, 1, D, VALUE_CHUNK), lambda l, c: (l, c, 0, 0)),
        compiler_params=_cparams("arbitrary", "arbitrary"),
        name="expert_value_layout",
    )(peer_v)


def _layer_weights(l, w_in, w_af, b_af, w_ab, b_ab, w_out, peer_wq, peer_k1, peer_k2):
    o_lr = 4 * GLA_W
    o_conv = o_lr + 2 * LOWRANK
    o_na = o_conv + 3 * CONV_W
    wi = w_in[l]
    wg = wi[:, :o_lr].astype(BF16)
    wl = jnp.tile(wi[:, o_lr:o_conv], (1, 8)).astype(BF16)
    wc = wi[:, o_conv:o_na].astype(BF16)
    wn = wi[:, o_na:].astype(BF16)
    wa = jnp.zeros((2 * LOWRANK, 2 * GLA_W), F32)
    wa = wa.at[:LOWRANK, :GLA_W].set(w_af[l]).at[LOWRANK:, GLA_W:].set(w_ab[l])
    a_hi = wa.astype(BF16)
    a_rest = wa - a_hi.astype(F32)
    a_mid = a_rest.astype(BF16)
    a_lo = (a_rest - a_mid.astype(F32)).astype(BF16)
    zero = jnp.zeros_like(a_hi)
    wab = jnp.concatenate([a_hi, a_mid, a_hi, a_lo, a_hi, a_mid, zero, zero], axis=0)
    bab = jnp.concatenate([b_af[l], b_ab[l]])[None, :]
    wo = w_out[l].astype(BF16)
    wo = (wo[:GLA_W], wo[GLA_W:GLA_W + CONV_W], wo[GLA_W + CONV_W:])
    half = peer_k1.shape[-1]
    kk = jnp.concatenate([jnp.pad(peer_k1[l], ((0, 0), (0, 0), (0, half))),
                          jnp.pad(peer_k2[l], ((0, 0), (0, 0), (half, 0)))], axis=1)
    return (wg, wc, wn, wl, wab, bab), wo, peer_wq[l].astype(BF16), kk


def kernel(x_prompt, x_sample, cache_na_k, cache_na_v, state_gla_fwd, state_gla_bwd, c, c_ctx, w_ada, b_ada, norm1_w, norm2_w, w_in, w_af, b_af, w_ab, b_ab, gla_norm_w, conv_w, conv_b, na_rpb, w_out, peer_wq, peer_k1, peer_k2, peer_u, peer_v, final_norm_w):
    bp, sp, _ = x_prompt.shape
    bs, ss, _ = x_sample.shape
    depth = w_ada.shape[0]
    xp = x_prompt.reshape(bp * sp, D)
    xs = x_sample.reshape(bs * ss, D)

    cvec = jnp.zeros((8, D), F32).at[0].set(c_ctx).at[1:1 + bs].set(c)
    mods = _modulation(cvec, w_ada, b_ada)
    kctx = jnp.swapaxes(cache_na_k, 2, 3).reshape(bs, depth, -1, NA_W)
    vctx = jnp.swapaxes(cache_na_v, 2, 3).reshape(bs, depth, -1, NA_W)

    prompt_row = lambda i: 0
    sample_row = lambda tile: (lambda i: 1 + i // (ss // tile))

    vt = _transposed_values(peer_v)
    u = peer_u.astype(BF16)
    fw = final_norm_w[None, :]
    new_k, new_v, new_sf, new_sb = [], [], [], []
    for l in range(depth):
        last = l == depth - 1
        inw, wo, wq, kk = _layer_weights(l, w_in, w_af, b_af, w_ab, b_ab, w_out, peer_wq, peer_k1, peer_k2)
        mod = mods[l]
        n1, n2 = norm1_w[l][None, :], norm2_w[l][None, :]
        gnw = gla_norm_w[l][None, :]
        cw, cbias = conv_w[l], conv_b[l][None, :]
        bias = _na_bias_table(na_rpb[l])
        s0 = (_states_to_blockdiag(state_gla_fwd[:, l]), _states_to_blockdiag(state_gla_bwd[:, l]))

        zg, la, zc, zn = _inproj(xp, mod, prompt_row, n1, inw)
        yg, sf, sb = _gla(zg, la, gnw, bp, sp, None)
        yc = _conv(zc, cw, cbias, bp, sp)
        yn = _dense_attn(zn, bp, sp)
        xp, h2, q = _outproj(xp, yg, yc, yn, mod, prompt_row, n2, wo, wq)
        xp = _experts(h2, u, vt, l, _route(q, kk), xp, mod, prompt_row, fw, last)
        heads = lambda a: a.reshape(bp, sp, NA_W // HD, HD).transpose(0, 2, 1, 3)
        new_k.append(heads(zn[:, NA_W:2 * NA_W]))
        new_v.append(heads(zn[:, 2 * NA_W:]))
        new_sf.append(_blockdiag_to_states(sf))
        new_sb.append(_blockdiag_to_states(sb))

        zg, la, zc, zn = _inproj(xs, mod, sample_row(TM), n1, inw)
        yg = _gla(zg, la, gnw, bs, ss, s0)
        yc = _conv(zc, cw, cbias, bs, ss)
        yn = _neighbourhood_attn(zn, kctx[:, l:l + 1], vctx[:, l:l + 1], bias, bs, ss)
        xs, h2, q = _outproj(xs, yg, yc, yn, mod, sample_row(TM), n2, wo, wq)
        xs = _experts(h2, u, vt, l, _route(q, kk), xs, mod, sample_row(TT), fw, last)

    y_prompt = xp.reshape(bp, sp, D)
    y_sample = xs.reshape(bs, ss, D)
    return (y_prompt, y_sample, jnp.stack(new_k, axis=1), jnp.stack(new_v, axis=1),
            jnp.stack(new_sf, axis=1), jnp.stack(new_sb, axis=1))
```
